```python
import jax, jax.numpy as jnp
from jax import lax
import numpy as np

D_MODEL = 2048
BATCH = 8
SEQ = 2048
DEPTH = 2
DEC_BATCH = 8
DEC_SEQ = 4096
PAST_LEN = 128

HEAD_DIM = 64
N_BRANCH = 4
DIL_CONFIGS = ((128, 1), (512, 4), (2048, 16))
N_DIL = 3
A_HEADS = 8
B_HEADS = 8
Q_LORA = 384
KV_LORA = 128
QK_NOPE = 64
QK_ROPE = 32
V_DIM = 64
ROPE_THETA = 10000.0
Q_BLOCK = 128
C_HEADS = 8
C_KV_HEADS = 2
C_RADIUS = 128
LRU_WIDTH = 512
LRU_BLOCKS = 8
LRU_BLOCK = 64
CONV_W = 4
CONV_PAD_L = 2
LRU_C = 8.0
D_FF = 5632
EPS = 1e-6
NEG = -1e30

A_COLS = 3 * N_DIL * A_HEADS * HEAD_DIM
B_COLS = Q_LORA + KV_LORA + QK_ROPE
C_COLS = (C_HEADS + 2 * C_KV_HEADS) * HEAD_DIM
D_COLS = 2 * LRU_WIDTH
G_COLS = N_BRANCH * D_MODEL
N_IN = A_COLS + B_COLS + C_COLS + D_COLS + G_COLS
SPLITS = (A_COLS, A_COLS + B_COLS, A_COLS + B_COLS + C_COLS, A_COLS + B_COLS + C_COLS + D_COLS)

kernel_name = 'hybrid_bidir_encoder_gated_branches'


def rms_norm(x, g):
    xf = x.astype(jnp.float32)
    y = xf * lax.rsqrt(jnp.mean(xf * xf, axis=-1, keepdims=True) + EPS)
    return (y * g.astype(jnp.float32)).astype(x.dtype)


def swiglu(x, w1, w3, w2):
    return (jax.nn.silu(x @ w1) * (x @ w3)) @ w2


def alibi_slopes(n):
    return jnp.asarray(2.0 ** (-8.0 * np.arange(1, n + 1) / n), dtype=jnp.float32)


def banded_attention(q, k, v, radius, step, slopes, sink=None):
    n, length, h, dh = q.shape
    g = k.shape[2]
    rep = h // g
    blk = radius
    nb = -(-length // blk)
    pad = nb * blk - length
    qb = jnp.pad(q, ((0, 0), (0, pad), (0, 0), (0, 0))).reshape(n, nb, blk, g, rep, dh)
    kv_pad = ((0, 0), (blk, blk + pad), (0, 0), (0, 0))
    kb = jnp.pad(k, kv_pad).reshape(n, nb + 2, blk, g, dh)
    vb = jnp.pad(v, kv_pad).reshape(n, nb + 2, blk, g, dh)
    kw = jnp.concatenate([kb[:, :-2], kb[:, 1:-1], kb[:, 2:]], axis=2)
    vw = jnp.concatenate([vb[:, :-2], vb[:, 1:-1], vb[:, 2:]], axis=2)
    s = jnp.einsum('nbqgrd,nbkgd->nbgrqk', qb, kw).astype(jnp.float32) * (dh ** -0.5)
    rel = jnp.arange(3 * blk)[None, :] - blk - jnp.arange(blk)[:, None]
    kpos = jnp.arange(nb)[:, None] * blk + jnp.arange(3 * blk)[None, :] - blk
    mask = (jnp.abs(rel) <= radius)[None] & ((kpos >= 0) & (kpos < length))[:, None, :]
    bias = -slopes.reshape(g, rep)[:, :, None, None] * (step * jnp.abs(rel)).astype(jnp.float32)
    s = jnp.where(mask[None, :, None, None], s + bias[None, None], NEG)
    m = jnp.max(s, axis=-1)
    if sink is not None:
        sk = sink.astype(jnp.float32).reshape(g, rep)[None, None, :, :, None]
        m = jnp.maximum(m, sk)
    p = jnp.exp(s - m[..., None])
    l = jnp.sum(p, axis=-1)
    if sink is not None:
        l = l + jnp.exp(sk - m)
    o = jnp.einsum('nbgrqk,nbkgd->nbqgrd', p.astype(v.dtype), vw).astype(jnp.float32)
    o = (o / jnp.moveaxis(l, -1, 2)[..., None]).astype(v.dtype)
    lse = jnp.moveaxis(m + jnp.log(l), -1, 2)
    o = o.reshape(n, nb * blk, h, dh)[:, :length]
    lse = lse.reshape(n, nb * blk, h)[:, :length]
    return o, lse


def to_residue_classes(t, dil):
    b, s = t.shape[:2]
    return t.reshape(b, s // dil, dil, *t.shape[2:]).swapaxes(1, 2).reshape(b * dil, s // dil, *t.shape[2:])


def from_residue_classes(t, b, dil):
    sub = t.shape[1]
    return t.reshape(b, dil, sub, *t.shape[2:]).swapaxes(1, 2).reshape(b, sub * dil, *t.shape[2:])


def dilated_attention(pa):
    b, s = pa.shape[:2]
    qkv = pa.reshape(b, s, 3, N_DIL, A_HEADS, HEAD_DIM)
    slopes = alibi_slopes(A_HEADS)
    outs, lses = [], []
    for gi, (window, dil) in enumerate(DIL_CONFIGS):
        q = to_residue_classes(qkv[:, :, 0, gi], dil)
        k = to_residue_classes(qkv[:, :, 1, gi], dil)
        v = to_residue_classes(qkv[:, :, 2, gi], dil)
        o, lse = banded_attention(q, k, v, window // (2 * dil), dil, slopes)
        outs.append(from_residue_classes(o, b, dil).astype(jnp.float32))
        lses.append(from_residue_classes(lse, b, dil))
    alpha = jax.nn.softmax(jnp.stack(lses, axis=0), axis=0)
    out = jnp.sum(alpha[..., None] * jnp.stack(outs, axis=0), axis=0)
    return out.reshape(b, s, A_HEADS * HEAD_DIM).astype(pa.dtype)


def rope_tables(s):
    inv = ROPE_THETA ** (-jnp.arange(0, QK_ROPE, 2, dtype=jnp.float32) / QK_ROPE)
    ang = jnp.arange(s, dtype=jnp.float32)[:, None] * inv[None, :]
    return jnp.cos(ang), jnp.sin(ang)


def apply_rope(x, cos, sin):
    x1, x2 = jnp.split(x.astype(jnp.float32), 2, axis=-1)
    c, sn = cos[None, :, None, :], sin[None, :, None, :]
    return jnp.concatenate([x1 * c - x2 * sn, x1 * sn + x2 * c], axis=-1).astype(x.dtype)


def mla(pb, q_norm, w_uq, kv_norm, w_ukv):
    b, s = pb.shape[:2]
    cq = pb[..., :Q_LORA]
    ckv = pb[..., Q_LORA:Q_LORA + KV_LORA]
    kr = pb[..., Q_LORA + KV_LORA:]
    q = (rms_norm(cq, q_norm) @ w_uq).reshape(b, s, B_HEADS, QK_NOPE + QK_ROPE)
    kv = (rms_norm(ckv, kv_norm) @ w_ukv).reshape(b, s, B_HEADS, QK_NOPE + V_DIM)
    cos, sin = rope_tables(s)
    q_nope = q[..., :QK_NOPE]
    q_rope = apply_rope(q[..., QK_NOPE:], cos, sin)
    k_nope, v = kv[..., :QK_NOPE], kv[..., QK_NOPE:]
    k_rope = apply_rope(kr[:, :, None, :], cos, sin)[:, :, 0]
    scale = (QK_NOPE + QK_ROPE) ** -0.5
    nb = s // Q_BLOCK
    qn = q_nope.reshape(b, nb, Q_BLOCK, B_HEADS, QK_NOPE).swapaxes(0, 1)
    qr = q_rope.reshape(b, nb, Q_BLOCK, B_HEADS, QK_ROPE).swapaxes(0, 1)

    def query_block(args):
        qn_b, qr_b = args
        sc = (jnp.einsum('bqhd,bkhd->bhqk', qn_b, k_nope)
              + jnp.einsum('bqhr,bkr->bhqk', qr_b, k_rope)).astype(jnp.float32) * scale
        p = jax.nn.softmax(sc, axis=-1)
        return jnp.einsum('bhqk,bkhd->bqhd', p.astype(v.dtype), v)

    o = lax.map(query_block, (qn, qr))
    return o.swapaxes(0, 1).reshape(b, s, B_HEADS * V_DIM)


def windowed_gqa_sink(pc, sink):
    b, s = pc.shape[:2]
    q = pc[..., :C_HEADS * HEAD_DIM].reshape(b, s, C_HEADS, HEAD_DIM)
    k = pc[..., C_HEADS * HEAD_DIM:(C_HEADS + C_KV_HEADS) * HEAD_DIM].reshape(b, s, C_KV_HEADS, HEAD_DIM)
    v = pc[..., (C_HEADS + C_KV_HEADS) * HEAD_DIM:].reshape(b, s, C_KV_HEADS, HEAD_DIM)
    o, _ = banded_attention(q, k, v, C_RADIUS, 1, alibi_slopes(C_HEADS), sink)
    return o.reshape(b, s, C_HEADS * HEAD_DIM)


def lru_combine(c1, c2):
    a1, u1 = c1
    a2, u2 = c2
    return a1 * a2, a2 * u1 + u2


def rg_lru_scan(xc, w_a, b_a, w_x, b_x, lam, reverse):
    b, s, w = xc.shape
    xb = xc.reshape(b, s, LRU_BLOCKS, LRU_BLOCK)
    r = jax.nn.sigmoid(jnp.einsum('bsne,nef->bsnf', xb, w_a).reshape(b, s, w) + b_a)
    i = jax.nn.sigmoid(jnp.einsum('bsne,nef->bsnf', xb, w_x).reshape(b, s, w) + b_x)
    log_a = -LRU_C * r.astype(jnp.float32) * jax.nn.softplus(-lam.astype(jnp.float32))
    a = jnp.exp(log_a)
    u = jnp.sqrt(-jnp.expm1(2.0 * log_a)) * (i * xc).astype(jnp.float32)
    _, hs = lax.associative_scan(lru_combine, (a, u), reverse=reverse, axis=1)
    return hs


def recurrent_branch(pd, conv_w, conv_b, w_a, b_a, w_x, b_x, lam):
    s = pd.shape[1]
    xr, gate = pd[..., :LRU_WIDTH], pd[..., LRU_WIDTH:]
    xp = jnp.pad(xr, ((0, 0), (CONV_PAD_L, CONV_W - 1 - CONV_PAD_L), (0, 0)))
    xc = conv_b
    for tap in range(CONV_W):
        xc = xc + xp[:, tap:tap + s] * conv_w[tap]
    h = (rg_lru_scan(xc, w_a[0], b_a[0], w_x[0], b_x[0], lam[0], False)
         + rg_lru_scan(xc, w_a[1], b_a[1], w_x[1], b_x[1], lam[1], True))
    return h.astype(pd.dtype) * jax.nn.gelu(gate, approximate=True)


def trunk(x, ffn1_norm, ffn1_w1, ffn1_w3, ffn1_w2, mix_norm, w_in, mla_q_norm, mla_w_uq, mla_kv_norm,
          mla_w_ukv, lru_conv_w, lru_conv_b, lru_w_a, lru_b_a, lru_w_x, lru_b_x, lru_lambda, sink_logits,
          w_branch, w_out, ffn2_norm, ffn2_w1, ffn2_w3, ffn2_w2, final_norm):
    for l in range(DEPTH):
        x = x + 0.5 * swiglu(rms_norm(x, ffn1_norm[l]), ffn1_w1[l], ffn1_w3[l], ffn1_w2[l])
        h = rms_norm(x, mix_norm[l])
        pa, pb, pc, pd, pg = jnp.split(h @ w_in[l], SPLITS, axis=-1)
        branches = (
            dilated_attention(pa),
            mla(pb, mla_q_norm[l], mla_w_uq[l], mla_kv_norm[l], mla_w_ukv[l]),
            windowed_gqa_sink(pc, sink_logits[l]),
            recurrent_branch(pd, lru_conv_w[l], lru_conv_b[l], lru_w_a[l], lru_b_a[l], lru_w_x[l],
                             lru_b_x[l], lru_lambda[l]),
        )
        merged = None
        for nbr, y in enumerate(branches):
            gate = jax.nn.sigmoid(pg[..., nbr * D_MODEL:(nbr + 1) * D_MODEL])
            term = gate * (y @ w_branch[l, nbr])
            merged = term if nbr == 0 else merged + term
        x = x + merged @ w_out[l]
        x = x + 0.5 * swiglu(rms_norm(x, ffn2_norm[l]), ffn2_w1[l], ffn2_w3[l], ffn2_w2[l])
    return rms_norm(x, final_norm)


def setup_inputs(seed: int = 0) -> dict:
    key = jax.random.key(seed)
    ks = jax.random.split(key, 32)
    f32 = jnp.float32

    def nrm(k, shape, scale):
        return jax.random.normal(k, shape, f32) * scale

    def gain(k, shape):
        return 1.0 + 0.02 * jax.random.normal(k, shape, f32)

    L = DEPTH
    u = jax.random.uniform(ks[18], (L, 2, LRU_WIDTH), f32, 0.9, 0.999)
    return {
        'x_prompt': nrm(ks[0], (BATCH, SEQ, D_MODEL), 1.0),
        'x_sample': nrm(ks[1], (DEC_BATCH, DEC_SEQ, D_MODEL), 1.0),
        'ffn1_norm': gain(ks[2], (L, D_MODEL)),
        'ffn1_w1': nrm(ks[3], (L, D_MODEL, D_FF), D_MODEL ** -0.5),
        'ffn1_w3': nrm(ks[4], (L, D_MODEL, D_FF), D_MODEL ** -0.5),
        'ffn1_w2': nrm(ks[5], (L, D_FF, D_MODEL), D_FF ** -0.5),
        'mix_norm': gain(ks[6], (L, D_MODEL)),
        'w_in': nrm(ks[7], (L, D_MODEL, N_IN), D_MODEL ** -0.5),
        'mla_q_norm': gain(ks[8], (L, Q_LORA)),
        'mla_w_uq': nrm(ks[9], (L, Q_LORA, B_HEADS * (QK_NOPE + QK_ROPE)), Q_LORA ** -0.5),
        'mla_kv_norm': gain(ks[10], (L, KV_LORA)),
        'mla_w_ukv': nrm(ks[11], (L, KV_LORA, B_HEADS * (QK_NOPE + V_DIM)), KV_LORA ** -0.5),
        'lru_conv_w': nrm(ks[12], (L, CONV_W, LRU_WIDTH), CONV_W ** -0.5),
        'lru_conv_b': nrm(ks[13], (L, LRU_WIDTH), 0.05),
        'lru_w_a': nrm(ks[14], (L, 2, LRU_BLOCKS, LRU_BLOCK, LRU_BLOCK), LRU_BLOCK ** -0.5),
        'lru_b_a': nrm(ks[15], (L, 2, LRU_WIDTH), 0.1),
        'lru_w_x': nrm(ks[16], (L, 2, LRU_BLOCKS, LRU_BLOCK, LRU_BLOCK), LRU_BLOCK ** -0.5),
        'lru_b_x': nrm(ks[17], (L, 2, LRU_WIDTH), 0.1),
        'lru_lambda': jnp.log(u) - jnp.log1p(-u),
        'sink_logits': nrm(ks[19], (L, C_HEADS), 0.5),
        'w_branch': nrm(ks[20], (L, N_BRANCH, 512, D_MODEL), 512 ** -0.5),
        'w_out': nrm(ks[21], (L, D_MODEL, D_MODEL), D_MODEL ** -0.5),
        'ffn2_norm': gain(ks[22], (L, D_MODEL)),
        'ffn2_w1': nrm(ks[23], (L, D_MODEL, D_FF), D_MODEL ** -0.5),
        'ffn2_w3': nrm(ks[24], (L, D_MODEL, D_FF), D_MODEL ** -0.5),
        'ffn2_w2': nrm(ks[25], (L, D_FF, D_MODEL), D_FF ** -0.5),
        'final_norm': gain(ks[26], (D_MODEL,)),
    }


def reference(x_prompt, x_sample, ffn1_norm, ffn1_w1, ffn1_w3, ffn1_w2, mix_norm, w_in, mla_q_norm,
              mla_w_uq, mla_kv_norm, mla_w_ukv, lru_conv_w, lru_conv_b, lru_w_a, lru_b_a, lru_w_x, lru_b_x,
              lru_lambda, sink_logits, w_branch, w_out, ffn2_norm, ffn2_w1, ffn2_w3, ffn2_w2, final_norm):
    weights = (ffn1_norm, ffn1_w1, ffn1_w3, ffn1_w2, mix_norm, w_in, mla_q_norm, mla_w_uq, mla_kv_norm,
               mla_w_ukv, lru_conv_w, lru_conv_b, lru_w_a, lru_b_a, lru_w_x, lru_b_x, lru_lambda, sink_logits,
               w_branch, w_out, ffn2_norm, ffn2_w1, ffn2_w3, ffn2_w2, final_norm)
    y_prompt = trunk(x_prompt, *weights)
    y_sample = trunk(x_sample, *weights)
    return (y_prompt, y_sample)
```

```python
import functools

import numpy as np
import jax
import jax.numpy as jnp
from jax import lax
from jax.experimental import pallas as pl
from jax.experimental.pallas import tpu as pltpu

F32 = jnp.float32
BF16 = jnp.bfloat16

D_MODEL = 2048
DEPTH = 2
HEAD_DIM = 64
N_BRANCH = 4
BRANCH_W = 512
DIL_CONFIGS = ((128, 1), (512, 4), (2048, 16))
N_DIL = 3
A_HEADS = 8
B_HEADS = 8
Q_LORA = 384
KV_LORA = 128
QK_NOPE = 64
QK_ROPE = 32
V_DIM = 64
ROPE_THETA = 10000.0
C_HEADS = 8
C_KV_HEADS = 2
C_RADIUS = 128
LRU_WIDTH = 512
LRU_BLOCKS = 8
LRU_BLOCK = 64
CONV_W = 4
CONV_PAD_L = 2
LRU_C = 8.0
D_FF = 5632
EPS = 1e-6
NEG = -1e30

A_COLS = 3 * N_DIL * A_HEADS * HEAD_DIM
B_COLS = Q_LORA + KV_LORA + QK_ROPE
C_COLS = (C_HEADS + 2 * C_KV_HEADS) * HEAD_DIM
D_COLS = 2 * LRU_WIDTH
G_COLS = N_BRANCH * D_MODEL

LANES = 128
SUBLANES = 8
SLOT = LANES
B_PAD_COLS = Q_LORA + KV_LORA + SLOT
VMEM_LIMIT = 52 * 1024 * 1024


def _cparams(*sem):
    return pltpu.CompilerParams(dimension_semantics=sem, vmem_limit_bytes=VMEM_LIMIT)


def _rms(x, g):
    return x * lax.rsqrt(jnp.mean(x * x, axis=-1, keepdims=True) + EPS) * g


def _pick(n, pref):
    t = min(n, pref)
    while n % t:
        t //= 2
    return t


def _ffn_kernel(x_ref, g_ref, w1_ref, w3_ref, w2_ref, fg_ref, o_ref, h_ref, acc_ref, *, final_norm):
    j = pl.program_id(1)

    @pl.when(j == 0)
    def _():
        h_ref[...] = _rms(x_ref[...], g_ref[...]).astype(BF16)
        acc_ref[...] = jnp.zeros_like(acc_ref)

    h = h_ref[...]
    a = jnp.dot(h, w1_ref[...], preferred_element_type=F32)
    b = jnp.dot(h, w3_ref[...], preferred_element_type=F32)
    act = (a * jax.nn.sigmoid(a) * b).astype(BF16)
    acc_ref[...] += jnp.dot(act, w2_ref[...], preferred_element_type=F32)

    @pl.when(j == pl.num_programs(1) - 1)
    def _():
        y = x_ref[...] + 0.5 * acc_ref[...]
        if final_norm:
            y = _rms(y, fg_ref[...])
        o_ref[...] = y


def _ffn(x, g, w1, w3, w2, fg, final_norm):
    t = x.shape[0]
    tm, tf = _pick(t, 512), 512
    return pl.pallas_call(
        functools.partial(_ffn_kernel, final_norm=final_norm),
        grid=(t // tm, D_FF // tf),
        in_specs=[
            pl.BlockSpec((tm, D_MODEL), lambda i, j: (i, 0)),
            pl.BlockSpec((1, D_MODEL), lambda i, j: (0, 0)),
            pl.BlockSpec((D_MODEL, tf), lambda i, j: (0, j)),
            pl.BlockSpec((D_MODEL, tf), lambda i, j: (0, j)),
            pl.BlockSpec((tf, D_MODEL), lambda i, j: (j, 0)),
            pl.BlockSpec((1, D_MODEL), lambda i, j: (0, 0)),
        ],
        out_specs=pl.BlockSpec((tm, D_MODEL), lambda i, j: (i, 0)),
        out_shape=jax.ShapeDtypeStruct((t, D_MODEL), F32),
        scratch_shapes=[pltpu.VMEM((tm, D_MODEL), BF16), pltpu.VMEM((tm, D_MODEL), F32)],
        compiler_params=_cparams("parallel", "arbitrary"),
        name="ffn",
    )(x, g, w1, w3, w2, fg)


def _norm_mm_kernel(x_ref, g_ref, w_ref, *rest, widths):
    o_refs, h_ref = rest[:-1], rest[-1]

    @pl.when(pl.program_id(1) == 0)
    def _():
        h_ref[...] = _rms(x_ref[...], g_ref[...]).astype(BF16)

    r = jnp.dot(h_ref[...], w_ref[...], preferred_element_type=F32)
    c = 0
    for o_ref, w in zip(o_refs, widths):
        o_ref[...] = r[:, c:c + w].astype(o_ref.dtype)
        c += w


def _norm_mm(x, g, w, out_dtype, tm, tn, widths=None):
    t, n = x.shape[0], w.shape[1]
    tm = _pick(t, tm)
    if widths is None:
        widths = (tn,)
        out_specs = [pl.BlockSpec((tm, tn), lambda i, j: (i, j))]
        out_shape = [jax.ShapeDtypeStruct((t, n), out_dtype)]
    else:
        tn = n
        out_specs = [pl.BlockSpec((tm, wd), lambda i, j: (i, 0)) for wd in widths]
        out_shape = [jax.ShapeDtypeStruct((t, wd), out_dtype) for wd in widths]
    outs = pl.pallas_call(
        functools.partial(_norm_mm_kernel, widths=widths),
        grid=(t // tm, n // tn),
        in_specs=[
            pl.BlockSpec((tm, D_MODEL), lambda i, j: (i, 0)),
            pl.BlockSpec((1, D_MODEL), lambda i, j: (0, 0)),
            pl.BlockSpec((D_MODEL, tn), lambda i, j: (0, j)),
        ],
        out_specs=out_specs,
        out_shape=out_shape,
        scratch_shapes=[pltpu.VMEM((tm, D_MODEL), BF16)],
        compiler_params=_cparams("parallel", "arbitrary"),
        name="norm_proj",
    )(x, g, w)
    return outs[0] if len(outs) == 1 else outs


def _band_kernel(*refs, n_heads, n_kv, radius, step, tq, length, slopes, has_sink, emit_lse):
    if has_sink:
        sink_ref, refs = refs[0], refs[1:]
    q_ref, kp_ref, kc_ref, kn_ref, vp_ref, vc_ref, vn_ref = refs[:7]
    o_ref = refs[7]
    i = pl.program_id(2)
    w = tq + 2 * radius
    kw = jnp.concatenate([kp_ref[0, tq - radius:, :], kc_ref[0], kn_ref[0, :radius, :]], axis=0)
    vw = jnp.concatenate([vp_ref[0, tq - radius:, :], vc_ref[0], vn_ref[0, :radius, :]], axis=0)
    qi = lax.broadcasted_iota(jnp.int32, (tq, w), 0)
    kj = lax.broadcasted_iota(jnp.int32, (tq, w), 1)
    rel = jnp.abs(kj - radius - qi)
    kpos = i * tq - radius + kj
    valid = (rel <= radius) & (kpos >= 0) & (kpos < length)
    dist = (step * rel).astype(F32)
    rep = n_heads // n_kv
    for h in range(n_heads):
        g = h // rep
        qh = q_ref[0, :, h * HEAD_DIM:(h + 1) * HEAD_DIM]
        kh = kw[:, g * HEAD_DIM:(g + 1) * HEAD_DIM]
        vh = vw[:, g * HEAD_DIM:(g + 1) * HEAD_DIM]
        s = lax.dot_general(qh, kh, (((1,), (1,)), ((), ())), preferred_element_type=F32) * (HEAD_DIM ** -0.5)
        s = jnp.where(valid, s - slopes[h] * dist, NEG)
        m = jnp.max(s, axis=-1, keepdims=True)
        if has_sink:
            m = jnp.maximum(m, sink_ref[h])
        p = jnp.exp(s - m)
        l = jnp.sum(p, axis=-1, keepdims=True)
        if has_sink:
            l = l + jnp.exp(sink_ref[h] - m)
        o = jnp.dot(p.astype(BF16), vh, preferred_element_type=F32) / l
        o_ref[0, :, h * HEAD_DIM:(h + 1) * HEAD_DIM] = o.astype(o_ref.dtype)
        if emit_lse:
            refs[8][0, :, h * HEAD_DIM:(h + 1) * HEAD_DIM] = jnp.broadcast_to(m + jnp.log(l), (tq, HEAD_DIM))


def _alibi(n):
    return tuple(float(v) for v in np.asarray(2.0 ** (-8.0 * np.arange(1, n + 1) / n), dtype=np.float32))


def _band_attention(arr, batch, length, n_classes, cols, *, q_col, k_col, v_col, kv_width, n_heads, n_kv,
                    radius, step, sink, out_dtype, emit_lse):
    qw = n_heads * HEAD_DIM
    tq = _pick(length, 256)
    nblk = length // tq
    qpc, kpc = cols // qw, cols // kv_width

    def kv_spec(col, shift):
        def imap(b, r, i):
            return (b, jnp.clip(i + shift, 0, nblk - 1), r * kpc + col)
        return pl.BlockSpec((1, tq, kv_width), imap)

    in_specs = [pl.BlockSpec((1, tq, qw), lambda b, r, i: (b, i, r * qpc + q_col)),
                kv_spec(k_col, -1), kv_spec(k_col, 0), kv_spec(k_col, 1),
                kv_spec(v_col, -1), kv_spec(v_col, 0), kv_spec(v_col, 1)]
    args = [arr] * 7
    if sink is not None:
        in_specs = [pl.BlockSpec(memory_space=pltpu.SMEM)] + in_specs
        args = [sink] + args
    o_spec = pl.BlockSpec((1, tq, qw), lambda b, r, i: (b, i, r))
    o_shape = jax.ShapeDtypeStruct((batch, length, n_classes * qw), out_dtype)
    out_specs, out_shape = [o_spec], [o_shape]
    if emit_lse:
        out_specs, out_shape = [o_spec, o_spec], [o_shape, jax.ShapeDtypeStruct(o_shape.shape, F32)]
    return pl.pallas_call(
        functools.partial(_band_kernel, n_heads=n_heads, n_kv=n_kv, radius=radius, step=step, tq=tq,
                          length=length, slopes=_alibi(n_heads), has_sink=sink is not None, emit_lse=emit_lse),
        grid=(batch, n_classes, nblk),
        in_specs=in_specs,
        out_specs=out_specs,
        out_shape=out_shape,
        compiler_params=_cparams("parallel", "parallel", "arbitrary"),
        name="band_attn",
    )(*args)


def _rope_slot(x, cos_t, sin_t):
    lane = lax.broadcasted_iota(jnp.int32, x.shape, 1)
    first = (lane >= QK_NOPE) & (lane < QK_NOPE + QK_ROPE // 2)
    partner = jnp.where(first, pltpu.roll(x, SLOT - QK_ROPE // 2, 1), pltpu.roll(x, QK_ROPE // 2, 1))
    return x * cos_t + partner * sin_t


def _mla_prep_kernel(pb_ref, qn_ref, kvn_ref, wq_ref, wk_ref, wv_ref, cq_ref, sq_ref, ck_ref, sk_ref,
                     q_out, k_out, v_out):
    pb = pb_ref[...]
    cq = _rms(pb[:, :Q_LORA], qn_ref[...]).astype(BF16)
    ckv = _rms(pb[:, Q_LORA:Q_LORA + KV_LORA], kvn_ref[...]).astype(BF16)
    kr = pb[:, Q_LORA + KV_LORA:]
    q = jnp.dot(cq, wq_ref[...], preferred_element_type=F32)
    k = jnp.dot(ckv, wk_ref[...], preferred_element_type=F32)
    v_out[...] = jnp.dot(ckv, wv_ref[...], preferred_element_type=F32).astype(BF16)
    kr = _rope_slot(kr, ck_ref[...], sk_ref[...])
    for h in range(B_HEADS):
        sl = slice(h * SLOT, (h + 1) * SLOT)
        q_out[:, sl] = _rope_slot(q[:, sl], cq_ref[...], sq_ref[...]).astype(BF16)
        k_out[:, sl] = (k[:, sl] + kr).astype(BF16)


def _mla_prep(pb, qn, kvn, wq, wk, wv, tabs, seq):
    t = pb.shape[0]
    tm = _pick(seq, 512)
    nseq = seq // tm
    row = lambda i: (i, 0)
    fixed = lambda i: (0, 0)
    tab = pl.BlockSpec((tm, SLOT), lambda i: (i % nseq, 0))
    return pl.pallas_call(
        _mla_prep_kernel,
        grid=(t // tm,),
        in_specs=[pl.BlockSpec((tm, B_PAD_COLS), row),
                  pl.BlockSpec((1, Q_LORA), fixed), pl.BlockSpec((1, KV_LORA), fixed),
                  pl.BlockSpec((Q_LORA, B_HEADS * SLOT), fixed),
                  pl.BlockSpec((KV_LORA, B_HEADS * SLOT), fixed),
                  pl.BlockSpec((KV_LORA, B_HEADS * V_DIM), fixed),
                  tab, tab, tab, tab],
        out_specs=[pl.BlockSpec((tm, B_HEADS * SLOT), row), pl.BlockSpec((tm, B_HEADS * SLOT), row),
                   pl.BlockSpec((tm, B_HEADS * V_DIM), row)],
        out_shape=[jax.ShapeDtypeStruct((t, B_HEADS * SLOT), BF16), jax.ShapeDtypeStruct((t, B_HEADS * SLOT), BF16),
                   jax.ShapeDtypeStruct((t, B_HEADS * V_DIM), BF16)],
        compiler_params=_cparams("parallel"),
        name="mla_prep",
    )(pb, qn, kvn, wq, wk, wv, *tabs)


def _mla_attn_kernel(q_ref, k_ref, v_ref, o_ref):
    v = v_ref[0]
    outs = []
    for hh in range(2):
        sl = slice(hh * SLOT, (hh + 1) * SLOT)
        s = lax.dot_general(q_ref[0, :, sl], k_ref[0, :, sl], (((1,), (1,)), ((), ())),
                            preferred_element_type=F32)
        m = jnp.max(s, axis=-1, keepdims=True)
        p = jnp.exp(s - m)
        l = jnp.sum(p, axis=-1, keepdims=True)
        outs.append(jnp.dot(p.astype(BF16), v, preferred_element_type=F32) / l)
    lane = lax.broadcasted_iota(jnp.int32, outs[0].shape, 1)
    o_ref[0] = jnp.where(lane < V_DIM, outs[0], outs[1]).astype(o_ref.dtype)


def _mla_attn(q, k, v, batch, seq):
    tq = _pick(seq, 256)
    pairs = B_HEADS // 2
    return pl.pallas_call(
        _mla_attn_kernel,
        grid=(batch, pairs, seq // tq),
        in_specs=[pl.BlockSpec((1, tq, 2 * SLOT), lambda b, hp, i: (b, i, hp)),
                  pl.BlockSpec((1, seq, 2 * SLOT), lambda b, hp, i: (b, 0, hp)),
                  pl.BlockSpec((1, seq, 2 * V_DIM), lambda b, hp, i: (b, 0, hp))],
        out_specs=pl.BlockSpec((1, tq, 2 * V_DIM), lambda b, hp, i: (b, i, hp)),
        out_shape=jax.ShapeDtypeStruct((batch, seq, B_HEADS * V_DIM), BF16),
        compiler_params=_cparams("parallel", "parallel", "arbitrary"),
        name="mla_attn",
    )(q, k, v)


def _softplus(x):
    return jnp.maximum(x, 0.0) + jnp.log1p(jnp.exp(-jnp.abs(x)))


def _lru_gates_kernel(xc_ref, xp_ref, xn_ref, cw_ref, cb_ref, wg_ref, bg_ref, lam_ref,
                      af_ref, uf_ref, ar_ref, ur_ref, ext_ref, *, tm):
    i = pl.program_id(1)
    halo = SUBLANES
    ext_ref[0:halo, :] = jnp.where(i > 0, xp_ref[0], 0.0)
    ext_ref[halo:halo + tm, :] = xc_ref[0]
    ext_ref[halo + tm:, :] = jnp.where(i < pl.num_programs(1) - 1, xn_ref[0], 0.0)
    xc = cb_ref[...]
    for tap in range(CONV_W):
        off = halo - CONV_PAD_L + tap
        xc = xc + ext_ref[off:off + tm, :] * cw_ref[tap:tap + 1, :]
    gates = jnp.dot(xc.astype(BF16), wg_ref[...], preferred_element_type=F32) + bg_ref[...]
    for d, (a_ref, u_ref) in enumerate(((af_ref, uf_ref), (ar_ref, ur_ref))):
        base = 2 * d * LRU_WIDTH
        r = jax.nn.sigmoid(gates[:, base:base + LRU_WIDTH])
        ig = jax.nn.sigmoid(gates[:, base + LRU_WIDTH:base + 2 * LRU_WIDTH])
        log_a = -LRU_C * r * _softplus(-lam_ref[d:d + 1, :])
        a_ref[0] = jnp.exp(log_a)
        th = jnp.tanh(log_a)
        u_ref[0] = jnp.sqrt(-2.0 * th / (1.0 - th)) * (ig * xc)


def _lru_gates(pd, cw, cb, wg, bg, lam, batch, seq):
    tm = _pick(seq, 512)
    nt = seq // tm
    per = tm // SUBLANES
    last8 = seq // SUBLANES - 1
    blk = pl.BlockSpec((1, tm, LRU_WIDTH), lambda b, i: (b, i, 0))
    fixed = lambda b, i: (0, 0)
    shp = jax.ShapeDtypeStruct((batch, seq, LRU_WIDTH), F32)
    return pl.pallas_call(
        functools.partial(_lru_gates_kernel, tm=tm),
        grid=(batch, nt),
        in_specs=[blk,
                  pl.BlockSpec((1, SUBLANES, LRU_WIDTH), lambda b, i: (b, jnp.maximum(i * per - 1, 0), 0)),
                  pl.BlockSpec((1, SUBLANES, LRU_WIDTH), lambda b, i: (b, jnp.minimum((i + 1) * per, last8), 0)),
                  pl.BlockSpec((CONV_W, LRU_WIDTH), fixed), pl.BlockSpec((1, LRU_WIDTH), fixed),
                  pl.BlockSpec((LRU_WIDTH, 4 * LRU_WIDTH), fixed), pl.BlockSpec((1, 4 * LRU_WIDTH), fixed),
                  pl.BlockSpec((2, LRU_WIDTH), fixed)],
        out_specs=[blk, blk, blk, blk],
        out_shape=[shp, shp, shp, shp],
        scratch_shapes=[pltpu.VMEM((tm + 2 * SUBLANES, LRU_WIDTH), F32)],
        compiler_params=_cparams("parallel", "arbitrary"),
        name="lru_gates",
    )(pd, pd, pd, cw, cb, wg, bg, lam)


def _lru_scan_kernel(af_ref, uf_ref, ar_ref, ur_ref, hf_ref, hr_ref, cf_ref, cr_ref, *, ts):
    @pl.when(pl.program_id(1) == 0)
    def _():
        cf_ref[...] = jnp.zeros_like(cf_ref)
        cr_ref[...] = jnp.zeros_like(cr_ref)

    ng = ts // SUBLANES
    row = lax.broadcasted_iota(jnp.int32, (SUBLANES, LRU_WIDTH), 0)

    def group_scan(a, u, carry, reverse):
        for d in (1, 2, 4):
            if reverse:
                keep, sh = row < SUBLANES - d, SUBLANES - d
            else:
                keep, sh = row >= d, d
            ap = jnp.where(keep, pltpu.roll(a, sh, 0), 1.0)
            up = jnp.where(keep, pltpu.roll(u, sh, 0), 0.0)
            u = a * up + u
            a = a * ap
        return u + a * carry

    def body(g, carry):
        cf, cr = carry
        r0 = pl.multiple_of(g * SUBLANES, SUBLANES)
        hf = group_scan(af_ref[0, pl.ds(r0, SUBLANES), :], uf_ref[0, pl.ds(r0, SUBLANES), :], cf, False)
        hf_ref[0, pl.ds(r0, SUBLANES), :] = hf
        r1 = pl.multiple_of((ng - 1 - g) * SUBLANES, SUBLANES)
        hr = group_scan(ar_ref[0, pl.ds(r1, SUBLANES), :], ur_ref[0, pl.ds(r1, SUBLANES), :], cr, True)
        hr_ref[0, pl.ds(r1, SUBLANES), :] = hr
        return (jnp.broadcast_to(hf[SUBLANES - 1:, :], hf.shape), jnp.broadcast_to(hr[:1, :], hr.shape))

    cf, cr = lax.fori_loop(0, ng, body, (cf_ref[...], cr_ref[...]))
    cf_ref[...] = cf
    cr_ref[...] = cr


def _lru_scan(af, uf, ar, ur):
    batch, seq, _ = af.shape
    ts = _pick(seq, 512)
    nt = seq // ts
    fwd = pl.BlockSpec((1, ts, LRU_WIDTH), lambda b, i: (b, i, 0))
    rev = pl.BlockSpec((1, ts, LRU_WIDTH), lambda b, i: (b, nt - 1 - i, 0))
    shp = jax.ShapeDtypeStruct(af.shape, F32)
    return pl.pallas_call(
        functools.partial(_lru_scan_kernel, ts=ts),
        grid=(batch, nt),
        in_specs=[fwd, fwd, rev, rev],
        out_specs=[fwd, rev],
        out_shape=[shp, shp],
        scratch_shapes=[pltpu.VMEM((SUBLANES, LRU_WIDTH), F32), pltpu.VMEM((SUBLANES, LRU_WIDTH), F32)],
        compiler_params=_cparams("parallel", "arbitrary"),
        name="lru_scan",
    )(af, uf, ar, ur)


def _gelu_tanh(x):
    return 0.5 * x * (1.0 + jnp.tanh(np.sqrt(2.0 / np.pi).astype(np.float32) * (x + 0.044715 * (x * x * x))))


def _merge_kernel(o0, o1, o2, l0, l1, l2, yb_ref, yc_ref, hf_ref, hr_ref, gd_ref,
                  g0, g1, g2, g3, wb_ref, out_ref, y_ref):
    @pl.when(pl.program_id(1) == 0)
    def _():
        lse = (l0[...], l1[...], l2[...])
        m = jnp.maximum(jnp.maximum(lse[0], lse[1]), lse[2])
        e = [jnp.exp(v - m) for v in lse]
        tot = e[0] + e[1] + e[2]
        ya = (e[0] / tot) * o0[...] + (e[1] / tot) * o1[...] + (e[2] / tot) * o2[...]
        y_ref[0] = ya.astype(BF16)
        y_ref[1] = yb_ref[...]
        y_ref[2] = yc_ref[...]
        y_ref[3] = ((hf_ref[...] + hr_ref[...]) * _gelu_tanh(gd_ref[...])).astype(BF16)

    acc = None
    for nbr, g_ref in enumerate((g0, g1, g2, g3)):
        term = jax.nn.sigmoid(g_ref[...]) * jnp.dot(y_ref[nbr], wb_ref[nbr], preferred_element_type=F32)
        acc = term if acc is None else acc + term
    out_ref[...] = acc.astype(out_ref.dtype)


def _merge(oa, la, yb, yc, hf, hr, pd, pg, wb):
    t = yb.shape[0]
    tm, tn = _pick(t, 256), 512
    nj = D_MODEL // tn
    blk = pl.BlockSpec((tm, BRANCH_W), lambda i, j: (i, 0))
    gate_specs = [pl.BlockSpec((tm, tn), functools.partial(lambda i, j, nbr: (i, nbr * nj + j), nbr=nbr))
                  for nbr in range(N_BRANCH)]
    return pl.pallas_call(
        _merge_kernel,
        grid=(t // tm, nj),
        in_specs=[blk] * 10 + [pl.BlockSpec((tm, BRANCH_W), lambda i, j: (i, 1))] + gate_specs
                 + [pl.BlockSpec((N_BRANCH, BRANCH_W, tn), lambda i, j: (0, 0, j))],
        out_specs=pl.BlockSpec((tm, tn), lambda i, j: (i, j)),
        out_shape=jax.ShapeDtypeStruct((t, D_MODEL), BF16),
        scratch_shapes=[pltpu.VMEM((N_BRANCH, tm, BRANCH_W), BF16)],
        compiler_params=_cparams("parallel", "arbitrary"),
        name="merge",
    )(*oa, *la, yb, yc, hf, hr, pd, pg, pg, pg, pg, wb)


def _mm_res_kernel(a_ref, w_ref, x_ref, o_ref):
    o_ref[...] = x_ref[...] + jnp.dot(a_ref[...], w_ref[...], preferred_element_type=F32)


def _mm_res(a, w, x):
    t, k = a.shape
    n = w.shape[1]
    tm, tn = _pick(t, 512), 1024
    return pl.pallas_call(
        _mm_res_kernel,
        grid=(t // tm, n // tn),
        in_specs=[pl.BlockSpec((tm, k), lambda i, j: (i, 0)),
                  pl.BlockSpec((k, tn), lambda i, j: (0, j)),
                  pl.BlockSpec((tm, tn), lambda i, j: (i, j))],
        out_specs=pl.BlockSpec((tm, tn), lambda i, j: (i, j)),
        out_shape=jax.ShapeDtypeStruct((t, n), F32),
        compiler_params=_cparams("parallel", "arbitrary"),
        name="out_proj",
    )(a, w, x)


def _block_diag(w):
    eye = jnp.eye(LRU_BLOCKS, dtype=w.dtype)
    return jnp.einsum('nef,nm->nemf', w, eye).reshape(LRU_WIDTH, LRU_WIDTH)


def _prep_layer(l, ffn1_norm, ffn1_w1, ffn1_w3, ffn1_w2, mix_norm, w_in, mla_q_norm, mla_w_uq, mla_kv_norm,
                mla_w_ukv, lru_conv_w, lru_conv_b, lru_w_a, lru_b_a, lru_w_x, lru_b_x, lru_lambda, sink_logits,
                w_branch, w_out, ffn2_norm, ffn2_w1, ffn2_w3, ffn2_w2):
    row = lambda v: v.reshape(1, -1)
    wi = w_in[l]
    c0, c1, c2, c3 = A_COLS, A_COLS + B_COLS, A_COLS + B_COLS + C_COLS, A_COLS + B_COLS + C_COLS + D_COLS
    w_a = wi[:, :c0].reshape(D_MODEL, 3, N_DIL, A_HEADS * HEAD_DIM).transpose(0, 2, 1, 3).reshape(D_MODEL, A_COLS)
    wb = wi[:, c0:c1]
    zeros = lambda n: jnp.zeros((D_MODEL, n), wi.dtype)
    w_b = jnp.concatenate([wb[:, :Q_LORA + KV_LORA], zeros(QK_NOPE), wb[:, Q_LORA + KV_LORA:],
                           zeros(SLOT - QK_NOPE - QK_ROPE)], axis=1)
    w_bd = jnp.concatenate([w_b, wi[:, c2:c3]], axis=1)
    wq = mla_w_uq[l].reshape(Q_LORA, B_HEADS, QK_NOPE + QK_ROPE)
    wq = jnp.pad(wq, ((0, 0), (0, 0), (0, SLOT - QK_NOPE - QK_ROPE))).reshape(Q_LORA, B_HEADS * SLOT)
    wkv = mla_w_ukv[l].reshape(KV_LORA, B_HEADS, QK_NOPE + V_DIM)
    wk = jnp.pad(wkv[:, :, :QK_NOPE], ((0, 0), (0, 0), (0, SLOT - QK_NOPE))).reshape(KV_LORA, B_HEADS * SLOT)
    wv = wkv[:, :, QK_NOPE:].reshape(KV_LORA, B_HEADS * V_DIM)
    wg = jnp.concatenate([_block_diag(lru_w_a[l, 0]), _block_diag(lru_w_x[l, 0]),
                          _block_diag(lru_w_a[l, 1]), _block_diag(lru_w_x[l, 1])], axis=1)
    bg = jnp.concatenate([lru_b_a[l, 0], lru_b_x[l, 0], lru_b_a[l, 1], lru_b_x[l, 1]]).reshape(1, -1)
    return dict(
        ffn1=(row(ffn1_norm[l]), ffn1_w1[l].astype(BF16), ffn1_w3[l].astype(BF16), ffn1_w2[l].astype(BF16)),
        ffn2=(row(ffn2_norm[l]), ffn2_w1[l].astype(BF16), ffn2_w3[l].astype(BF16), ffn2_w2[l].astype(BF16)),
        mix_norm=row(mix_norm[l]),
        w_a=w_a.astype(BF16), w_c=wi[:, c1:c2].astype(BF16), w_bd=w_bd.astype(BF16), w_g=wi[:, c3:].astype(BF16),
        qn=row(mla_q_norm[l]), kvn=row(mla_kv_norm[l]),
        wq=wq.astype(BF16), wk=wk.astype(BF16), wv=wv.astype(BF16),
        conv_w=lru_conv_w[l], conv_b=row(lru_conv_b[l]), wg=wg.astype(BF16), bg=bg, lam=lru_lambda[l],
        sink=sink_logits[l], w_branch=w_branch[l].astype(BF16), w_out=w_out[l].astype(BF16),
    )


def _rope_slot_tables(seq):
    inv = ROPE_THETA ** (-jnp.arange(0, QK_ROPE, 2, dtype=F32) / QK_ROPE)
    ang = jnp.arange(seq, dtype=F32)[:, None] * inv[None, :]
    cos, sin = jnp.cos(ang), jnp.sin(ang)
    scale = (QK_NOPE + QK_ROPE) ** -0.5
    z = lambda n: jnp.zeros((seq, n), F32)
    tail = SLOT - QK_NOPE - QK_ROPE
    cos_q = jnp.concatenate([jnp.full((seq, QK_NOPE), scale, F32), cos * scale, cos * scale, z(tail)], axis=1)
    sin_q = jnp.concatenate([z(QK_NOPE), -sin * scale, sin * scale, z(tail)], axis=1)
    cos_k = jnp.concatenate([z(QK_NOPE), cos, cos, z(tail)], axis=1)
    sin_k = jnp.concatenate([z(QK_NOPE), -sin, sin, z(tail)], axis=1)
    return cos_q, sin_q, cos_k, sin_k


def _layer(x, w, batch, seq, final_g):
    t = batch * seq
    x = _ffn(x, *w['ffn1'], final_g, False)
    pa = _norm_mm(x, w['mix_norm'], w['w_a'], BF16, 1024, 1536)
    pc = _norm_mm(x, w['mix_norm'], w['w_c'], BF16, 1024, C_COLS)
    pb, pd = _norm_mm(x, w['mix_norm'], w['w_bd'], F32, 512, None, widths=(B_PAD_COLS, D_COLS))
    pg = _norm_mm(x, w['mix_norm'], w['w_g'], F32, 1024, 1024)

    oa, la = [], []
    group_cols = 3 * A_HEADS * HEAD_DIM
    for gi, (window, dil) in enumerate(DIL_CONFIGS):
        sub = seq // dil
        o, lse = _band_attention(
            pa.reshape(batch, sub, dil * A_COLS), batch, sub, dil, A_COLS,
            q_col=3 * gi, k_col=3 * gi + 1, v_col=3 * gi + 2, kv_width=A_HEADS * HEAD_DIM,
            n_heads=A_HEADS, n_kv=A_HEADS, radius=window // (2 * dil), step=dil, sink=None,
            out_dtype=F32, emit_lse=True)
        oa.append(o.reshape(t, BRANCH_W))
        la.append(lse.reshape(t, BRANCH_W))
    del group_cols

    q, k, v = _mla_prep(pb, w['qn'], w['kvn'], w['wq'], w['wk'], w['wv'], _rope_slot_tables(seq), seq)
    yb = _mla_attn(q.reshape(batch, seq, -1), k.reshape(batch, seq, -1), v.reshape(batch, seq, -1), batch, seq)

    kvw = C_KV_HEADS * HEAD_DIM
    yc = _band_attention(
        pc.reshape(batch, seq, C_COLS), batch, seq, 1, C_COLS,
        q_col=0, k_col=C_HEADS * HEAD_DIM // kvw, v_col=C_HEADS * HEAD_DIM // kvw + 1, kv_width=kvw,
        n_heads=C_HEADS, n_kv=C_KV_HEADS, radius=C_RADIUS, step=1, sink=w['sink'],
        out_dtype=BF16, emit_lse=False)[0]

    af, uf, ar, ur = _lru_gates(pd.reshape(batch, seq, D_COLS), w['conv_w'], w['conv_b'], w['wg'], w['bg'],
                                w['lam'], batch, seq)
    hf, hr = _lru_scan(af, uf, ar, ur)

    merged = _merge(oa, la, yb.reshape(t, BRANCH_W), yc.reshape(t, BRANCH_W), hf.reshape(t, LRU_WIDTH),
                    hr.reshape(t, LRU_WIDTH), pd, pg, w['w_branch'])
    x = _mm_res(merged, w['w_out'], x)
    return _ffn(x, *w['ffn2'], final_g, final_g is not None and w.get('last', False))


def _trunk(x, layers, final_norm):
    batch, seq, _ = x.shape
    h = x.reshape(batch * seq, D_MODEL)
    fg = final_norm.reshape(1, -1)
    for l, w in enumerate(layers):
        h = _layer(h, dict(w, last=(l == len(layers) - 1)), batch, seq, fg)
    return h.reshape(batch, seq, D_MODEL)


def kernel(x_prompt, x_sample, ffn1_norm, ffn1_w1, ffn1_w3, ffn1_w2, mix_norm, w_in, mla_q_norm, mla_w_uq, mla_kv_norm, mla_w_ukv, lru_conv_w, lru_conv_b, lru_w_a, lru_b_a, lru_w_x, lru_b_x, lru_lambda, sink_logits, w_branch, w_out, ffn2_norm, ffn2_w1, ffn2_w3, ffn2_w2, final_norm):
    layers = [_prep_layer(l, ffn1_norm, ffn1_w1, ffn1_w3, ffn1_w2, mix_norm, w_in, mla_q_norm, mla_w_uq,
                          mla_kv_norm, mla_w_ukv, lru_conv_w, lru_conv_b, lru_w_a, lru_b_a, lru_w_x, lru_b_x,
                          lru_lambda, sink_logits, w_branch, w_out, ffn2_norm, ffn2_w1, ffn2_w3, ffn2_w2)
              for l in range(DEPTH)]
    return (_trunk(x_prompt, layers, final_norm), _trunk(x_sample, layers, final_norm))
```

```python
import functools

import numpy as np
import jax
import jax.numpy as jnp
from jax import lax
from jax.experimental import pallas as pl
from jax.experimental.pallas import tpu as pltpu

F32 = jnp.float32
BF16 = jnp.bfloat16

D_MODEL = 2048
DEPTH = 2
HEAD_DIM = 64
N_BRANCH = 4
BRANCH_W = 512
DIL_CONFIGS = ((128, 1), (512, 4), (2048, 16))
N_DIL = 3
A_HEADS = 8
B_HEADS = 8
Q_LORA = 384
KV_LORA = 128
QK_NOPE = 64
QK_ROPE = 32
V_DIM = 64
ROPE_THETA = 10000.0
C_HEADS = 8
C_KV_HEADS = 2
C_RADIUS = 128
LRU_WIDTH = 512
LRU_BLOCKS = 8
LRU_BLOCK = 64
CONV_W = 4
CONV_PAD_L = 2
LRU_C = 8.0
D_FF = 5632
EPS = 1e-6
NEG = -1e30

A_COLS = 3 * N_DIL * A_HEADS * HEAD_DIM
B_COLS = Q_LORA + KV_LORA + QK_ROPE
C_COLS = (C_HEADS + 2 * C_KV_HEADS) * HEAD_DIM
D_COLS = 2 * LRU_WIDTH
G_COLS = N_BRANCH * D_MODEL

LANES = 128
SUBLANES = 8
SLOT = LANES
B_PAD_COLS = Q_LORA + KV_LORA + SLOT
VMEM_LIMIT = 52 * 1024 * 1024


def _cparams(*sem):
    return pltpu.CompilerParams(dimension_semantics=sem, vmem_limit_bytes=VMEM_LIMIT)


def _rms(x, g):
    return x * lax.rsqrt(jnp.mean(x * x, axis=-1, keepdims=True) + EPS) * g


def _pick(n, pref):
    t = min(n, pref)
    while n % t:
        t //= 2
    return t


def _ffn_kernel(x_ref, g_ref, w1_ref, w3_ref, w2_ref, fg_ref, o_ref, h_ref, acc_ref, *, final_norm):
    j = pl.program_id(1)

    @pl.when(j == 0)
    def _():
        h_ref[...] = _rms(x_ref[...], g_ref[...]).astype(BF16)
        acc_ref[...] = jnp.zeros_like(acc_ref)

    h = h_ref[...]
    a = jnp.dot(h, w1_ref[...], preferred_element_type=F32)
    b = jnp.dot(h, w3_ref[...], preferred_element_type=F32)
    act = (a * jax.nn.sigmoid(a) * b).astype(BF16)
    acc_ref[...] += jnp.dot(act, w2_ref[...], preferred_element_type=F32)

    @pl.when(j == pl.num_programs(1) - 1)
    def _():
        y = x_ref[...] + 0.5 * acc_ref[...]
        if final_norm:
            y = _rms(y, fg_ref[...])
        o_ref[...] = y


def _ffn(x, g, w1, w3, w2, fg, final_norm):
    t = x.shape[0]
    tm, tf = _pick(t, 512), 512
    return pl.pallas_call(
        functools.partial(_ffn_kernel, final_norm=final_norm),
        grid=(t // tm, D_FF // tf),
        in_specs=[
            pl.BlockSpec((tm, D_MODEL), lambda i, j: (i, 0)),
            pl.BlockSpec((1, D_MODEL), lambda i, j: (0, 0)),
            pl.BlockSpec((D_MODEL, tf), lambda i, j: (0, j)),
            pl.BlockSpec((D_MODEL, tf), lambda i, j: (0, j)),
            pl.BlockSpec((tf, D_MODEL), lambda i, j: (j, 0)),
            pl.BlockSpec((1, D_MODEL), lambda i, j: (0, 0)),
        ],
        out_specs=pl.BlockSpec((tm, D_MODEL), lambda i, j: (i, 0)),
        out_shape=jax.ShapeDtypeStruct((t, D_MODEL), F32),
        scratch_shapes=[pltpu.VMEM((tm, D_MODEL), BF16), pltpu.VMEM((tm, D_MODEL), F32)],
        compiler_params=_cparams("parallel", "arbitrary"),
        name="ffn",
    )(x, g, w1, w3, w2, fg)


def _norm_mm_kernel(x_ref, g_ref, w_ref, *rest, widths):
    o_refs, h_ref = rest[:-1], rest[-1]

    @pl.when(pl.program_id(1) == 0)
    def _():
        h_ref[...] = _rms(x_ref[...], g_ref[...]).astype(BF16)

    r = jnp.dot(h_ref[...], w_ref[...], preferred_element_type=F32)
    c = 0
    for o_ref, w in zip(o_refs, widths):
        o_ref[...] = r[:, c:c + w].astype(o_ref.dtype)
        c += w


def _norm_mm(x, g, w, out_dtype, tm, tn, widths=None):
    t, n = x.shape[0], w.shape[1]
    tm = _pick(t, tm)
    if widths is None:
        widths = (tn,)
        out_specs = [pl.BlockSpec((tm, tn), lambda i, j: (i, j))]
        out_shape = [jax.ShapeDtypeStruct((t, n), out_dtype)]
    else:
        tn = n
        out_specs = [pl.BlockSpec((tm, wd), lambda i, j: (i, 0)) for wd in widths]
        out_shape = [jax.ShapeDtypeStruct((t, wd), out_dtype) for wd in widths]
    outs = pl.pallas_call(
        functools.partial(_norm_mm_kernel, widths=widths),
        grid=(t // tm, n // tn),
        in_specs=[
            pl.BlockSpec((tm, D_MODEL), lambda i, j: (i, 0)),
            pl.BlockSpec((1, D_MODEL), lambda i, j: (0, 0)),
            pl.BlockSpec((D_MODEL, tn), lambda i, j: (0, j)),
        ],
        out_specs=out_specs,
        out_shape=out_shape,
        scratch_shapes=[pltpu.VMEM((tm, D_MODEL), BF16)],
        compiler_params=_cparams("parallel", "arbitrary"),
        name="norm_proj",
    )(x, g, w)
    return outs[0] if len(outs) == 1 else outs


def _proj_a_kernel(x_ref, g_ref, w_ref, o1_ref, o2_ref, o3_ref, hn_ref, h_ref, *, tm):
    j = pl.program_id(1)
    chunks = D_MODEL // LANES

    @pl.when(j == 0)
    def _():
        hn = _rms(x_ref[...], g_ref[...])
        for c in range(chunks):
            hn_ref[c] = hn[:, c * LANES:(c + 1) * LANES]
        h_ref[...] = hn.astype(BF16)

    for g, o_ref in enumerate((o1_ref, o2_ref, o3_ref)):
        dil = DIL_CONFIGS[g][1]
        n = tm // dil

        @pl.when(j == g)
        def _(o_ref=o_ref, dil=dil, n=n):
            if dil > 1:
                for r in range(dil):
                    for c in range(chunks):
                        h_ref[r * n:(r + 1) * n, c * LANES:(c + 1) * LANES] = (
                            hn_ref[c, pl.ds(r, n, stride=dil), :].astype(BF16))
            res = jnp.dot(h_ref[...], w_ref[...], preferred_element_type=F32)
            for r in range(dil):
                o_ref[0, r] = res[r * n:(r + 1) * n].astype(BF16)


def _proj_a(x, g, w, batch, seq):
    t = x.shape[0]
    tm = _pick(seq, 512)
    nseq = seq // tm
    gw = A_COLS // N_DIL
    out_specs = [pl.BlockSpec((1, dil, tm // dil, gw), lambda i, j: (i // nseq, 0, i % nseq, 0))
                 for _, dil in DIL_CONFIGS]
    out_shape = [jax.ShapeDtypeStruct((batch, dil, seq // dil, gw), BF16) for _, dil in DIL_CONFIGS]
    return pl.pallas_call(
        functools.partial(_proj_a_kernel, tm=tm),
        grid=(t // tm, N_DIL),
        in_specs=[pl.BlockSpec((tm, D_MODEL), lambda i, j: (i, 0)),
                  pl.BlockSpec((1, D_MODEL), lambda i, j: (0, 0)),
                  pl.BlockSpec((D_MODEL, gw), lambda i, j: (0, j))],
        out_specs=out_specs,
        out_shape=out_shape,
        scratch_shapes=[pltpu.VMEM((D_MODEL // LANES, tm, LANES), F32), pltpu.VMEM((tm, D_MODEL), BF16)],
        compiler_params=_cparams("parallel", "arbitrary"),
        name="proj_a",
    )(x, g, w)


def _band_kernel(*refs, n_heads, n_kv, radius, sb, tq, has_sink, emit_lse):
    if has_sink:
        sink_ref, refs = refs[0], refs[1:]
    bias_ref, q_ref, kp_ref, kc_ref, kn_ref, vp_ref, vc_ref, vn_ref = refs[:8]
    o_ref = refs[8]
    kx_ref, vx_ref, vt_ref, qt_ref = refs[-4:]
    for x_ref, (p_ref, c_ref, n_ref) in ((kx_ref, (kp_ref, kc_ref, kn_ref)), (vx_ref, (vp_ref, vc_ref, vn_ref))):
        x_ref[0:radius, :] = p_ref[0, 0, tq - radius:, :]
        x_ref[radius:radius + tq, :] = c_ref[0, 0]
        x_ref[radius + tq:, :] = n_ref[0, 0, :radius, :]
    vt_ref[...] = vx_ref[...].astype(F32).T.astype(BF16)
    qt_ref[...] = (q_ref[0, 0].astype(F32) * (HEAD_DIM ** -0.5)).T.astype(BF16)
    w = sb + 2 * radius
    rep = n_heads // n_kv
    heads = range(n_heads)
    zeros = jnp.zeros((HEAD_DIM, sb), BF16)
    for u in range(tq // sb):
        rows = slice(u * sb, (u + 1) * sb)
        win = slice(u * sb, u * sb + w)
        scores = []
        for h in heads:
            g = h // rep
            qh = qt_ref[h * HEAD_DIM:(h + 1) * HEAD_DIM, rows]
            rhs = jnp.concatenate([qh, zeros] if g % 2 == 0 else [zeros, qh], axis=0)
            pair = slice((g // 2) * 2 * HEAD_DIM, (g // 2 + 1) * 2 * HEAD_DIM)
            scores.append(jnp.dot(kx_ref[win, pair], rhs, preferred_element_type=F32) + bias_ref[0, u, h])
        probs, stats = [], []
        for h in heads:
            m = jnp.max(scores[h], axis=0, keepdims=True)
            if has_sink:
                m = jnp.maximum(m, sink_ref[h])
            p = jnp.exp(scores[h] - m)
            l = jnp.sum(p, axis=0, keepdims=True)
            if has_sink:
                l = l + jnp.exp(sink_ref[h] - m)
            probs.append(p.astype(BF16))
            stats.append((m, l))
        outs, lses = [], []
        for h in heads:
            g = h // rep
            m, l = stats[h]
            outs.append(jnp.dot(vt_ref[g * HEAD_DIM:(g + 1) * HEAD_DIM, win], probs[h],
                                preferred_element_type=F32) / l)
            lses.append(jnp.broadcast_to(m + jnp.log(l), (HEAD_DIM, sb)))
        o_ref[0, 0, rows, :] = jnp.concatenate(outs, axis=0).T.astype(o_ref.dtype)
        if emit_lse:
            refs[9][0, 0, rows, :] = jnp.concatenate(lses, axis=0).T


def _alibi(n):
    return np.asarray(2.0 ** (-8.0 * np.arange(1, n + 1) / n), dtype=np.float32)


def _band_bias(n_heads, radius, step, sb, tq):
    w = sb + 2 * radius
    shape = (4, tq // sb, n_heads, w, sb)
    v, u, h, kj, qi = (lax.broadcasted_iota(jnp.int32, shape, d) for d in range(5))
    rel = jnp.abs(kj - radius - qi)
    kpos = u * sb - radius + kj
    ok = (rel <= radius) & ((kpos >= 0) | (v % 2 == 0)) & ((kpos < tq) | (v < 2))
    slopes = jnp.asarray(_alibi(n_heads))[h]
    return jnp.where(ok, -slopes * (step * rel).astype(F32), NEG)


def _band_attention(arr, *, q_col, k_col, v_col, kv_width, n_heads, n_kv, radius, step, sink, out_dtype,
                    emit_lse):
    batch, n_classes, length, _ = arr.shape
    qw = n_heads * HEAD_DIM
    tq = _pick(length, 256)
    sb = min(tq, 128)
    nblk = length // tq

    def kv_spec(col, shift):
        return pl.BlockSpec((1, 1, tq, kv_width), lambda b, r, i: (b, r, jnp.clip(i + shift, 0, nblk - 1), col))

    bias = _band_bias(n_heads, radius, step, sb, tq)
    in_specs = [pl.BlockSpec((1,) + bias.shape[1:],
                             lambda b, r, i: ((i == 0).astype(jnp.int32) + 2 * (i == nblk - 1).astype(jnp.int32),
                                              0, 0, 0, 0)),
                pl.BlockSpec((1, 1, tq, qw), lambda b, r, i: (b, r, i, q_col)),
                kv_spec(k_col, -1), kv_spec(k_col, 0), kv_spec(k_col, 1),
                kv_spec(v_col, -1), kv_spec(v_col, 0), kv_spec(v_col, 1)]
    args = [bias] + [arr] * 7
    if sink is not None:
        in_specs = [pl.BlockSpec(memory_space=pltpu.SMEM)] + in_specs
        args = [sink] + args
    o_spec = pl.BlockSpec((1, 1, tq, qw), lambda b, r, i: (b, r, i, 0))
    o_shape = jax.ShapeDtypeStruct((batch, n_classes, length, qw), out_dtype)
    out_specs, out_shape = [o_spec], [o_shape]
    if emit_lse:
        out_specs, out_shape = [o_spec, o_spec], [o_shape, jax.ShapeDtypeStruct(o_shape.shape, F32)]
    return pl.pallas_call(
        functools.partial(_band_kernel, n_heads=n_heads, n_kv=n_kv, radius=radius, sb=sb, tq=tq,
                          has_sink=sink is not None, emit_lse=emit_lse),
        grid=(batch, n_classes, nblk),
        in_specs=in_specs,
        out_specs=out_specs,
        out_shape=out_shape,
        scratch_shapes=[pltpu.VMEM((tq + 2 * radius, kv_width), BF16), pltpu.VMEM((tq + 2 * radius, kv_width), BF16),
                        pltpu.VMEM((kv_width, tq + 2 * radius), BF16), pltpu.VMEM((qw, tq), BF16)],
        compiler_params=_cparams("parallel", "parallel", "arbitrary"),
        name="band_attn",
    )(*args)


def _rope_slot(x, cos_t, sin_t):
    lane = lax.broadcasted_iota(jnp.int32, x.shape, 1)
    first = (lane >= QK_NOPE) & (lane < QK_NOPE + QK_ROPE // 2)
    partner = jnp.where(first, pltpu.roll(x, SLOT - QK_ROPE // 2, 1), pltpu.roll(x, QK_ROPE // 2, 1))
    return x * cos_t + partner * sin_t


def _mla_prep_kernel(pb_ref, qn_ref, kvn_ref, wq_ref, wk_ref, wv_ref, cq_ref, sq_ref, ck_ref, sk_ref,
                     q_out, k_out, vt_out):
    pb = pb_ref[...]
    cq = _rms(pb[:, :Q_LORA], qn_ref[...]).astype(BF16)
    ckv = _rms(pb[:, Q_LORA:Q_LORA + KV_LORA], kvn_ref[...]).astype(BF16)
    kr = pb[:, Q_LORA + KV_LORA:]
    q = jnp.dot(cq, wq_ref[...], preferred_element_type=F32)
    k = jnp.dot(ckv, wk_ref[...], preferred_element_type=F32)
    vt_out[0] = jnp.dot(ckv, wv_ref[...], preferred_element_type=F32).T.astype(BF16)
    kr = _rope_slot(kr, ck_ref[...], sk_ref[...])
    for h in range(B_HEADS):
        sl = slice(h * SLOT, (h + 1) * SLOT)
        q_out[:, sl] = _rope_slot(q[:, sl], cq_ref[...], sq_ref[...]).astype(BF16)
        k_out[:, sl] = (k[:, sl] + kr).astype(BF16)


def _mla_prep(pb, qn, kvn, wq, wk, wv, tabs, batch, seq):
    t = pb.shape[0]
    tm = _pick(seq, 512)
    nseq = seq // tm
    row = lambda i: (i, 0)
    fixed = lambda i: (0, 0)
    tab = pl.BlockSpec((tm, SLOT), lambda i: (i % nseq, 0))
    return pl.pallas_call(
        _mla_prep_kernel,
        grid=(t // tm,),
        in_specs=[pl.BlockSpec((tm, B_PAD_COLS), row),
                  pl.BlockSpec((1, Q_LORA), fixed), pl.BlockSpec((1, KV_LORA), fixed),
                  pl.BlockSpec((Q_LORA, B_HEADS * SLOT), fixed),
                  pl.BlockSpec((KV_LORA, B_HEADS * SLOT), fixed),
                  pl.BlockSpec((KV_LORA, B_HEADS * V_DIM), fixed),
                  tab, tab, tab, tab],
        out_specs=[pl.BlockSpec((tm, B_HEADS * SLOT), row), pl.BlockSpec((tm, B_HEADS * SLOT), row),
                   pl.BlockSpec((1, B_HEADS * V_DIM, tm), lambda i: (i // nseq, 0, i % nseq))],
        out_shape=[jax.ShapeDtypeStruct((t, B_HEADS * SLOT), BF16), jax.ShapeDtypeStruct((t, B_HEADS * SLOT), BF16),
                   jax.ShapeDtypeStruct((batch, B_HEADS * V_DIM, seq), BF16)],
        compiler_params=_cparams("parallel"),
        name="mla_prep",
    )(pb, qn, kvn, wq, wk, wv, *tabs)


def _mla_attn_kernel(q_ref, k_ref, vt_ref, o_ref, st_ref, *, seq, kc):
    chunks = [slice(c * kc, (c + 1) * kc) for c in range(seq // kc)]

    def qk(hh, rows):
        sl = slice(hh * SLOT, (hh + 1) * SLOT)
        st = lax.dot_general(k_ref[0, rows, sl], q_ref[0, :, sl], (((1,), (1,)), ((), ())),
                             preferred_element_type=F32)
        st_ref[hh, rows, :] = st
        return jnp.max(st, axis=0, keepdims=True)

    def pv(hh, rows, m):
        p = jnp.exp2(st_ref[hh, rows, :] - m)
        return (jnp.sum(p, axis=0, keepdims=True),
                jnp.dot(vt_ref[0, :, rows], p.astype(BF16), preferred_element_type=F32))

    def add(acc, new):
        return new if acc is None else (acc[0] + new[0], acc[1] + new[1])

    m0 = functools.reduce(jnp.maximum, [qk(0, rows) for rows in chunks])
    m1 = acc0 = acc1 = None
    for rows in chunks:
        acc0 = add(acc0, pv(0, rows, m0))
        mc = qk(1, rows)
        m1 = mc if m1 is None else jnp.maximum(m1, mc)
    for rows in chunks:
        acc1 = add(acc1, pv(1, rows, m1))
    outs = [acc[1][hh * V_DIM:(hh + 1) * V_DIM] / acc[0] for hh, acc in enumerate((acc0, acc1))]
    o_ref[0] = jnp.concatenate(outs, axis=0).T.astype(o_ref.dtype)


def _mla_attn(q, k, vt, batch, seq):
    tq = _pick(seq, 256)
    kc = _pick(seq, 512)
    pairs = B_HEADS // 2
    return pl.pallas_call(
        functools.partial(_mla_attn_kernel, seq=seq, kc=kc),
        scratch_shapes=[pltpu.VMEM((2, seq, tq), F32)],
        grid=(batch, pairs, seq // tq),
        in_specs=[pl.BlockSpec((1, tq, 2 * SLOT), lambda b, hp, i: (b, i, hp)),
                  pl.BlockSpec((1, seq, 2 * SLOT), lambda b, hp, i: (b, 0, hp)),
                  pl.BlockSpec((1, 2 * V_DIM, seq), lambda b, hp, i: (b, hp, 0))],
        out_specs=pl.BlockSpec((1, tq, 2 * V_DIM), lambda b, hp, i: (b, i, hp)),
        out_shape=jax.ShapeDtypeStruct((batch, seq, B_HEADS * V_DIM), BF16),
        compiler_params=_cparams("parallel", "parallel", "arbitrary"),
        name="mla_attn",
    )(q, k, vt)


def _softplus(x):
    return jnp.maximum(x, 0.0) + jnp.log1p(jnp.exp(-jnp.abs(x)))


def _lru_gates_kernel(xc_ref, xp_ref, xn_ref, cw_ref, cb_ref, wg_ref, bg_ref, lam_ref,
                      af_ref, uf_ref, ar_ref, ur_ref, ext_ref, *, tm):
    i = pl.program_id(1)
    halo = SUBLANES
    ext_ref[0:halo, :] = jnp.where(i > 0, xp_ref[0], 0.0)
    ext_ref[halo:halo + tm, :] = xc_ref[0]
    ext_ref[halo + tm:, :] = jnp.where(i < pl.num_programs(1) - 1, xn_ref[0], 0.0)
    xc = cb_ref[...]
    for tap in range(CONV_W):
        off = halo - CONV_PAD_L + tap
        xc = xc + ext_ref[off:off + tm, :] * cw_ref[tap:tap + 1, :]
    gates = jnp.dot(xc.astype(BF16), wg_ref[...], preferred_element_type=F32) + bg_ref[...]
    for d, (a_ref, u_ref) in enumerate(((af_ref, uf_ref), (ar_ref, ur_ref))):
        base = 2 * d * LRU_WIDTH
        r = jax.nn.sigmoid(gates[:, base:base + LRU_WIDTH])
        ig = jax.nn.sigmoid(gates[:, base + LRU_WIDTH:base + 2 * LRU_WIDTH])
        log_a = -LRU_C * r * _softplus(-lam_ref[d:d + 1, :])
        a_ref[0] = jnp.exp(log_a)
        th = jnp.tanh(log_a)
        u_ref[0] = jnp.sqrt(-2.0 * th / (1.0 - th)) * (ig * xc)


def _lru_gates(pd, cw, cb, wg, bg, lam, batch, seq):
    tm = _pick(seq, 512)
    nt = seq // tm
    per = tm // SUBLANES
    last8 = seq // SUBLANES - 1
    blk = pl.BlockSpec((1, tm, LRU_WIDTH), lambda b, i: (b, i, 0))
    fixed = lambda b, i: (0, 0)
    shp = jax.ShapeDtypeStruct((batch, seq, LRU_WIDTH), F32)
    return pl.pallas_call(
        functools.partial(_lru_gates_kernel, tm=tm),
        grid=(batch, nt),
        in_specs=[blk,
                  pl.BlockSpec((1, SUBLANES, LRU_WIDTH), lambda b, i: (b, jnp.maximum(i * per - 1, 0), 0)),
                  pl.BlockSpec((1, SUBLANES, LRU_WIDTH), lambda b, i: (b, jnp.minimum((i + 1) * per, last8), 0)),
                  pl.BlockSpec((CONV_W, LRU_WIDTH), fixed), pl.BlockSpec((1, LRU_WIDTH), fixed),
                  pl.BlockSpec((LRU_WIDTH, 4 * LRU_WIDTH), fixed), pl.BlockSpec((1, 4 * LRU_WIDTH), fixed),
                  pl.BlockSpec((2, LRU_WIDTH), fixed)],
        out_specs=[blk, blk, blk, blk],
        out_shape=[shp, shp, shp, shp],
        scratch_shapes=[pltpu.VMEM((tm + 2 * SUBLANES, LRU_WIDTH), F32)],
        compiler_params=_cparams("parallel", "arbitrary"),
        name="lru_gates",
    )(pd, pd, pd, cw, cb, wg, bg, lam)


def _lru_scan_kernel(af_ref, uf_ref, ar_ref, ur_ref, hf_ref, hr_ref, cf_ref, cr_ref, *, ts):
    @pl.when(pl.program_id(1) == 0)
    def _():
        cf_ref[...] = jnp.zeros_like(cf_ref)
        cr_ref[...] = jnp.zeros_like(cr_ref)

    ng = ts // SUBLANES
    row = lax.broadcasted_iota(jnp.int32, (SUBLANES, LRU_WIDTH), 0)

    def group_scan(a, u, carry, reverse):
        for d in (1, 2, 4):
            if reverse:
                keep, sh = row < SUBLANES - d, SUBLANES - d
            else:
                keep, sh = row >= d, d
            ap = jnp.where(keep, pltpu.roll(a, sh, 0), 1.0)
            up = jnp.where(keep, pltpu.roll(u, sh, 0), 0.0)
            u = a * up + u
            a = a * ap
        return u + a * carry

    def body(g, carry):
        cf, cr = carry
        r0 = pl.multiple_of(g * SUBLANES, SUBLANES)
        hf = group_scan(af_ref[0, pl.ds(r0, SUBLANES), :], uf_ref[0, pl.ds(r0, SUBLANES), :], cf, False)
        hf_ref[0, pl.ds(r0, SUBLANES), :] = hf
        r1 = pl.multiple_of((ng - 1 - g) * SUBLANES, SUBLANES)
        hr = group_scan(ar_ref[0, pl.ds(r1, SUBLANES), :], ur_ref[0, pl.ds(r1, SUBLANES), :], cr, True)
        hr_ref[0, pl.ds(r1, SUBLANES), :] = hr
        return (jnp.broadcast_to(hf[SUBLANES - 1:, :], hf.shape), jnp.broadcast_to(hr[:1, :], hr.shape))

    cf, cr = lax.fori_loop(0, ng, body, (cf_ref[...], cr_ref[...]))
    cf_ref[...] = cf
    cr_ref[...] = cr


def _lru_scan(af, uf, ar, ur):
    batch, seq, _ = af.shape
    ts = _pick(seq, 512)
    nt = seq // ts
    fwd = pl.BlockSpec((1, ts, LRU_WIDTH), lambda b, i: (b, i, 0))
    rev = pl.BlockSpec((1, ts, LRU_WIDTH), lambda b, i: (b, nt - 1 - i, 0))
    shp = jax.ShapeDtypeStruct(af.shape, F32)
    return pl.pallas_call(
        functools.partial(_lru_scan_kernel, ts=ts),
        grid=(batch, nt),
        in_specs=[fwd, fwd, rev, rev],
        out_specs=[fwd, rev],
        out_shape=[shp, shp],
        scratch_shapes=[pltpu.VMEM((SUBLANES, LRU_WIDTH), F32), pltpu.VMEM((SUBLANES, LRU_WIDTH), F32)],
        compiler_params=_cparams("parallel", "arbitrary"),
        name="lru_scan",
    )(af, uf, ar, ur)


def _gelu_tanh(x):
    return 0.5 * x * (1.0 + jnp.tanh(np.sqrt(2.0 / np.pi).astype(np.float32) * (x + 0.044715 * (x * x * x))))


def _merge_kernel(o0, o1, o2, l0, l1, l2, yb_ref, yc_ref, hf_ref, hr_ref, gd_ref,
                  g0, g1, g2, g3, wb_ref, out_ref, y_ref, *nat_refs, tm):
    @pl.when(pl.program_id(1) == 0)
    def _():
        def natural(ref, g, buf):
            dil = DIL_CONFIGS[g][1]
            if dil == 1:
                return ref[0, 0]
            chunks = BRANCH_W // LANES
            for r in range(dil):
                for c in range(chunks):
                    buf[c, pl.ds(r, tm // dil, stride=dil), :] = ref[0, r, :, c * LANES:(c + 1) * LANES]
            return jnp.concatenate([buf[c] for c in range(chunks)], axis=1)

        outs = (o0[0, 0], natural(o1, 1, nat_refs[0]), natural(o2, 2, nat_refs[1]))
        lse = (l0[0, 0], natural(l1, 1, nat_refs[2]), natural(l2, 2, nat_refs[3]))
        m = jnp.maximum(jnp.maximum(lse[0], lse[1]), lse[2])
        e = [jnp.exp(v - m) for v in lse]
        tot = e[0] + e[1] + e[2]
        ya = (e[0] / tot) * outs[0] + (e[1] / tot) * outs[1] + (e[2] / tot) * outs[2]
        y_ref[0] = ya.astype(BF16)
        y_ref[1] = yb_ref[...]
        y_ref[2] = yc_ref[...]
        y_ref[3] = ((hf_ref[...] + hr_ref[...]) * _gelu_tanh(gd_ref[...])).astype(BF16)

    acc = None
    for nbr, g_ref in enumerate((g0, g1, g2, g3)):
        term = jax.nn.sigmoid(g_ref[...]) * jnp.dot(y_ref[nbr], wb_ref[nbr], preferred_element_type=F32)
        acc = term if acc is None else acc + term
    out_ref[...] = acc.astype(out_ref.dtype)


def _merge(oa, la, yb, yc, hf, hr, pd, pg, wb, seq):
    t = yb.shape[0]
    tm, tn = _pick(seq, 256), 512
    nseq = seq // tm
    nj = D_MODEL // tn
    blk = pl.BlockSpec((tm, BRANCH_W), lambda i, j: (i, 0))
    cls = [pl.BlockSpec((1, dil, tm // dil, BRANCH_W), lambda i, j: (i // nseq, 0, i % nseq, 0))
           for _, dil in DIL_CONFIGS]
    gate_specs = [pl.BlockSpec((tm, tn), functools.partial(lambda i, j, nbr: (i, nbr * nj + j), nbr=nbr))
                  for nbr in range(N_BRANCH)]
    return pl.pallas_call(
        functools.partial(_merge_kernel, tm=tm),
        grid=(t // tm, nj),
        in_specs=cls + cls + [blk] * 4 + [pl.BlockSpec((tm, BRANCH_W), lambda i, j: (i, 1))] + gate_specs
                 + [pl.BlockSpec((N_BRANCH, BRANCH_W, tn), lambda i, j: (0, 0, j))],
        out_specs=pl.BlockSpec((tm, tn), lambda i, j: (i, j)),
        out_shape=jax.ShapeDtypeStruct((t, D_MODEL), BF16),
        scratch_shapes=[pltpu.VMEM((N_BRANCH, tm, BRANCH_W), BF16)] + [pltpu.VMEM((BRANCH_W // LANES, tm, LANES), F32)] * 4,
        compiler_params=_cparams("parallel", "arbitrary"),
        name="merge",
    )(*oa, *la, yb, yc, hf, hr, pd, pg, pg, pg, pg, wb)


def _mm_res_kernel(a_ref, w_ref, x_ref, o_ref):
    o_ref[...] = x_ref[...] + jnp.dot(a_ref[...], w_ref[...], preferred_element_type=F32)


def _mm_res(a, w, x):
    t, k = a.shape
    n = w.shape[1]
    tm, tn = _pick(t, 512), 1024
    return pl.pallas_call(
        _mm_res_kernel,
        grid=(t // tm, n // tn),
        in_specs=[pl.BlockSpec((tm, k), lambda i, j: (i, 0)),
                  pl.BlockSpec((k, tn), lambda i, j: (0, j)),
                  pl.BlockSpec((tm, tn), lambda i, j: (i, j))],
        out_specs=pl.BlockSpec((tm, tn), lambda i, j: (i, j)),
        out_shape=jax.ShapeDtypeStruct((t, n), F32),
        compiler_params=_cparams("parallel", "arbitrary"),
        name="out_proj",
    )(a, w, x)


def _block_diag(w):
    eye = jnp.eye(LRU_BLOCKS, dtype=w.dtype)
    return jnp.einsum('nef,nm->nemf', w, eye).reshape(LRU_WIDTH, LRU_WIDTH)


def _prep_layer(l, ffn1_norm, ffn1_w1, ffn1_w3, ffn1_w2, mix_norm, w_in, mla_q_norm, mla_w_uq, mla_kv_norm,
                mla_w_ukv, lru_conv_w, lru_conv_b, lru_w_a, lru_b_a, lru_w_x, lru_b_x, lru_lambda, sink_logits,
                w_branch, w_out, ffn2_norm, ffn2_w1, ffn2_w3, ffn2_w2):
    row = lambda v: v.reshape(1, -1)
    wi = w_in[l]
    c0, c1, c2, c3 = A_COLS, A_COLS + B_COLS, A_COLS + B_COLS + C_COLS, A_COLS + B_COLS + C_COLS + D_COLS
    w_a = wi[:, :c0].reshape(D_MODEL, 3, N_DIL, A_HEADS * HEAD_DIM).transpose(0, 2, 1, 3).reshape(D_MODEL, A_COLS)
    wb = wi[:, c0:c1]
    zeros = lambda n: jnp.zeros((D_MODEL, n), wi.dtype)
    w_b = jnp.concatenate([wb[:, :Q_LORA + KV_LORA], zeros(QK_NOPE), wb[:, Q_LORA + KV_LORA:],
                           zeros(SLOT - QK_NOPE - QK_ROPE)], axis=1)
    w_bd = jnp.concatenate([w_b, wi[:, c2:c3]], axis=1)
    wq = mla_w_uq[l].reshape(Q_LORA, B_HEADS, QK_NOPE + QK_ROPE)
    wq = jnp.pad(wq, ((0, 0), (0, 0), (0, SLOT - QK_NOPE - QK_ROPE))).reshape(Q_LORA, B_HEADS * SLOT)
    wkv = mla_w_ukv[l].reshape(KV_LORA, B_HEADS, QK_NOPE + V_DIM)
    wk = jnp.pad(wkv[:, :, :QK_NOPE], ((0, 0), (0, 0), (0, SLOT - QK_NOPE))).reshape(KV_LORA, B_HEADS * SLOT)
    wv = wkv[:, :, QK_NOPE:].reshape(KV_LORA, B_HEADS * V_DIM)
    wg = jnp.concatenate([_block_diag(lru_w_a[l, 0]), _block_diag(lru_w_x[l, 0]),
                          _block_diag(lru_w_a[l, 1]), _block_diag(lru_w_x[l, 1])], axis=1)
    bg = jnp.concatenate([lru_b_a[l, 0], lru_b_x[l, 0], lru_b_a[l, 1], lru_b_x[l, 1]]).reshape(1, -1)
    return dict(
        ffn1=(row(ffn1_norm[l]), ffn1_w1[l].astype(BF16), ffn1_w3[l].astype(BF16), ffn1_w2[l].astype(BF16)),
        ffn2=(row(ffn2_norm[l]), ffn2_w1[l].astype(BF16), ffn2_w3[l].astype(BF16), ffn2_w2[l].astype(BF16)),
        mix_norm=row(mix_norm[l]),
        w_a=w_a.astype(BF16), w_c=wi[:, c1:c2].astype(BF16), w_bd=w_bd.astype(BF16), w_g=wi[:, c3:].astype(BF16),
        qn=row(mla_q_norm[l]), kvn=row(mla_kv_norm[l]),
        wq=wq.astype(BF16), wk=wk.astype(BF16), wv=wv.astype(BF16),
        conv_w=lru_conv_w[l], conv_b=row(lru_conv_b[l]), wg=wg.astype(BF16), bg=bg, lam=lru_lambda[l],
        sink=sink_logits[l], w_branch=w_branch[l].astype(BF16), w_out=w_out[l].astype(BF16),
    )


def _rope_slot_tables(seq):
    inv = ROPE_THETA ** (-jnp.arange(0, QK_ROPE, 2, dtype=F32) / QK_ROPE)
    ang = jnp.arange(seq, dtype=F32)[:, None] * inv[None, :]
    cos, sin = jnp.cos(ang), jnp.sin(ang)
    scale = (QK_NOPE + QK_ROPE) ** -0.5 * np.log2(np.e)
    z = lambda n: jnp.zeros((seq, n), F32)
    tail = SLOT - QK_NOPE - QK_ROPE
    cos_q = jnp.concatenate([jnp.full((seq, QK_NOPE), scale, F32), cos * scale, cos * scale, z(tail)], axis=1)
    sin_q = jnp.concatenate([z(QK_NOPE), -sin * scale, sin * scale, z(tail)], axis=1)
    cos_k = jnp.concatenate([z(QK_NOPE), cos, cos, z(tail)], axis=1)
    sin_k = jnp.concatenate([z(QK_NOPE), -sin, sin, z(tail)], axis=1)
    return cos_q, sin_q, cos_k, sin_k


def _layer(x, w, batch, seq, final_g):
    t = batch * seq
    x = _ffn(x, *w['ffn1'], final_g, False)
    pas = _proj_a(x, w['mix_norm'], w['w_a'], batch, seq)
    pc = _norm_mm(x, w['mix_norm'], w['w_c'], BF16, 1024, C_COLS)
    pb, pd = _norm_mm(x, w['mix_norm'], w['w_bd'], F32, 512, None, widths=(B_PAD_COLS, D_COLS))
    pg = _norm_mm(x, w['mix_norm'], w['w_g'], F32, 1024, 1024)

    oa, la = [], []
    for pa, (window, dil) in zip(pas, DIL_CONFIGS):
        o, lse = _band_attention(
            pa, q_col=0, k_col=1, v_col=2, kv_width=A_HEADS * HEAD_DIM,
            n_heads=A_HEADS, n_kv=A_HEADS, radius=window // (2 * dil), step=dil, sink=None,
            out_dtype=F32, emit_lse=True)
        oa.append(o)
        la.append(lse)

    q, k, vt = _mla_prep(pb, w['qn'], w['kvn'], w['wq'], w['wk'], w['wv'], _rope_slot_tables(seq), batch, seq)
    yb = _mla_attn(q.reshape(batch, seq, -1), k.reshape(batch, seq, -1), vt, batch, seq)

    kvw = C_KV_HEADS * HEAD_DIM
    yc = _band_attention(
        pc.reshape(batch, 1, seq, C_COLS),
        q_col=0, k_col=C_HEADS * HEAD_DIM // kvw, v_col=C_HEADS * HEAD_DIM // kvw + 1, kv_width=kvw,
        n_heads=C_HEADS, n_kv=C_KV_HEADS, radius=C_RADIUS, step=1, sink=w['sink'],
        out_dtype=BF16, emit_lse=False)[0]

    af, uf, ar, ur = _lru_gates(pd.reshape(batch, seq, D_COLS), w['conv_w'], w['conv_b'], w['wg'], w['bg'],
                                w['lam'], batch, seq)
    hf, hr = _lru_scan(af, uf, ar, ur)

    merged = _merge(oa, la, yb.reshape(t, BRANCH_W), yc.reshape(t, BRANCH_W), hf.reshape(t, LRU_WIDTH),
                    hr.reshape(t, LRU_WIDTH), pd, pg, w['w_branch'], seq)
    x = _mm_res(merged, w['w_out'], x)
    return _ffn(x, *w['ffn2'], final_g, final_g is not None and w.get('last', False))


def _trunk(x, layers, final_norm):
    batch, seq, _ = x.shape
    h = x.reshape(batch * seq, D_MODEL)
    fg = final_norm.reshape(1, -1)
    for l, w in enumerate(layers):
        h = _layer(h, dict(w, last=(l == len(layers) - 1)), batch, seq, fg)
    return h.reshape(batch, seq, D_MODEL)


def kernel(x_prompt, x_sample, ffn1_norm, ffn1_w1, ffn1_w3, ffn1_w2, mix_norm, w_in, mla_q_norm, mla_w_uq, mla_kv_norm, mla_w_ukv, lru_conv_w, lru_conv_b, lru_w_a, lru_b_a, lru_w_x, lru_b_x, lru_lambda, sink_logits, w_branch, w_out, ffn2_norm, ffn2_w1, ffn2_w3, ffn2_w2, final_norm):
    layers = [_prep_layer(l, ffn1_norm, ffn1_w1, ffn1_w3, ffn1_w2, mix_norm, w_in, mla_q_norm, mla_w_uq,
                          mla_kv_norm, mla_w_ukv, lru_conv_w, lru_conv_b, lru_w_a, lru_b_a, lru_w_x, lru_b_x,
                          lru_lambda, sink_logits, w_branch, w_out, ffn2_norm, ffn2_w1, ffn2_w3, ffn2_w2)
              for l in range(DEPTH)]
    return (_trunk(x_prompt, layers, final_norm), _trunk(x_sample, layers, final_norm))
```

```python
import functools

import numpy as np
import jax
import jax.numpy as jnp
from jax import lax
from jax.experimental import pallas as pl
from jax.experimental.pallas import tpu as pltpu

F32 = jnp.float32
BF16 = jnp.bfloat16

D_MODEL = 2048
DEPTH = 2
HEAD_DIM = 64
N_BRANCH = 4
BRANCH_W = 512
DIL_CONFIGS = ((128, 1), (512, 4), (2048, 16))
N_DIL = 3
A_HEADS = 8
B_HEADS = 8
Q_LORA = 384
KV_LORA = 128
QK_NOPE = 64
QK_ROPE = 32
V_DIM = 64
ROPE_THETA = 10000.0
C_HEADS = 8
C_KV_HEADS = 2
C_RADIUS = 128
LRU_WIDTH = 512
LRU_BLOCKS = 8
LRU_BLOCK = 64
CONV_W = 4
CONV_PAD_L = 2
LRU_C = 8.0
D_FF = 5632
EPS = 1e-6
NEG = -1e30

A_COLS = 3 * N_DIL * A_HEADS * HEAD_DIM
B_COLS = Q_LORA + KV_LORA + QK_ROPE
C_COLS = (C_HEADS + 2 * C_KV_HEADS) * HEAD_DIM
D_COLS = 2 * LRU_WIDTH
G_COLS = N_BRANCH * D_MODEL

LANES = 128
SUBLANES = 8
SLOT = LANES
B_PAD_COLS = Q_LORA + KV_LORA + SLOT
ROW_CHUNK = 128
VMEM_LIMIT = 52 * 1024 * 1024
VMEM_LIMIT_FFN = 54 * 1024 * 1024


def _cparams(*sem, vmem=VMEM_LIMIT):
    return pltpu.CompilerParams(dimension_semantics=sem, vmem_limit_bytes=vmem)


def _rms(x, g):
    return x * lax.rsqrt(jnp.mean(x * x, axis=-1, keepdims=True) + EPS) * g


def _pick(n, pref):
    t = min(n, pref)
    while n % t:
        t //= 2
    return t


def _ffn_kernel(x_ref, g_ref, w1_ref, w3_ref, w2_ref, fg_ref, o_ref, h_ref, *, final_norm):
    j = pl.program_id(1)
    n_chunks = x_ref.shape[0] // ROW_CHUNK

    def chunk(r):
        return pl.ds(pl.multiple_of(r * ROW_CHUNK, ROW_CHUNK), ROW_CHUNK)

    @pl.when(j == 0)
    def _():
        def norm_step(r, carry):
            h_ref[chunk(r), :] = _rms(x_ref[chunk(r), :], g_ref[...]).astype(BF16)
            o_ref[chunk(r), :] = jnp.zeros((ROW_CHUNK, D_MODEL), F32)
            return carry
        lax.fori_loop(0, n_chunks, norm_step, 0)

    half = min(x_ref.shape[0], 512)

    def rows_step(r, carry):
        rows = pl.ds(pl.multiple_of(r * half, half), half)
        h = h_ref[rows, :]
        a = jnp.dot(h, w1_ref[...], preferred_element_type=F32)
        b = jnp.dot(h, w3_ref[...], preferred_element_type=F32)
        act = (a * jax.nn.sigmoid(a) * b).astype(BF16)
        for c in range(0, D_MODEL, half):
            o_ref[rows, c:c + half] += jnp.dot(act, w2_ref[:, c:c + half], preferred_element_type=F32)
        return carry

    lax.fori_loop(0, x_ref.shape[0] // half, rows_step, 0)

    @pl.when(j == pl.num_programs(1) - 1)
    def _():
        def out_step(r, carry):
            y = x_ref[chunk(r), :] + 0.5 * o_ref[chunk(r), :]
            if final_norm:
                y = _rms(y, fg_ref[...])
            o_ref[chunk(r), :] = y
            return carry
        lax.fori_loop(0, n_chunks, out_step, 0)


def _ffn(x, g, w1, w3, w2, fg, final_norm):
    t = x.shape[0]
    tm, tf = _pick(t, 1024), 512
    return pl.pallas_call(
        functools.partial(_ffn_kernel, final_norm=final_norm),
        grid=(t // tm, D_FF // tf),
        in_specs=[
            pl.BlockSpec((tm, D_MODEL), lambda i, j: (i, 0)),
            pl.BlockSpec((1, D_MODEL), lambda i, j: (0, 0)),
            pl.BlockSpec((D_MODEL, tf), lambda i, j: (0, j)),
            pl.BlockSpec((D_MODEL, tf), lambda i, j: (0, j)),
            pl.BlockSpec((tf, D_MODEL), lambda i, j: (j, 0)),
            pl.BlockSpec((1, D_MODEL), lambda i, j: (0, 0)),
        ],
        out_specs=pl.BlockSpec((tm, D_MODEL), lambda i, j: (i, 0)),
        out_shape=jax.ShapeDtypeStruct((t, D_MODEL), F32),
        scratch_shapes=[pltpu.VMEM((tm, D_MODEL), BF16)],
        compiler_params=_cparams("parallel", "arbitrary", vmem=VMEM_LIMIT_FFN),
        name="ffn",
    )(x, g, w1, w3, w2, fg)


def _proj_bcd_kernel(h_ref, w_ref, *o_refs):
    r = jnp.dot(h_ref[...], w_ref[...], preferred_element_type=F32)
    c = 0
    for o_ref in o_refs:
        wd = o_ref.shape[1]
        o_ref[...] = r[:, c:c + wd].astype(o_ref.dtype)
        c += wd


def _proj_bcd(h, w, outs):
    t, n = h.shape[0], w.shape[1]
    tm = _pick(t, 512)
    return pl.pallas_call(
        _proj_bcd_kernel,
        grid=(t // tm,),
        in_specs=[pl.BlockSpec((tm, D_MODEL), lambda i: (i, 0)),
                  pl.BlockSpec((D_MODEL, n), lambda i: (0, 0))],
        out_specs=[pl.BlockSpec((tm, wd), lambda i: (i, 0)) for wd, _ in outs],
        out_shape=[jax.ShapeDtypeStruct((t, wd), dt) for wd, dt in outs],
        compiler_params=_cparams("parallel"),
        name="proj_bcd",
    )(h, w)


def _proj_a_kernel(x_ref, g_ref, w_ref, o1_ref, o2_ref, o3_ref, hout_ref, hn_ref, h_ref, *, tm):
    j = pl.program_id(1)
    chunks = D_MODEL // LANES

    @pl.when(j == 0)
    def _():
        hn = _rms(x_ref[...], g_ref[...])
        for c in range(chunks):
            hn_ref[c] = hn[:, c * LANES:(c + 1) * LANES]
        h_ref[...] = hn.astype(BF16)
        hout_ref[...] = hn.astype(BF16)

    for g, o_ref in enumerate((o1_ref, o2_ref, o3_ref)):
        dil = DIL_CONFIGS[g][1]
        n = tm // dil

        @pl.when(j == g)
        def _(o_ref=o_ref, dil=dil, n=n):
            if dil > 1:
                for r in range(dil):
                    for c in range(chunks):
                        h_ref[r * n:(r + 1) * n, c * LANES:(c + 1) * LANES] = (
                            hn_ref[c, pl.ds(r, n, stride=dil), :].astype(BF16))
            res = jnp.dot(h_ref[...], w_ref[...], preferred_element_type=F32)
            for r in range(dil):
                o_ref[0, r] = res[r * n:(r + 1) * n].astype(BF16)


def _proj_a(x, g, w, batch, seq):
    t = x.shape[0]
    tm = _pick(seq, 512)
    nseq = seq // tm
    gw = A_COLS // N_DIL
    out_specs = [pl.BlockSpec((1, dil, tm // dil, gw), lambda i, j: (i // nseq, 0, i % nseq, 0))
                 for _, dil in DIL_CONFIGS]
    out_shape = [jax.ShapeDtypeStruct((batch, dil, seq // dil, gw), BF16) for _, dil in DIL_CONFIGS]
    out_specs.append(pl.BlockSpec((tm, D_MODEL), lambda i, j: (i, 0)))
    out_shape.append(jax.ShapeDtypeStruct((t, D_MODEL), BF16))
    return pl.pallas_call(
        functools.partial(_proj_a_kernel, tm=tm),
        grid=(t // tm, N_DIL),
        in_specs=[pl.BlockSpec((tm, D_MODEL), lambda i, j: (i, 0)),
                  pl.BlockSpec((1, D_MODEL), lambda i, j: (0, 0)),
                  pl.BlockSpec((D_MODEL, gw), lambda i, j: (0, j))],
        out_specs=out_specs,
        out_shape=out_shape,
        scratch_shapes=[pltpu.VMEM((D_MODEL // LANES, tm, LANES), F32), pltpu.VMEM((tm, D_MODEL), BF16)],
        compiler_params=_cparams("parallel", "arbitrary"),
        name="proj_a",
    )(x, g, w)


def _band_kernel(*refs, n_heads, n_kv, radius, sb, tq, has_sink, emit_lse):
    if has_sink:
        sink_ref, refs = refs[0], refs[1:]
    bias_ref, q_ref, kp_ref, kc_ref, kn_ref, vp_ref, vc_ref, vn_ref = refs[:8]
    o_ref = refs[8]
    kx_ref, vx_ref, vt_ref, qt_ref = refs[-4:]
    for x_ref, (p_ref, c_ref, n_ref) in ((kx_ref, (kp_ref, kc_ref, kn_ref)), (vx_ref, (vp_ref, vc_ref, vn_ref))):
        x_ref[0:radius, :] = p_ref[0, 0, tq - radius:, :]
        x_ref[radius:radius + tq, :] = c_ref[0, 0]
        x_ref[radius + tq:, :] = n_ref[0, 0, :radius, :]
    vt_ref[...] = vx_ref[...].astype(F32).T.astype(BF16)
    qt_ref[...] = (q_ref[0, 0].astype(F32) * (HEAD_DIM ** -0.5)).T.astype(BF16)
    w = sb + 2 * radius
    rep = n_heads // n_kv
    heads = range(n_heads)
    zeros = jnp.zeros((HEAD_DIM, sb), BF16)
    for u in range(tq // sb):
        rows = slice(u * sb, (u + 1) * sb)
        win = slice(u * sb, u * sb + w)
        scores = []
        for h in heads:
            g = h // rep
            qh = qt_ref[h * HEAD_DIM:(h + 1) * HEAD_DIM, rows]
            rhs = jnp.concatenate([qh, zeros] if g % 2 == 0 else [zeros, qh], axis=0)
            pair = slice((g // 2) * 2 * HEAD_DIM, (g // 2 + 1) * 2 * HEAD_DIM)
            scores.append(jnp.dot(kx_ref[win, pair], rhs, preferred_element_type=F32) + bias_ref[0, u, h])
        probs, stats = [], []
        for h in heads:
            m = jnp.max(scores[h], axis=0, keepdims=True)
            if has_sink:
                m = jnp.maximum(m, sink_ref[h])
            p = jnp.exp(scores[h] - m)
            l = jnp.sum(p, axis=0, keepdims=True)
            if has_sink:
                l = l + jnp.exp(sink_ref[h] - m)
            probs.append(p.astype(BF16))
            stats.append((m, l))
        outs, lses = [], []
        for h in heads:
            g = h // rep
            m, l = stats[h]
            outs.append(jnp.dot(vt_ref[g * HEAD_DIM:(g + 1) * HEAD_DIM, win], probs[h],
                                preferred_element_type=F32) / l)
            lses.append(jnp.broadcast_to(m + jnp.log(l), (HEAD_DIM, sb)))
        o_ref[0, 0, rows, :] = jnp.concatenate(outs, axis=0).T.astype(o_ref.dtype)
        if emit_lse:
            refs[9][0, 0, rows, :] = jnp.concatenate(lses, axis=0).T


def _alibi(n):
    return np.asarray(2.0 ** (-8.0 * np.arange(1, n + 1) / n), dtype=np.float32)


def _band_bias(n_heads, radius, step, sb, tq):
    w = sb + 2 * radius
    shape = (4, tq // sb, n_heads, w, sb)
    v, u, h, kj, qi = (lax.broadcasted_iota(jnp.int32, shape, d) for d in range(5))
    rel = jnp.abs(kj - radius - qi)
    kpos = u * sb - radius + kj
    ok = (rel <= radius) & ((kpos >= 0) | (v % 2 == 0)) & ((kpos < tq) | (v < 2))
    slopes = jnp.asarray(_alibi(n_heads))[h]
    return jnp.where(ok, -slopes * (step * rel).astype(F32), NEG)


def _band_attention(arr, *, q_col, k_col, v_col, kv_width, n_heads, n_kv, radius, step, sink, out_dtype,
                    emit_lse):
    batch, n_classes, length, _ = arr.shape
    qw = n_heads * HEAD_DIM
    tq = _pick(length, 256)
    sb = min(tq, 128)
    nblk = length // tq

    def kv_spec(col, shift):
        return pl.BlockSpec((1, 1, tq, kv_width), lambda b, r, i: (b, r, jnp.clip(i + shift, 0, nblk - 1), col))

    bias = _band_bias(n_heads, radius, step, sb, tq)
    in_specs = [pl.BlockSpec((1,) + bias.shape[1:],
                             lambda b, r, i: ((i == 0).astype(jnp.int32) + 2 * (i == nblk - 1).astype(jnp.int32),
                                              0, 0, 0, 0)),
                pl.BlockSpec((1, 1, tq, qw), lambda b, r, i: (b, r, i, q_col)),
                kv_spec(k_col, -1), kv_spec(k_col, 0), kv_spec(k_col, 1),
                kv_spec(v_col, -1), kv_spec(v_col, 0), kv_spec(v_col, 1)]
    args = [bias] + [arr] * 7
    if sink is not None:
        in_specs = [pl.BlockSpec(memory_space=pltpu.SMEM)] + in_specs
        args = [sink] + args
    o_spec = pl.BlockSpec((1, 1, tq, qw), lambda b, r, i: (b, r, i, 0))
    o_shape = jax.ShapeDtypeStruct((batch, n_classes, length, qw), out_dtype)
    out_specs, out_shape = [o_spec], [o_shape]
    if emit_lse:
        out_specs, out_shape = [o_spec, o_spec], [o_shape, jax.ShapeDtypeStruct(o_shape.shape, F32)]
    return pl.pallas_call(
        functools.partial(_band_kernel, n_heads=n_heads, n_kv=n_kv, radius=radius, sb=sb, tq=tq,
                          has_sink=sink is not None, emit_lse=emit_lse),
        grid=(batch, n_classes, nblk),
        in_specs=in_specs,
        out_specs=out_specs,
        out_shape=out_shape,
        scratch_shapes=[pltpu.VMEM((tq + 2 * radius, kv_width), BF16), pltpu.VMEM((tq + 2 * radius, kv_width), BF16),
                        pltpu.VMEM((kv_width, tq + 2 * radius), BF16), pltpu.VMEM((qw, tq), BF16)],
        compiler_params=_cparams("parallel", "parallel", "arbitrary"),
        name="band_attn",
    )(*args)


def _rope_slot(x, cos_t, sin_t):
    lane = lax.broadcasted_iota(jnp.int32, x.shape, 1)
    first = (lane >= QK_NOPE) & (lane < QK_NOPE + QK_ROPE // 2)
    partner = jnp.where(first, pltpu.roll(x, SLOT - QK_ROPE // 2, 1), pltpu.roll(x, QK_ROPE // 2, 1))
    return x * cos_t + partner * sin_t


def _mla_prep_kernel(pb_ref, qn_ref, kvn_ref, wq_ref, wk_ref, wv_ref, cq_ref, sq_ref, ck_ref, sk_ref,
                     q_out, k_out, vt_out):
    pb = pb_ref[...]
    cq = _rms(pb[:, :Q_LORA], qn_ref[...]).astype(BF16)
    ckv = _rms(pb[:, Q_LORA:Q_LORA + KV_LORA], kvn_ref[...]).astype(BF16)
    kr = pb[:, Q_LORA + KV_LORA:]
    q = jnp.dot(cq, wq_ref[...], preferred_element_type=F32)
    k = jnp.dot(ckv, wk_ref[...], preferred_element_type=F32)
    vt_out[0] = jnp.dot(ckv, wv_ref[...], preferred_element_type=F32).T.astype(BF16)
    kr = _rope_slot(kr, ck_ref[...], sk_ref[...])
    for h in range(B_HEADS):
        sl = slice(h * SLOT, (h + 1) * SLOT)
        q_out[:, sl] = _rope_slot(q[:, sl], cq_ref[...], sq_ref[...]).astype(BF16)
        k_out[:, sl] = (k[:, sl] + kr).astype(BF16)


def _mla_prep(pb, qn, kvn, wq, wk, wv, tabs, batch, seq):
    t = pb.shape[0]
    tm = _pick(seq, 512)
    nseq = seq // tm
    row = lambda i: (i, 0)
    fixed = lambda i: (0, 0)
    tab = pl.BlockSpec((tm, SLOT), lambda i: (i % nseq, 0))
    return pl.pallas_call(
        _mla_prep_kernel,
        grid=(t // tm,),
        in_specs=[pl.BlockSpec((tm, B_PAD_COLS), row),
                  pl.BlockSpec((1, Q_LORA), fixed), pl.BlockSpec((1, KV_LORA), fixed),
                  pl.BlockSpec((Q_LORA, B_HEADS * SLOT), fixed),
                  pl.BlockSpec((KV_LORA, B_HEADS * SLOT), fixed),
                  pl.BlockSpec((KV_LORA, B_HEADS * V_DIM), fixed),
                  tab, tab, tab, tab],
        out_specs=[pl.BlockSpec((tm, B_HEADS * SLOT), row), pl.BlockSpec((tm, B_HEADS * SLOT), row),
                   pl.BlockSpec((1, B_HEADS * V_DIM, tm), lambda i: (i // nseq, 0, i % nseq))],
        out_shape=[jax.ShapeDtypeStruct((t, B_HEADS * SLOT), BF16), jax.ShapeDtypeStruct((t, B_HEADS * SLOT), BF16),
                   jax.ShapeDtypeStruct((batch, B_HEADS * V_DIM, seq), BF16)],
        compiler_params=_cparams("parallel"),
        name="mla_prep",
    )(pb, qn, kvn, wq, wk, wv, *tabs)


def _mla_attn_kernel(q_ref, k_ref, vt_ref, o_ref, st_ref, *, seq, kc):
    chunks = [slice(c * kc, (c + 1) * kc) for c in range(seq // kc)]

    def qk(hh, rows):
        sl = slice(hh * SLOT, (hh + 1) * SLOT)
        st = lax.dot_general(k_ref[0, rows, sl], q_ref[0, :, sl], (((1,), (1,)), ((), ())),
                             preferred_element_type=F32)
        st_ref[hh, rows, :] = st
        return jnp.max(st, axis=0, keepdims=True)

    def pv(hh, rows, m):
        p = jnp.exp2(st_ref[hh, rows, :] - m)
        return (jnp.sum(p, axis=0, keepdims=True),
                jnp.dot(vt_ref[0, :, rows], p.astype(BF16), preferred_element_type=F32))

    def add(acc, new):
        return new if acc is None else (acc[0] + new[0], acc[1] + new[1])

    m0 = functools.reduce(jnp.maximum, [qk(0, rows) for rows in chunks])
    m1 = acc0 = acc1 = None
    for rows in chunks:
        acc0 = add(acc0, pv(0, rows, m0))
        mc = qk(1, rows)
        m1 = mc if m1 is None else jnp.maximum(m1, mc)
    for rows in chunks:
        acc1 = add(acc1, pv(1, rows, m1))
    outs = [acc[1][hh * V_DIM:(hh + 1) * V_DIM] / acc[0] for hh, acc in enumerate((acc0, acc1))]
    o_ref[0] = jnp.concatenate(outs, axis=0).T.astype(o_ref.dtype)


def _mla_attn(q, k, vt, batch, seq):
    tq = _pick(seq, 256)
    kc = _pick(seq, 512)
    pairs = B_HEADS // 2
    return pl.pallas_call(
        functools.partial(_mla_attn_kernel, seq=seq, kc=kc),
        scratch_shapes=[pltpu.VMEM((2, seq, tq), F32)],
        grid=(batch, pairs, seq // tq),
        in_specs=[pl.BlockSpec((1, tq, 2 * SLOT), lambda b, hp, i: (b, i, hp)),
                  pl.BlockSpec((1, seq, 2 * SLOT), lambda b, hp, i: (b, 0, hp)),
                  pl.BlockSpec((1, 2 * V_DIM, seq), lambda b, hp, i: (b, hp, 0))],
        out_specs=pl.BlockSpec((1, tq, 2 * V_DIM), lambda b, hp, i: (b, i, hp)),
        out_shape=jax.ShapeDtypeStruct((batch, seq, B_HEADS * V_DIM), BF16),
        compiler_params=_cparams("parallel", "parallel", "arbitrary"),
        name="mla_attn",
    )(q, k, vt)


def _softplus(x):
    return jnp.maximum(x, 0.0) + jnp.log1p(jnp.exp(-jnp.abs(x)))


def _lru_gates_kernel(xc_ref, xp_ref, xn_ref, cw_ref, cb_ref, wg_ref, bg_ref, lam_ref,
                      af_ref, uf_ref, ar_ref, ur_ref, ext_ref, *, tm):
    i = pl.program_id(1)
    halo = SUBLANES
    ext_ref[0:halo, :] = jnp.where(i > 0, xp_ref[0], 0.0)
    ext_ref[halo:halo + tm, :] = xc_ref[0]
    ext_ref[halo + tm:, :] = jnp.where(i < pl.num_programs(1) - 1, xn_ref[0], 0.0)
    xc = cb_ref[...]
    for tap in range(CONV_W):
        off = halo - CONV_PAD_L + tap
        xc = xc + ext_ref[off:off + tm, :] * cw_ref[tap:tap + 1, :]
    gates = jnp.dot(xc.astype(BF16), wg_ref[...], preferred_element_type=F32) + bg_ref[...]
    for d, (a_ref, u_ref) in enumerate(((af_ref, uf_ref), (ar_ref, ur_ref))):
        base = 2 * d * LRU_WIDTH
        r = jax.nn.sigmoid(gates[:, base:base + LRU_WIDTH])
        ig = jax.nn.sigmoid(gates[:, base + LRU_WIDTH:base + 2 * LRU_WIDTH])
        log_a = -LRU_C * r * _softplus(-lam_ref[d:d + 1, :])
        a_ref[0] = jnp.exp(log_a)
        th = jnp.tanh(log_a)
        u_ref[0] = jnp.sqrt(-2.0 * th / (1.0 - th)) * (ig * xc)


def _lru_gates(pd, cw, cb, wg, bg, lam, batch, seq):
    tm = _pick(seq, 512)
    nt = seq // tm
    per = tm // SUBLANES
    last8 = seq // SUBLANES - 1
    blk = pl.BlockSpec((1, tm, LRU_WIDTH), lambda b, i: (b, i, 0))
    fixed = lambda b, i: (0, 0)
    shp = jax.ShapeDtypeStruct((batch, seq, LRU_WIDTH), F32)
    return pl.pallas_call(
        functools.partial(_lru_gates_kernel, tm=tm),
        grid=(batch, nt),
        in_specs=[blk,
                  pl.BlockSpec((1, SUBLANES, LRU_WIDTH), lambda b, i: (b, jnp.maximum(i * per - 1, 0), 0)),
                  pl.BlockSpec((1, SUBLANES, LRU_WIDTH), lambda b, i: (b, jnp.minimum((i + 1) * per, last8), 0)),
                  pl.BlockSpec((CONV_W, LRU_WIDTH), fixed), pl.BlockSpec((1, LRU_WIDTH), fixed),
                  pl.BlockSpec((LRU_WIDTH, 4 * LRU_WIDTH), fixed), pl.BlockSpec((1, 4 * LRU_WIDTH), fixed),
                  pl.BlockSpec((2, LRU_WIDTH), fixed)],
        out_specs=[blk, blk, blk, blk],
        out_shape=[shp, shp, shp, shp],
        scratch_shapes=[pltpu.VMEM((tm + 2 * SUBLANES, LRU_WIDTH), F32)],
        compiler_params=_cparams("parallel", "arbitrary"),
        name="lru_gates",
    )(pd, pd, pd, cw, cb, wg, bg, lam)


def _lru_scan_kernel(af_ref, uf_ref, ar_ref, ur_ref, hf_ref, hr_ref, cf_ref, cr_ref, *, ts):
    @pl.when(pl.program_id(1) == 0)
    def _():
        cf_ref[...] = jnp.zeros_like(cf_ref)
        cr_ref[...] = jnp.zeros_like(cr_ref)

    ng = ts // SUBLANES
    row = lax.broadcasted_iota(jnp.int32, (SUBLANES, LRU_WIDTH), 0)

    def group_scan(a, u, carry, reverse):
        for d in (1, 2, 4):
            if reverse:
                keep, sh = row < SUBLANES - d, SUBLANES - d
            else:
                keep, sh = row >= d, d
            ap = jnp.where(keep, pltpu.roll(a, sh, 0), 1.0)
            up = jnp.where(keep, pltpu.roll(u, sh, 0), 0.0)
            u = a * up + u
            a = a * ap
        return u + a * carry

    def body(g, carry):
        cf, cr = carry
        r0 = pl.multiple_of(g * SUBLANES, SUBLANES)
        hf = group_scan(af_ref[0, pl.ds(r0, SUBLANES), :], uf_ref[0, pl.ds(r0, SUBLANES), :], cf, False)
        hf_ref[0, pl.ds(r0, SUBLANES), :] = hf
        r1 = pl.multiple_of((ng - 1 - g) * SUBLANES, SUBLANES)
        hr = group_scan(ar_ref[0, pl.ds(r1, SUBLANES), :], ur_ref[0, pl.ds(r1, SUBLANES), :], cr, True)
        hr_ref[0, pl.ds(r1, SUBLANES), :] = hr
        return (jnp.broadcast_to(hf[SUBLANES - 1:, :], hf.shape), jnp.broadcast_to(hr[:1, :], hr.shape))

    cf, cr = lax.fori_loop(0, ng, body, (cf_ref[...], cr_ref[...]))
    cf_ref[...] = cf
    cr_ref[...] = cr


def _lru_scan(af, uf, ar, ur):
    batch, seq, _ = af.shape
    ts = _pick(seq, 512)
    nt = seq // ts
    fwd = pl.BlockSpec((1, ts, LRU_WIDTH), lambda b, i: (b, i, 0))
    rev = pl.BlockSpec((1, ts, LRU_WIDTH), lambda b, i: (b, nt - 1 - i, 0))
    shp = jax.ShapeDtypeStruct(af.shape, F32)
    return pl.pallas_call(
        functools.partial(_lru_scan_kernel, ts=ts),
        grid=(batch, nt),
        in_specs=[fwd, fwd, rev, rev],
        out_specs=[fwd, rev],
        out_shape=[shp, shp],
        scratch_shapes=[pltpu.VMEM((SUBLANES, LRU_WIDTH), F32), pltpu.VMEM((SUBLANES, LRU_WIDTH), F32)],
        compiler_params=_cparams("parallel", "arbitrary"),
        name="lru_scan",
    )(af, uf, ar, ur)


def _gelu_tanh(x):
    return 0.5 * x * (1.0 + jnp.tanh(np.sqrt(2.0 / np.pi).astype(np.float32) * (x + 0.044715 * (x * x * x))))


def _branch_out_kernel(o0, o1, o2, l0, l1, l2, hf_ref, hr_ref, gd_ref, ya_ref, yd_ref, *nat_refs, tm):
    def natural(ref, g, buf):
        dil = DIL_CONFIGS[g][1]
        if dil == 1:
            return ref[0, 0]
        chunks = BRANCH_W // LANES
        for r in range(dil):
            for c in range(chunks):
                buf[c, pl.ds(r, tm // dil, stride=dil), :] = ref[0, r, :, c * LANES:(c + 1) * LANES]
        return jnp.concatenate([buf[c] for c in range(chunks)], axis=1)

    outs = (o0[0, 0], natural(o1, 1, nat_refs[0]), natural(o2, 2, nat_refs[1]))
    lse = (l0[0, 0], natural(l1, 1, nat_refs[2]), natural(l2, 2, nat_refs[3]))
    m = jnp.maximum(jnp.maximum(lse[0], lse[1]), lse[2])
    e = [jnp.exp(v - m) for v in lse]
    tot = e[0] + e[1] + e[2]
    ya = (e[0] / tot) * outs[0] + (e[1] / tot) * outs[1] + (e[2] / tot) * outs[2]
    ya_ref[...] = ya.astype(BF16)
    yd_ref[...] = ((hf_ref[...] + hr_ref[...]) * _gelu_tanh(gd_ref[...])).astype(BF16)


def _branch_out(oa, la, hf, hr, pd, seq):
    t = hf.shape[0]
    tm = _pick(seq, 256)
    nseq = seq // tm
    blk = pl.BlockSpec((tm, BRANCH_W), lambda i: (i, 0))
    cls = [pl.BlockSpec((1, dil, tm // dil, BRANCH_W), lambda i: (i // nseq, 0, i % nseq, 0))
           for _, dil in DIL_CONFIGS]
    shp = jax.ShapeDtypeStruct((t, BRANCH_W), BF16)
    return pl.pallas_call(
        functools.partial(_branch_out_kernel, tm=tm),
        grid=(t // tm,),
        in_specs=cls + cls + [blk, blk, pl.BlockSpec((tm, BRANCH_W), lambda i: (i, 1))],
        out_specs=[blk, blk],
        out_shape=[shp, shp],
        scratch_shapes=[pltpu.VMEM((BRANCH_W // LANES, tm, LANES), F32)] * 4,
        compiler_params=_cparams("parallel"),
        name="branch_out",
    )(*oa, *la, hf, hr, pd)


def _gate_merge_out_kernel(x_ref, h_ref, ya, yb, yc, yd, g0, g1, g2, g3, wb_ref, wo_ref, o_ref):
    j = pl.program_id(1)

    @pl.when(j == 0)
    def _():
        o_ref[...] = jnp.zeros_like(o_ref)

    h = h_ref[...]
    merged = None
    for nbr, (y_ref, wg_ref) in enumerate(zip((ya, yb, yc, yd), (g0, g1, g2, g3))):
        gate = jax.nn.sigmoid(jnp.dot(h, wg_ref[...], preferred_element_type=F32))
        term = gate * jnp.dot(y_ref[...], wb_ref[nbr], preferred_element_type=F32)
        merged = term if merged is None else merged + term
    o_ref[...] += jnp.dot(merged.astype(BF16), wo_ref[...], preferred_element_type=F32)

    @pl.when(j == pl.num_programs(1) - 1)
    def _():
        o_ref[...] = x_ref[...] + o_ref[...]


def _gate_merge_out(x, h, ys, wg, wb, wo):
    t = x.shape[0]
    tm, tn = _pick(t, 512), 512
    nj = D_MODEL // tn
    row = lambda i, j: (i, 0)
    gate_specs = [pl.BlockSpec((D_MODEL, tn), functools.partial(lambda i, j, nbr: (0, nbr * nj + j), nbr=nbr))
                  for nbr in range(N_BRANCH)]
    return pl.pallas_call(
        _gate_merge_out_kernel,
        grid=(t // tm, nj),
        in_specs=[pl.BlockSpec((tm, D_MODEL), row), pl.BlockSpec((tm, D_MODEL), row)]
                 + [pl.BlockSpec((tm, BRANCH_W), row)] * N_BRANCH + gate_specs
                 + [pl.BlockSpec((N_BRANCH, BRANCH_W, tn), lambda i, j: (0, 0, j)),
                    pl.BlockSpec((tn, D_MODEL), lambda i, j: (j, 0))],
        out_specs=pl.BlockSpec((tm, D_MODEL), row),
        out_shape=jax.ShapeDtypeStruct((t, D_MODEL), F32),
        compiler_params=_cparams("parallel", "arbitrary"),
        name="gate_merge_out",
    )(x, h, *ys, wg, wg, wg, wg, wb, wo)


def _block_diag(w):
    eye = jnp.eye(LRU_BLOCKS, dtype=w.dtype)
    return jnp.einsum('nef,nm->nemf', w, eye).reshape(LRU_WIDTH, LRU_WIDTH)


def _prep_layer(l, ffn1_norm, ffn1_w1, ffn1_w3, ffn1_w2, mix_norm, w_in, mla_q_norm, mla_w_uq, mla_kv_norm,
                mla_w_ukv, lru_conv_w, lru_conv_b, lru_w_a, lru_b_a, lru_w_x, lru_b_x, lru_lambda, sink_logits,
                w_branch, w_out, ffn2_norm, ffn2_w1, ffn2_w3, ffn2_w2):
    row = lambda v: v.reshape(1, -1)
    wi = w_in[l]
    c0, c1, c2, c3 = A_COLS, A_COLS + B_COLS, A_COLS + B_COLS + C_COLS, A_COLS + B_COLS + C_COLS + D_COLS
    w_a = wi[:, :c0].reshape(D_MODEL, 3, N_DIL, A_HEADS * HEAD_DIM).transpose(0, 2, 1, 3).reshape(D_MODEL, A_COLS)
    wb = wi[:, c0:c1]
    zeros = lambda n: jnp.zeros((D_MODEL, n), wi.dtype)
    w_b = jnp.concatenate([wb[:, :Q_LORA + KV_LORA], zeros(QK_NOPE), wb[:, Q_LORA + KV_LORA:],
                           zeros(SLOT - QK_NOPE - QK_ROPE)], axis=1)
    w_cbd = jnp.concatenate([wi[:, c1:c2], w_b, wi[:, c2:c3]], axis=1)
    wq = mla_w_uq[l].reshape(Q_LORA, B_HEADS, QK_NOPE + QK_ROPE)
    wq = jnp.pad(wq, ((0, 0), (0, 0), (0, SLOT - QK_NOPE - QK_ROPE))).reshape(Q_LORA, B_HEADS * SLOT)
    wkv = mla_w_ukv[l].reshape(KV_LORA, B_HEADS, QK_NOPE + V_DIM)
    wk = jnp.pad(wkv[:, :, :QK_NOPE], ((0, 0), (0, 0), (0, SLOT - QK_NOPE))).reshape(KV_LORA, B_HEADS * SLOT)
    wv = wkv[:, :, QK_NOPE:].reshape(KV_LORA, B_HEADS * V_DIM)
    wg = jnp.concatenate([_block_diag(lru_w_a[l, 0]), _block_diag(lru_w_x[l, 0]),
                          _block_diag(lru_w_a[l, 1]), _block_diag(lru_w_x[l, 1])], axis=1)
    bg = jnp.concatenate([lru_b_a[l, 0], lru_b_x[l, 0], lru_b_a[l, 1], lru_b_x[l, 1]]).reshape(1, -1)
    return dict(
        ffn1=(row(ffn1_norm[l]), ffn1_w1[l].astype(BF16), ffn1_w3[l].astype(BF16), ffn1_w2[l].astype(BF16)),
        ffn2=(row(ffn2_norm[l]), ffn2_w1[l].astype(BF16), ffn2_w3[l].astype(BF16), ffn2_w2[l].astype(BF16)),
        mix_norm=row(mix_norm[l]),
        w_a=w_a.astype(BF16), w_cbd=w_cbd.astype(BF16), w_g=wi[:, c3:].astype(BF16),
        qn=row(mla_q_norm[l]), kvn=row(mla_kv_norm[l]),
        wq=wq.astype(BF16), wk=wk.astype(BF16), wv=wv.astype(BF16),
        conv_w=lru_conv_w[l], conv_b=row(lru_conv_b[l]), wg=wg.astype(BF16), bg=bg, lam=lru_lambda[l],
        sink=sink_logits[l], w_branch=w_branch[l].astype(BF16), w_out=w_out[l].astype(BF16),
    )


def _rope_slot_tables(seq):
    inv = ROPE_THETA ** (-jnp.arange(0, QK_ROPE, 2, dtype=F32) / QK_ROPE)
    ang = jnp.arange(seq, dtype=F32)[:, None] * inv[None, :]
    cos, sin = jnp.cos(ang), jnp.sin(ang)
    scale = (QK_NOPE + QK_ROPE) ** -0.5 * np.log2(np.e)
    z = lambda n: jnp.zeros((seq, n), F32)
    tail = SLOT - QK_NOPE - QK_ROPE
    cos_q = jnp.concatenate([jnp.full((seq, QK_NOPE), scale, F32), cos * scale, cos * scale, z(tail)], axis=1)
    sin_q = jnp.concatenate([z(QK_NOPE), -sin * scale, sin * scale, z(tail)], axis=1)
    cos_k = jnp.concatenate([z(QK_NOPE), cos, cos, z(tail)], axis=1)
    sin_k = jnp.concatenate([z(QK_NOPE), -sin, sin, z(tail)], axis=1)
    return cos_q, sin_q, cos_k, sin_k


def _layer(x, w, batch, seq, final_g):
    t = batch * seq
    x = _ffn(x, *w['ffn1'], final_g, False)
    *pas, h = _proj_a(x, w['mix_norm'], w['w_a'], batch, seq)
    pc, pb, pd = _proj_bcd(h, w['w_cbd'], ((C_COLS, BF16), (B_PAD_COLS, F32), (D_COLS, F32)))

    oa, la = [], []
    for pa, (window, dil) in zip(pas, DIL_CONFIGS):
        o, lse = _band_attention(
            pa, q_col=0, k_col=1, v_col=2, kv_width=A_HEADS * HEAD_DIM,
            n_heads=A_HEADS, n_kv=A_HEADS, radius=window // (2 * dil), step=dil, sink=None,
            out_dtype=F32, emit_lse=True)
        oa.append(o)
        la.append(lse)

    q, k, vt = _mla_prep(pb, w['qn'], w['kvn'], w['wq'], w['wk'], w['wv'], _rope_slot_tables(seq), batch, seq)
    yb = _mla_attn(q.reshape(batch, seq, -1), k.reshape(batch, seq, -1), vt, batch, seq)

    kvw = C_KV_HEADS * HEAD_DIM
    yc = _band_attention(
        pc.reshape(batch, 1, seq, C_COLS),
        q_col=0, k_col=C_HEADS * HEAD_DIM // kvw, v_col=C_HEADS * HEAD_DIM // kvw + 1, kv_width=kvw,
        n_heads=C_HEADS, n_kv=C_KV_HEADS, radius=C_RADIUS, step=1, sink=w['sink'],
        out_dtype=BF16, emit_lse=False)[0]

    af, uf, ar, ur = _lru_gates(pd.reshape(batch, seq, D_COLS), w['conv_w'], w['conv_b'], w['wg'], w['bg'],
                                w['lam'], batch, seq)
    hf, hr = _lru_scan(af, uf, ar, ur)

    ya, yd = _branch_out(oa, la, hf.reshape(t, LRU_WIDTH), hr.reshape(t, LRU_WIDTH), pd, seq)
    x = _gate_merge_out(x, h, (ya, yb.reshape(t, BRANCH_W), yc.reshape(t, BRANCH_W), yd),
                        w['w_g'], w['w_branch'], w['w_out'])
    return _ffn(x, *w['ffn2'], final_g, final_g is not None and w.get('last', False))


def _trunk(x, layers, final_norm):
    batch, seq, _ = x.shape
    h = x.reshape(batch * seq, D_MODEL)
    fg = final_norm.reshape(1, -1)
    for l, w in enumerate(layers):
        h = _layer(h, dict(w, last=(l == len(layers) - 1)), batch, seq, fg)
    return h.reshape(batch, seq, D_MODEL)


def kernel(x_prompt, x_sample, ffn1_norm, ffn1_w1, ffn1_w3, ffn1_w2, mix_norm, w_in, mla_q_norm, mla_w_uq, mla_kv_norm, mla_w_ukv, lru_conv_w, lru_conv_b, lru_w_a, lru_b_a, lru_w_x, lru_b_x, lru_lambda, sink_logits, w_branch, w_out, ffn2_norm, ffn2_w1, ffn2_w3, ffn2_w2, final_norm):
    layers = [_prep_layer(l, ffn1_norm, ffn1_w1, ffn1_w3, ffn1_w2, mix_norm, w_in, mla_q_norm, mla_w_uq,
                          mla_kv_norm, mla_w_ukv, lru_conv_w, lru_conv_b, lru_w_a, lru_b_a, lru_w_x, lru_b_x,
                          lru_lambda, sink_logits, w_branch, w_out, ffn2_norm, ffn2_w1, ffn2_w3, ffn2_w2)
              for l in range(DEPTH)]
    return (_trunk(x_prompt, layers, final_norm), _trunk(x_sample, layers, final_norm))
```

```python
import functools

import numpy as np
import jax
import jax.numpy as jnp
from jax import lax
from jax.experimental import pallas as pl
from jax.experimental.pallas import tpu as pltpu

F32 = jnp.float32
BF16 = jnp.bfloat16

D_MODEL = 2048
DEPTH = 2
HEAD_DIM = 64
N_BRANCH = 4
BRANCH_W = 512
DIL_CONFIGS = ((128, 1), (512, 4), (2048, 16))
N_DIL = 3
A_HEADS = 8
B_HEADS = 8
Q_LORA = 384
KV_LORA = 128
QK_NOPE = 64
QK_ROPE = 32
V_DIM = 64
ROPE_THETA = 10000.0
C_HEADS = 8
C_KV_HEADS = 2
C_RADIUS = 128
LRU_WIDTH = 512
LRU_BLOCKS = 8
LRU_BLOCK = 64
CONV_W = 4
CONV_PAD_L = 2
LRU_C = 8.0
D_FF = 5632
EPS = 1e-6
NEG = -1e30

A_COLS = 3 * N_DIL * A_HEADS * HEAD_DIM
B_COLS = Q_LORA + KV_LORA + QK_ROPE
C_COLS = (C_HEADS + 2 * C_KV_HEADS) * HEAD_DIM
D_COLS = 2 * LRU_WIDTH
G_COLS = N_BRANCH * D_MODEL

LANES = 128
SUBLANES = 8
SLOT = LANES
B_PAD_COLS = Q_LORA + KV_LORA + SLOT
ROW_CHUNK = 128
VMEM_LIMIT = 52 * 1024 * 1024
VMEM_LIMIT_FFN = 54 * 1024 * 1024


def _cparams(*sem, vmem=VMEM_LIMIT):
    return pltpu.CompilerParams(dimension_semantics=sem, vmem_limit_bytes=vmem)


def _rms(x, g):
    return x * lax.rsqrt(jnp.mean(x * x, axis=-1, keepdims=True) + EPS) * g


def _pick(n, pref):
    t = min(n, pref)
    while n % t:
        t //= 2
    return t


def _ffn_kernel(x_ref, g_ref, w1_ref, w3_ref, w2_ref, fg_ref, o_ref, h_ref, *, final_norm):
    j = pl.program_id(1)
    n_chunks = x_ref.shape[0] // ROW_CHUNK

    def chunk(r):
        return pl.ds(pl.multiple_of(r * ROW_CHUNK, ROW_CHUNK), ROW_CHUNK)

    @pl.when(j == 0)
    def _():
        def norm_step(r, carry):
            h_ref[chunk(r), :] = _rms(x_ref[chunk(r), :], g_ref[...]).astype(BF16)
            o_ref[chunk(r), :] = jnp.zeros((ROW_CHUNK, D_MODEL), F32)
            return carry
        lax.fori_loop(0, n_chunks, norm_step, 0)

    half = min(x_ref.shape[0], 512)

    def rows_step(r, carry):
        rows = pl.ds(pl.multiple_of(r * half, half), half)
        h = h_ref[rows, :]
        a = jnp.dot(h, w1_ref[...], preferred_element_type=F32)
        b = jnp.dot(h, w3_ref[...], preferred_element_type=F32)
        act = (a * jax.nn.sigmoid(a) * b).astype(BF16)
        for c in range(0, D_MODEL, half):
            o_ref[rows, c:c + half] += jnp.dot(act, w2_ref[:, c:c + half], preferred_element_type=F32)
        return carry

    lax.fori_loop(0, x_ref.shape[0] // half, rows_step, 0)

    @pl.when(j == pl.num_programs(1) - 1)
    def _():
        def out_step(r, carry):
            y = x_ref[chunk(r), :] + 0.5 * o_ref[chunk(r), :]
            if final_norm:
                y = _rms(y, fg_ref[...])
            o_ref[chunk(r), :] = y
            return carry
        lax.fori_loop(0, n_chunks, out_step, 0)


def _ffn(x, g, w1, w3, w2, fg, final_norm):
    t = x.shape[0]
    tm, tf = _pick(t, 1024), 512
    return pl.pallas_call(
        functools.partial(_ffn_kernel, final_norm=final_norm),
        grid=(t // tm, D_FF // tf),
        in_specs=[
            pl.BlockSpec((tm, D_MODEL), lambda i, j: (i, 0)),
            pl.BlockSpec((1, D_MODEL), lambda i, j: (0, 0)),
            pl.BlockSpec((D_MODEL, tf), lambda i, j: (0, j)),
            pl.BlockSpec((D_MODEL, tf), lambda i, j: (0, j)),
            pl.BlockSpec((tf, D_MODEL), lambda i, j: (j, 0)),
            pl.BlockSpec((1, D_MODEL), lambda i, j: (0, 0)),
        ],
        out_specs=pl.BlockSpec((tm, D_MODEL), lambda i, j: (i, 0)),
        out_shape=jax.ShapeDtypeStruct((t, D_MODEL), F32),
        scratch_shapes=[pltpu.VMEM((tm, D_MODEL), BF16)],
        compiler_params=_cparams("parallel", "arbitrary", vmem=VMEM_LIMIT_FFN),
        name="ffn",
    )(x, g, w1, w3, w2, fg)


def _proj_bcd_kernel(h_ref, w_ref, *o_refs):
    r = jnp.dot(h_ref[...], w_ref[...], preferred_element_type=F32)
    c = 0
    for o_ref in o_refs:
        wd = o_ref.shape[1]
        o_ref[...] = r[:, c:c + wd].astype(o_ref.dtype)
        c += wd


def _proj_bcd(h, w, outs):
    t, n = h.shape[0], w.shape[1]
    tm = _pick(t, 512)
    return pl.pallas_call(
        _proj_bcd_kernel,
        grid=(t // tm,),
        in_specs=[pl.BlockSpec((tm, D_MODEL), lambda i: (i, 0)),
                  pl.BlockSpec((D_MODEL, n), lambda i: (0, 0))],
        out_specs=[pl.BlockSpec((tm, wd), lambda i: (i, 0)) for wd, _ in outs],
        out_shape=[jax.ShapeDtypeStruct((t, wd), dt) for wd, dt in outs],
        compiler_params=_cparams("parallel"),
        name="proj_bcd",
    )(h, w)


def _proj_a_kernel(x_ref, g_ref, w_ref, o1_ref, o2_ref, o3_ref, hout_ref, hn_ref, ha_ref, hb_ref, *, tm):
    j = pl.program_id(1)
    chunks = D_MODEL // LANES
    o_refs = (o1_ref, o2_ref, o3_ref)
    h_ref = (ha_ref, hb_ref)

    def regroup(g):
        dil = DIL_CONFIGS[g][1]
        n = tm // dil
        for r in range(dil):
            for c in range(chunks):
                h_ref[g % 2][r * n:(r + 1) * n, c * LANES:(c + 1) * LANES] = (
                    hn_ref[c, pl.ds(r, n, stride=dil), :].astype(BF16))

    def project(g):
        dil = DIL_CONFIGS[g][1]
        n = tm // dil
        res = jnp.dot(h_ref[g % 2][...], w_ref[...], preferred_element_type=F32)
        for r in range(dil):
            o_refs[g][0, r] = res[r * n:(r + 1) * n].astype(BF16)

    @pl.when(j == 0)
    def _():
        hn = _rms(x_ref[...], g_ref[...])
        for c in range(chunks):
            hn_ref[c] = hn[:, c * LANES:(c + 1) * LANES]
        h_ref[0][...] = hn.astype(BF16)
        hout_ref[...] = hn.astype(BF16)
        project(0)
        regroup(1)

    for g in range(1, N_DIL):
        @pl.when(j == g)
        def _(g=g):
            project(g)
            if g + 1 < N_DIL:
                regroup(g + 1)


def _proj_a(x, g, w, batch, seq):
    t = x.shape[0]
    tm = _pick(seq, 512)
    nseq = seq // tm
    gw = A_COLS // N_DIL
    out_specs = [pl.BlockSpec((1, dil, tm // dil, gw), lambda i, j: (i // nseq, 0, i % nseq, 0))
                 for _, dil in DIL_CONFIGS]
    out_shape = [jax.ShapeDtypeStruct((batch, dil, seq // dil, gw), BF16) for _, dil in DIL_CONFIGS]
    out_specs.append(pl.BlockSpec((tm, D_MODEL), lambda i, j: (i, 0)))
    out_shape.append(jax.ShapeDtypeStruct((t, D_MODEL), BF16))
    return pl.pallas_call(
        functools.partial(_proj_a_kernel, tm=tm),
        grid=(t // tm, N_DIL),
        in_specs=[pl.BlockSpec((tm, D_MODEL), lambda i, j: (i, 0)),
                  pl.BlockSpec((1, D_MODEL), lambda i, j: (0, 0)),
                  pl.BlockSpec((D_MODEL, gw), lambda i, j: (0, j))],
        out_specs=out_specs,
        out_shape=out_shape,
        scratch_shapes=[pltpu.VMEM((D_MODEL // LANES, tm, LANES), F32), pltpu.VMEM((tm, D_MODEL), BF16),
                        pltpu.VMEM((tm, D_MODEL), BF16)],
        compiler_params=_cparams("parallel", "arbitrary"),
        name="proj_a",
    )(x, g, w)


def _band_kernel(*refs, n_heads, n_kv, radius, sb, tq, has_sink, emit_lse):
    if has_sink:
        sink_ref, refs = refs[0], refs[1:]
    bias_ref, q_ref, kp_ref, kc_ref, kn_ref, vp_ref, vc_ref, vn_ref = refs[:8]
    o_ref = refs[8]
    kx_ref, vx_ref, vt_ref, qt_ref = refs[-4:]
    for x_ref, (p_ref, c_ref, n_ref) in ((kx_ref, (kp_ref, kc_ref, kn_ref)), (vx_ref, (vp_ref, vc_ref, vn_ref))):
        x_ref[0:radius, :] = p_ref[0, 0, tq - radius:, :]
        x_ref[radius:radius + tq, :] = c_ref[0, 0]
        x_ref[radius + tq:, :] = n_ref[0, 0, :radius, :]
    vt_ref[...] = vx_ref[...].astype(F32).T.astype(BF16)
    qt_ref[...] = (q_ref[0, 0].astype(F32) * (HEAD_DIM ** -0.5)).T.astype(BF16)
    w = sb + 2 * radius
    rep = n_heads // n_kv
    heads = range(n_heads)
    zeros = jnp.zeros((HEAD_DIM, sb), BF16)
    nsub = tq // sb
    tile, last_tile = pl.program_id(2), pl.num_programs(2) - 1
    for u in range(nsub):
        rows = slice(u * sb, (u + 1) * sb)
        win = slice(u * sb, u * sb + w)
        kind = 0
        if u == 0:
            kind = kind + (tile == 0).astype(jnp.int32)
        if u == nsub - 1:
            kind = kind + 2 * (tile == last_tile).astype(jnp.int32)
        scores = []
        for h in heads:
            g = h // rep
            qh = qt_ref[h * HEAD_DIM:(h + 1) * HEAD_DIM, rows]
            rhs = jnp.concatenate([qh, zeros] if g % 2 == 0 else [zeros, qh], axis=0)
            pair = slice((g // 2) * 2 * HEAD_DIM, (g // 2 + 1) * 2 * HEAD_DIM)
            scores.append(jnp.dot(kx_ref[win, pair], rhs, preferred_element_type=F32) + bias_ref[kind, h])
        probs, stats = [], []
        for h in heads:
            m = jnp.max(scores[h], axis=0, keepdims=True)
            if has_sink:
                m = jnp.maximum(m, sink_ref[h])
            p = jnp.exp(scores[h] - m)
            l = jnp.sum(p, axis=0, keepdims=True)
            if has_sink:
                l = l + jnp.exp(sink_ref[h] - m)
            probs.append(p.astype(BF16))
            stats.append((m, l))
        outs, lses = [], []
        for h in heads:
            g = h // rep
            m, l = stats[h]
            outs.append(jnp.dot(vt_ref[g * HEAD_DIM:(g + 1) * HEAD_DIM, win], probs[h],
                                preferred_element_type=F32) / l)
            lses.append(jnp.broadcast_to(m + jnp.log(l), (HEAD_DIM, sb)))
        o_ref[0, 0, rows, :] = jnp.concatenate(outs, axis=0).T.astype(o_ref.dtype)
        if emit_lse:
            refs[9][0, 0, rows, :] = jnp.concatenate(lses, axis=0).T


def _alibi(n):
    return np.asarray(2.0 ** (-8.0 * np.arange(1, n + 1) / n), dtype=np.float32)


def _band_bias(n_heads, radius, step, sb):
    w = sb + 2 * radius
    shape = (4, n_heads, w, sb)
    kind, h, kj, qi = (lax.broadcasted_iota(jnp.int32, shape, d) for d in range(4))
    rel = jnp.abs(kj - radius - qi)
    kpos = kj - radius
    ok = (rel <= radius) & ((kpos >= 0) | (kind % 2 == 0)) & ((kpos < sb) | (kind < 2))
    slopes = jnp.asarray(_alibi(n_heads))[h]
    return jnp.where(ok, -slopes * (step * rel).astype(F32), NEG)


def _band_attention(arr, *, q_col, k_col, v_col, kv_width, n_heads, n_kv, radius, step, sink, out_dtype,
                    emit_lse):
    batch, n_classes, length, _ = arr.shape
    qw = n_heads * HEAD_DIM
    tq = _pick(length, 512)
    sb = min(tq, 128)
    nblk = length // tq

    def kv_spec(col, shift):
        return pl.BlockSpec((1, 1, tq, kv_width), lambda b, r, i: (b, r, jnp.clip(i + shift, 0, nblk - 1), col))

    bias = _band_bias(n_heads, radius, step, sb)
    in_specs = [pl.BlockSpec(bias.shape, lambda b, r, i: (0, 0, 0, 0)),
                pl.BlockSpec((1, 1, tq, qw), lambda b, r, i: (b, r, i, q_col)),
                kv_spec(k_col, -1), kv_spec(k_col, 0), kv_spec(k_col, 1),
                kv_spec(v_col, -1), kv_spec(v_col, 0), kv_spec(v_col, 1)]
    args = [bias] + [arr] * 7
    if sink is not None:
        in_specs = [pl.BlockSpec(memory_space=pltpu.SMEM)] + in_specs
        args = [sink] + args
    o_spec = pl.BlockSpec((1, 1, tq, qw), lambda b, r, i: (b, r, i, 0))
    o_shape = jax.ShapeDtypeStruct((batch, n_classes, length, qw), out_dtype)
    out_specs, out_shape = [o_spec], [o_shape]
    if emit_lse:
        out_specs, out_shape = [o_spec, o_spec], [o_shape, jax.ShapeDtypeStruct(o_shape.shape, F32)]
    return pl.pallas_call(
        functools.partial(_band_kernel, n_heads=n_heads, n_kv=n_kv, radius=radius, sb=sb, tq=tq,
                          has_sink=sink is not None, emit_lse=emit_lse),
        grid=(batch, n_classes, nblk),
        in_specs=in_specs,
        out_specs=out_specs,
        out_shape=out_shape,
        scratch_shapes=[pltpu.VMEM((tq + 2 * radius, kv_width), BF16), pltpu.VMEM((tq + 2 * radius, kv_width), BF16),
                        pltpu.VMEM((kv_width, tq + 2 * radius), BF16), pltpu.VMEM((qw, tq), BF16)],
        compiler_params=_cparams("parallel", "parallel", "arbitrary"),
        name="band_attn",
    )(*args)


def _rope_slot(x, cos_t, sin_t):
    lane = lax.broadcasted_iota(jnp.int32, x.shape, 1)
    first = (lane >= QK_NOPE) & (lane < QK_NOPE + QK_ROPE // 2)
    partner = jnp.where(first, pltpu.roll(x, SLOT - QK_ROPE // 2, 1), pltpu.roll(x, QK_ROPE // 2, 1))
    return x * cos_t + partner * sin_t


def _mla_prep_kernel(pb_ref, qn_ref, kvn_ref, wq_ref, wk_ref, wv_ref, cq_ref, sq_ref, ck_ref, sk_ref,
                     q_out, k_out, vt_out):
    pb = pb_ref[...]
    cq = _rms(pb[:, :Q_LORA], qn_ref[...]).astype(BF16)
    ckv = _rms(pb[:, Q_LORA:Q_LORA + KV_LORA], kvn_ref[...]).astype(BF16)
    kr = pb[:, Q_LORA + KV_LORA:]
    q = jnp.dot(cq, wq_ref[...], preferred_element_type=F32)
    k = jnp.dot(ckv, wk_ref[...], preferred_element_type=F32)
    vt_out[0] = jnp.dot(ckv, wv_ref[...], preferred_element_type=F32).T.astype(BF16)
    kr = _rope_slot(kr, ck_ref[...], sk_ref[...])
    for h in range(B_HEADS):
        sl = slice(h * SLOT, (h + 1) * SLOT)
        q_out[:, sl] = _rope_slot(q[:, sl], cq_ref[...], sq_ref[...]).astype(BF16)
        k_out[:, sl] = (k[:, sl] + kr).astype(BF16)


def _mla_prep(pb, qn, kvn, wq, wk, wv, tabs, batch, seq):
    t = pb.shape[0]
    tm = _pick(seq, 512)
    nseq = seq // tm
    row = lambda i: (i, 0)
    fixed = lambda i: (0, 0)
    tab = pl.BlockSpec((tm, SLOT), lambda i: (i % nseq, 0))
    return pl.pallas_call(
        _mla_prep_kernel,
        grid=(t // tm,),
        in_specs=[pl.BlockSpec((tm, B_PAD_COLS), row),
                  pl.BlockSpec((1, Q_LORA), fixed), pl.BlockSpec((1, KV_LORA), fixed),
                  pl.BlockSpec((Q_LORA, B_HEADS * SLOT), fixed),
                  pl.BlockSpec((KV_LORA, B_HEADS * SLOT), fixed),
                  pl.BlockSpec((KV_LORA, B_HEADS * V_DIM), fixed),
                  tab, tab, tab, tab],
        out_specs=[pl.BlockSpec((tm, B_HEADS * SLOT), row), pl.BlockSpec((tm, B_HEADS * SLOT), row),
                   pl.BlockSpec((1, B_HEADS * V_DIM, tm), lambda i: (i // nseq, 0, i % nseq))],
        out_shape=[jax.ShapeDtypeStruct((t, B_HEADS * SLOT), BF16), jax.ShapeDtypeStruct((t, B_HEADS * SLOT), BF16),
                   jax.ShapeDtypeStruct((batch, B_HEADS * V_DIM, seq), BF16)],
        compiler_params=_cparams("parallel"),
        name="mla_prep",
    )(pb, qn, kvn, wq, wk, wv, *tabs)


def _mla_attn_kernel(q_ref, k_ref, vt_ref, o_ref, st_ref, *, seq, kc):
    chunks = [slice(c * kc, (c + 1) * kc) for c in range(seq // kc)]

    def qk(hh, rows):
        sl = slice(hh * SLOT, (hh + 1) * SLOT)
        st = lax.dot_general(k_ref[0, rows, sl], q_ref[0, :, sl], (((1,), (1,)), ((), ())),
                             preferred_element_type=F32)
        st_ref[hh, rows, :] = st
        return jnp.max(st, axis=0, keepdims=True)

    def pv(hh, rows, m):
        p = jnp.exp2(st_ref[hh, rows, :] - m)
        return (jnp.sum(p, axis=0, keepdims=True),
                jnp.dot(vt_ref[0, :, rows], p.astype(BF16), preferred_element_type=F32))

    def add(acc, new):
        return new if acc is None else (acc[0] + new[0], acc[1] + new[1])

    m0 = functools.reduce(jnp.maximum, [qk(0, rows) for rows in chunks])
    m1 = acc0 = acc1 = None
    for rows in chunks:
        acc0 = add(acc0, pv(0, rows, m0))
        mc = qk(1, rows)
        m1 = mc if m1 is None else jnp.maximum(m1, mc)
    for rows in chunks:
        acc1 = add(acc1, pv(1, rows, m1))
    outs = [acc[1][hh * V_DIM:(hh + 1) * V_DIM] / acc[0] for hh, acc in enumerate((acc0, acc1))]
    o_ref[0] = jnp.concatenate(outs, axis=0).T.astype(o_ref.dtype)


def _mla_attn(q, k, vt, batch, seq):
    tq = _pick(seq, 256)
    kc = _pick(seq, 512)
    pairs = B_HEADS // 2
    return pl.pallas_call(
        functools.partial(_mla_attn_kernel, seq=seq, kc=kc),
        scratch_shapes=[pltpu.VMEM((2, seq, tq), F32)],
        grid=(batch, pairs, seq // tq),
        in_specs=[pl.BlockSpec((1, tq, 2 * SLOT), lambda b, hp, i: (b, i, hp)),
                  pl.BlockSpec((1, seq, 2 * SLOT), lambda b, hp, i: (b, 0, hp)),
                  pl.BlockSpec((1, 2 * V_DIM, seq), lambda b, hp, i: (b, hp, 0))],
        out_specs=pl.BlockSpec((1, tq, 2 * V_DIM), lambda b, hp, i: (b, i, hp)),
        out_shape=jax.ShapeDtypeStruct((batch, seq, B_HEADS * V_DIM), BF16),
        compiler_params=_cparams("parallel", "parallel", "arbitrary"),
        name="mla_attn",
    )(q, k, vt)


def _softplus(x):
    return jnp.maximum(x, 0.0) + jnp.log1p(jnp.exp(-jnp.abs(x)))


def _lru_kernel(xf_ref, xfp_ref, xfn_ref, xr_ref, xrp_ref, xrn_ref, cw_ref, cb_ref, wg_ref, bg_ref, lam_ref,
                hf_ref, hr_ref, ext_ref, af_ref, uf_ref, ar_ref, ur_ref, cf_ref, cr_ref, *, ts):
    i = pl.program_id(1)
    nt = pl.num_programs(1)
    halo = SUBLANES

    @pl.when(i == 0)
    def _():
        cf_ref[...] = jnp.zeros_like(cf_ref)
        cr_ref[...] = jnp.zeros_like(cr_ref)

    def gates(d, tile, x_ref, xp_ref, xn_ref, a_ref, u_ref):
        ext_ref[0:halo, :] = jnp.where(tile > 0, xp_ref[0], 0.0)
        ext_ref[halo:halo + ts, :] = x_ref[0]
        ext_ref[halo + ts:, :] = jnp.where(tile < nt - 1, xn_ref[0], 0.0)
        xc = cb_ref[...]
        for tap in range(CONV_W):
            off = halo - CONV_PAD_L + tap
            xc = xc + ext_ref[off:off + ts, :] * cw_ref[tap:tap + 1, :]
        g = jnp.dot(xc.astype(BF16), wg_ref[d], preferred_element_type=F32) + bg_ref[d:d + 1, :]
        r = jax.nn.sigmoid(g[:, :LRU_WIDTH])
        ig = jax.nn.sigmoid(g[:, LRU_WIDTH:])
        log_a = -LRU_C * r * _softplus(-lam_ref[d:d + 1, :])
        a_ref[...] = jnp.exp(log_a)
        th = jnp.tanh(log_a)
        u_ref[...] = jnp.sqrt(-2.0 * th / (1.0 - th)) * (ig * xc)

    gates(0, i, xf_ref, xfp_ref, xfn_ref, af_ref, uf_ref)
    gates(1, nt - 1 - i, xr_ref, xrp_ref, xrn_ref, ar_ref, ur_ref)

    ng = ts // SUBLANES
    row = lax.broadcasted_iota(jnp.int32, (SUBLANES, LRU_WIDTH), 0)

    def group_scan(a, u, carry, reverse):
        for d in (1, 2, 4):
            if reverse:
                keep, sh = row < SUBLANES - d, SUBLANES - d
            else:
                keep, sh = row >= d, d
            ap = jnp.where(keep, pltpu.roll(a, sh, 0), 1.0)
            up = jnp.where(keep, pltpu.roll(u, sh, 0), 0.0)
            u = a * up + u
            a = a * ap
        return u + a * carry

    def body(g, carry):
        cf, cr = carry
        r0 = pl.multiple_of(g * SUBLANES, SUBLANES)
        hf = group_scan(af_ref[pl.ds(r0, SUBLANES), :], uf_ref[pl.ds(r0, SUBLANES), :], cf, False)
        hf_ref[0, pl.ds(r0, SUBLANES), :] = hf
        r1 = pl.multiple_of((ng - 1 - g) * SUBLANES, SUBLANES)
        hr = group_scan(ar_ref[pl.ds(r1, SUBLANES), :], ur_ref[pl.ds(r1, SUBLANES), :], cr, True)
        hr_ref[0, pl.ds(r1, SUBLANES), :] = hr
        return (jnp.broadcast_to(hf[SUBLANES - 1:, :], hf.shape), jnp.broadcast_to(hr[:1, :], hr.shape))

    cf, cr = lax.fori_loop(0, ng, body, (cf_ref[...], cr_ref[...]))
    cf_ref[...] = cf
    cr_ref[...] = cr


def _lru(pd, cw, cb, wg, bg, lam):
    batch, seq, _ = pd.shape
    ts = _pick(seq, 512)
    nt = seq // ts
    per = ts // SUBLANES
    last8 = seq // SUBLANES - 1

    def tile_specs(tile):
        return [pl.BlockSpec((1, ts, LRU_WIDTH), lambda b, i: (b, tile(i), 0)),
                pl.BlockSpec((1, SUBLANES, LRU_WIDTH), lambda b, i: (b, jnp.maximum(tile(i) * per - 1, 0), 0)),
                pl.BlockSpec((1, SUBLANES, LRU_WIDTH), lambda b, i: (b, jnp.minimum((tile(i) + 1) * per, last8), 0))]

    fwd_tile = lambda i: i
    rev_tile = lambda i: nt - 1 - i
    fixed2 = lambda b, i: (0, 0)
    shp = jax.ShapeDtypeStruct((batch, seq, LRU_WIDTH), F32)
    tile_f32 = pltpu.VMEM((ts, LRU_WIDTH), F32)
    return pl.pallas_call(
        functools.partial(_lru_kernel, ts=ts),
        grid=(batch, nt),
        in_specs=tile_specs(fwd_tile) + tile_specs(rev_tile)
                 + [pl.BlockSpec((CONV_W, LRU_WIDTH), fixed2), pl.BlockSpec((1, LRU_WIDTH), fixed2),
                    pl.BlockSpec((2, LRU_WIDTH, 2 * LRU_WIDTH), lambda b, i: (0, 0, 0)),
                    pl.BlockSpec((2, 2 * LRU_WIDTH), fixed2), pl.BlockSpec((2, LRU_WIDTH), fixed2)],
        out_specs=[pl.BlockSpec((1, ts, LRU_WIDTH), lambda b, i: (b, i, 0)),
                   pl.BlockSpec((1, ts, LRU_WIDTH), lambda b, i: (b, nt - 1 - i, 0))],
        out_shape=[shp, shp],
        scratch_shapes=[pltpu.VMEM((ts + 2 * SUBLANES, LRU_WIDTH), F32), tile_f32, tile_f32, tile_f32, tile_f32,
                        pltpu.VMEM((SUBLANES, LRU_WIDTH), F32), pltpu.VMEM((SUBLANES, LRU_WIDTH), F32)],
        compiler_params=_cparams("parallel", "arbitrary"),
        name="lru",
    )(pd, pd, pd, pd, pd, pd, cw, cb, wg, bg, lam)


def _gelu_tanh(x):
    return 0.5 * x * (1.0 + jnp.tanh(np.sqrt(2.0 / np.pi).astype(np.float32) * (x + 0.044715 * (x * x * x))))


def _branch_out_kernel(o0, o1, o2, l0, l1, l2, hf_ref, hr_ref, gd_ref, ya_ref, yd_ref, *nat_refs, tm):
    def natural(ref, g, buf):
        dil = DIL_CONFIGS[g][1]
        if dil == 1:
            return ref[0, 0]
        chunks = BRANCH_W // LANES
        for r in range(dil):
            for c in range(chunks):
                buf[c, pl.ds(r, tm // dil, stride=dil), :] = ref[0, r, :, c * LANES:(c + 1) * LANES]
        return jnp.concatenate([buf[c] for c in range(chunks)], axis=1)

    outs = (o0[0, 0], natural(o1, 1, nat_refs[0]), natural(o2, 2, nat_refs[1]))
    lse = (l0[0, 0], natural(l1, 1, nat_refs[2]), natural(l2, 2, nat_refs[3]))
    m = jnp.maximum(jnp.maximum(lse[0], lse[1]), lse[2])
    e = [jnp.exp(v - m) for v in lse]
    tot = e[0] + e[1] + e[2]
    ya = (e[0] / tot) * outs[0] + (e[1] / tot) * outs[1] + (e[2] / tot) * outs[2]
    ya_ref[...] = ya.astype(BF16)
    yd_ref[...] = ((hf_ref[...] + hr_ref[...]) * _gelu_tanh(gd_ref[...])).astype(BF16)


def _branch_out(oa, la, hf, hr, pd, seq):
    t = hf.shape[0]
    tm = _pick(seq, 256)
    nseq = seq // tm
    blk = pl.BlockSpec((tm, BRANCH_W), lambda i: (i, 0))
    cls = [pl.BlockSpec((1, dil, tm // dil, BRANCH_W), lambda i: (i // nseq, 0, i % nseq, 0))
           for _, dil in DIL_CONFIGS]
    shp = jax.ShapeDtypeStruct((t, BRANCH_W), BF16)
    return pl.pallas_call(
        functools.partial(_branch_out_kernel, tm=tm),
        grid=(t // tm,),
        in_specs=cls + cls + [blk, blk, pl.BlockSpec((tm, BRANCH_W), lambda i: (i, 1))],
        out_specs=[blk, blk],
        out_shape=[shp, shp],
        scratch_shapes=[pltpu.VMEM((BRANCH_W // LANES, tm, LANES), F32)] * 4,
        compiler_params=_cparams("parallel"),
        name="branch_out",
    )(*oa, *la, hf, hr, pd)


def _gate_merge_out_kernel(x_ref, h_ref, ya, yb, yc, yd, g0, g1, g2, g3, wb_ref, wo_ref, o_ref):
    j = pl.program_id(1)

    @pl.when(j == 0)
    def _():
        o_ref[...] = jnp.zeros_like(o_ref)

    h = h_ref[...]
    merged = None
    for nbr, (y_ref, wg_ref) in enumerate(zip((ya, yb, yc, yd), (g0, g1, g2, g3))):
        gate = jax.nn.sigmoid(jnp.dot(h, wg_ref[...], preferred_element_type=F32))
        term = gate * jnp.dot(y_ref[...], wb_ref[nbr], preferred_element_type=F32)
        merged = term if merged is None else merged + term
    o_ref[...] += jnp.dot(merged.astype(BF16), wo_ref[...], preferred_element_type=F32)

    @pl.when(j == pl.num_programs(1) - 1)
    def _():
        o_ref[...] = x_ref[...] + o_ref[...]


def _gate_merge_out(x, h, ys, wg, wb, wo):
    t = x.shape[0]
    tm, tn = _pick(t, 512), 512
    nj = D_MODEL // tn
    row = lambda i, j: (i, 0)
    gate_specs = [pl.BlockSpec((D_MODEL, tn), functools.partial(lambda i, j, nbr: (0, nbr * nj + j), nbr=nbr))
                  for nbr in range(N_BRANCH)]
    return pl.pallas_call(
        _gate_merge_out_kernel,
        grid=(t // tm, nj),
        in_specs=[pl.BlockSpec((tm, D_MODEL), row), pl.BlockSpec((tm, D_MODEL), row)]
                 + [pl.BlockSpec((tm, BRANCH_W), row)] * N_BRANCH + gate_specs
                 + [pl.BlockSpec((N_BRANCH, BRANCH_W, tn), lambda i, j: (0, 0, j)),
                    pl.BlockSpec((tn, D_MODEL), lambda i, j: (j, 0))],
        out_specs=pl.BlockSpec((tm, D_MODEL), row),
        out_shape=jax.ShapeDtypeStruct((t, D_MODEL), F32),
        compiler_params=_cparams("parallel", "arbitrary"),
        name="gate_merge_out",
    )(x, h, *ys, wg, wg, wg, wg, wb, wo)


def _block_diag(w):
    eye = jnp.eye(LRU_BLOCKS, dtype=w.dtype)
    return jnp.einsum('nef,nm->nemf', w, eye).reshape(LRU_WIDTH, LRU_WIDTH)


def _prep_layer(l, ffn1_norm, ffn1_w1, ffn1_w3, ffn1_w2, mix_norm, w_in, mla_q_norm, mla_w_uq, mla_kv_norm,
                mla_w_ukv, lru_conv_w, lru_conv_b, lru_w_a, lru_b_a, lru_w_x, lru_b_x, lru_lambda, sink_logits,
                w_branch, w_out, ffn2_norm, ffn2_w1, ffn2_w3, ffn2_w2):
    row = lambda v: v.reshape(1, -1)
    wi = w_in[l]
    c0, c1, c2, c3 = A_COLS, A_COLS + B_COLS, A_COLS + B_COLS + C_COLS, A_COLS + B_COLS + C_COLS + D_COLS
    w_a = wi[:, :c0].reshape(D_MODEL, 3, N_DIL, A_HEADS * HEAD_DIM).transpose(0, 2, 1, 3).reshape(D_MODEL, A_COLS)
    wb = wi[:, c0:c1]
    zeros = lambda n: jnp.zeros((D_MODEL, n), wi.dtype)
    w_b = jnp.concatenate([wb[:, :Q_LORA + KV_LORA], zeros(QK_NOPE), wb[:, Q_LORA + KV_LORA:],
                           zeros(SLOT - QK_NOPE - QK_ROPE)], axis=1)
    w_cbd = jnp.concatenate([wi[:, c1:c2], w_b, wi[:, c2:c3]], axis=1)
    wq = mla_w_uq[l].reshape(Q_LORA, B_HEADS, QK_NOPE + QK_ROPE)
    wq = jnp.pad(wq, ((0, 0), (0, 0), (0, SLOT - QK_NOPE - QK_ROPE))).reshape(Q_LORA, B_HEADS * SLOT)
    wkv = mla_w_ukv[l].reshape(KV_LORA, B_HEADS, QK_NOPE + V_DIM)
    wk = jnp.pad(wkv[:, :, :QK_NOPE], ((0, 0), (0, 0), (0, SLOT - QK_NOPE))).reshape(KV_LORA, B_HEADS * SLOT)
    wv = wkv[:, :, QK_NOPE:].reshape(KV_LORA, B_HEADS * V_DIM)
    wg = jnp.stack([jnp.concatenate([_block_diag(lru_w_a[l, d]), _block_diag(lru_w_x[l, d])], axis=1)
                    for d in range(2)])
    bg = jnp.concatenate([lru_b_a[l], lru_b_x[l]], axis=1)
    return dict(
        ffn1=(row(ffn1_norm[l]), ffn1_w1[l].astype(BF16), ffn1_w3[l].astype(BF16), ffn1_w2[l].astype(BF16)),
        ffn2=(row(ffn2_norm[l]), ffn2_w1[l].astype(BF16), ffn2_w3[l].astype(BF16), ffn2_w2[l].astype(BF16)),
        mix_norm=row(mix_norm[l]),
        w_a=w_a.astype(BF16), w_cbd=w_cbd.astype(BF16), w_g=wi[:, c3:].astype(BF16),
        qn=row(mla_q_norm[l]), kvn=row(mla_kv_norm[l]),
        wq=wq.astype(BF16), wk=wk.astype(BF16), wv=wv.astype(BF16),
        conv_w=lru_conv_w[l], conv_b=row(lru_conv_b[l]), wg=wg.astype(BF16), bg=bg, lam=lru_lambda[l],
        sink=sink_logits[l], w_branch=w_branch[l].astype(BF16), w_out=w_out[l].astype(BF16),
    )


def _rope_slot_tables(seq):
    inv = ROPE_THETA ** (-jnp.arange(0, QK_ROPE, 2, dtype=F32) / QK_ROPE)
    ang = jnp.arange(seq, dtype=F32)[:, None] * inv[None, :]
    cos, sin = jnp.cos(ang), jnp.sin(ang)
    scale = (QK_NOPE + QK_ROPE) ** -0.5 * np.log2(np.e)
    z = lambda n: jnp.zeros((seq, n), F32)
    tail = SLOT - QK_NOPE - QK_ROPE
    cos_q = jnp.concatenate([jnp.full((seq, QK_NOPE), scale, F32), cos * scale, cos * scale, z(tail)], axis=1)
    sin_q = jnp.concatenate([z(QK_NOPE), -sin * scale, sin * scale, z(tail)], axis=1)
    cos_k = jnp.concatenate([z(QK_NOPE), cos, cos, z(tail)], axis=1)
    sin_k = jnp.concatenate([z(QK_NOPE), -sin, sin, z(tail)], axis=1)
    return cos_q, sin_q, cos_k, sin_k


def _layer(x, w, batch, seq, final_g):
    t = batch * seq
    x = _ffn(x, *w['ffn1'], final_g, False)
    *pas, h = _proj_a(x, w['mix_norm'], w['w_a'], batch, seq)
    pc, pb, pd = _proj_bcd(h, w['w_cbd'], ((C_COLS, BF16), (B_PAD_COLS, F32), (D_COLS, F32)))

    oa, la = [], []
    for pa, (window, dil) in zip(pas, DIL_CONFIGS):
        o, lse = _band_attention(
            pa, q_col=0, k_col=1, v_col=2, kv_width=A_HEADS * HEAD_DIM,
            n_heads=A_HEADS, n_kv=A_HEADS, radius=window // (2 * dil), step=dil, sink=None,
            out_dtype=F32, emit_lse=True)
        oa.append(o)
        la.append(lse)

    q, k, vt = _mla_prep(pb, w['qn'], w['kvn'], w['wq'], w['wk'], w['wv'], _rope_slot_tables(seq), batch, seq)
    yb = _mla_attn(q.reshape(batch, seq, -1), k.reshape(batch, seq, -1), vt, batch, seq)

    kvw = C_KV_HEADS * HEAD_DIM
    yc = _band_attention(
        pc.reshape(batch, 1, seq, C_COLS),
        q_col=0, k_col=C_HEADS * HEAD_DIM // kvw, v_col=C_HEADS * HEAD_DIM // kvw + 1, kv_width=kvw,
        n_heads=C_HEADS, n_kv=C_KV_HEADS, radius=C_RADIUS, step=1, sink=w['sink'],
        out_dtype=BF16, emit_lse=False)[0]

    hf, hr = _lru(pd.reshape(batch, seq, D_COLS), w['conv_w'], w['conv_b'], w['wg'], w['bg'], w['lam'])

    ya, yd = _branch_out(oa, la, hf.reshape(t, LRU_WIDTH), hr.reshape(t, LRU_WIDTH), pd, seq)
    x = _gate_merge_out(x, h, (ya, yb.reshape(t, BRANCH_W), yc.reshape(t, BRANCH_W), yd),
                        w['w_g'], w['w_branch'], w['w_out'])
    return _ffn(x, *w['ffn2'], final_g, final_g is not None and w.get('last', False))


def _trunk(x, layers, final_norm):
    batch, seq, _ = x.shape
    h = x.reshape(batch * seq, D_MODEL)
    fg = final_norm.reshape(1, -1)
    for l, w in enumerate(layers):
        h = _layer(h, dict(w, last=(l == len(layers) - 1)), batch, seq, fg)
    return h.reshape(batch, seq, D_MODEL)


def kernel(x_prompt, x_sample, ffn1_norm, ffn1_w1, ffn1_w3, ffn1_w2, mix_norm, w_in, mla_q_norm, mla_w_uq, mla_kv_norm, mla_w_ukv, lru_conv_w, lru_conv_b, lru_w_a, lru_b_a, lru_w_x, lru_b_x, lru_lambda, sink_logits, w_branch, w_out, ffn2_norm, ffn2_w1, ffn2_w3, ffn2_w2, final_norm):
    layers = [_prep_layer(l, ffn1_norm, ffn1_w1, ffn1_w3, ffn1_w2, mix_norm, w_in, mla_q_norm, mla_w_uq,
                          mla_kv_norm, mla_w_ukv, lru_conv_w, lru_conv_b, lru_w_a, lru_b_a, lru_w_x, lru_b_x,
                          lru_lambda, sink_logits, w_branch, w_out, ffn2_norm, ffn2_w1, ffn2_w3, ffn2_w2)
              for l in range(DEPTH)]
    return (_trunk(x_prompt, layers, final_norm), _trunk(x_sample, layers, final_norm))
```

```python
import functools

import numpy as np
import jax
import jax.numpy as jnp
from jax import lax
from jax.experimental import pallas as pl
from jax.experimental.pallas import tpu as pltpu

F32 = jnp.float32
BF16 = jnp.bfloat16

D_MODEL = 2048
DEPTH = 2
HEAD_DIM = 64
N_BRANCH = 4
BRANCH_W = 512
DIL_CONFIGS = ((128, 1), (512, 4), (2048, 16))
N_DIL = 3
A_HEADS = 8
B_HEADS = 8
Q_LORA = 384
KV_LORA = 128
QK_NOPE = 64
QK_ROPE = 32
V_DIM = 64
ROPE_THETA = 10000.0
C_HEADS = 8
C_KV_HEADS = 2
C_RADIUS = 128
LRU_WIDTH = 512
LRU_BLOCKS = 8
LRU_BLOCK = 64
CONV_W = 4
CONV_PAD_L = 2
LRU_C = 8.0
D_FF = 5632
EPS = 1e-6
NEG = -1e30

A_COLS = 3 * N_DIL * A_HEADS * HEAD_DIM
B_COLS = Q_LORA + KV_LORA + QK_ROPE
C_COLS = (C_HEADS + 2 * C_KV_HEADS) * HEAD_DIM
D_COLS = 2 * LRU_WIDTH
G_COLS = N_BRANCH * D_MODEL

LANES = 128
SUBLANES = 8
SLOT = LANES
B_PAD_COLS = Q_LORA + KV_LORA + SLOT
ROW_CHUNK = 128
VMEM_LIMIT = 52 * 1024 * 1024
VMEM_LIMIT_FFN = 54 * 1024 * 1024


def _cparams(*sem, vmem=VMEM_LIMIT):
    return pltpu.CompilerParams(dimension_semantics=sem, vmem_limit_bytes=vmem)


def _rms(x, g):
    return x * lax.rsqrt(jnp.mean(x * x, axis=-1, keepdims=True) + EPS) * g


def _pick(n, pref):
    t = min(n, pref)
    while n % t:
        t //= 2
    return t


def _ffn_kernel(x_ref, g_ref, w1_ref, w3_ref, w2_ref, fg_ref, o_ref, h_ref, *, final_norm):
    j = pl.program_id(1)
    last = pl.num_programs(1) - 1
    tm = x_ref.shape[0]
    halves = (slice(0, tm // 2), slice(tm // 2, tm))

    def down(h):
        a = jnp.dot(h, w1_ref[...], preferred_element_type=F32)
        b = jnp.dot(h, w3_ref[...], preferred_element_type=F32)
        act = (a * jax.nn.sigmoid(a) * b).astype(BF16)
        return jnp.dot(act, w2_ref[...], preferred_element_type=F32)

    @pl.when(j == 0)
    def _():
        for rows in halves:
            h = _rms(x_ref[rows, :], g_ref[...]).astype(BF16)
            h_ref[rows, :] = h
            o_ref[rows, :] = down(h)

    @pl.when((j > 0) & (j < last))
    def _():
        for rows in halves:
            o_ref[rows, :] += down(h_ref[rows, :])

    @pl.when(j == last)
    def _():
        for rows in halves:
            y = x_ref[rows, :] + 0.5 * (o_ref[rows, :] + down(h_ref[rows, :]))
            if final_norm:
                y = _rms(y, fg_ref[...])
            o_ref[rows, :] = y


def _ffn(x, g, w1, w3, w2, fg, final_norm):
    t = x.shape[0]
    tm, tf = _pick(t, 512), 512
    return pl.pallas_call(
        functools.partial(_ffn_kernel, final_norm=final_norm),
        grid=(t // tm, D_FF // tf),
        in_specs=[
            pl.BlockSpec((tm, D_MODEL), lambda i, j: (i, 0)),
            pl.BlockSpec((1, D_MODEL), lambda i, j: (0, 0)),
            pl.BlockSpec((D_MODEL, tf), lambda i, j: (0, j)),
            pl.BlockSpec((D_MODEL, tf), lambda i, j: (0, j)),
            pl.BlockSpec((tf, D_MODEL), lambda i, j: (j, 0)),
            pl.BlockSpec((1, D_MODEL), lambda i, j: (0, 0)),
        ],
        out_specs=pl.BlockSpec((tm, D_MODEL), lambda i, j: (i, 0)),
        out_shape=jax.ShapeDtypeStruct((t, D_MODEL), F32),
        scratch_shapes=[pltpu.VMEM((tm, D_MODEL), BF16)],
        compiler_params=_cparams("parallel", "arbitrary"),
        name="ffn",
    )(x, g, w1, w3, w2, fg)


def _proj_bcd_kernel(h_ref, w_ref, *o_refs):
    r = jnp.dot(h_ref[...], w_ref[...], preferred_element_type=F32)
    c = 0
    for o_ref in o_refs:
        wd = o_ref.shape[1]
        o_ref[...] = r[:, c:c + wd].astype(o_ref.dtype)
        c += wd


def _proj_bcd(h, w, outs):
    t, n = h.shape[0], w.shape[1]
    tm = _pick(t, 512)
    return pl.pallas_call(
        _proj_bcd_kernel,
        grid=(t // tm,),
        in_specs=[pl.BlockSpec((tm, D_MODEL), lambda i: (i, 0)),
                  pl.BlockSpec((D_MODEL, n), lambda i: (0, 0))],
        out_specs=[pl.BlockSpec((tm, wd), lambda i: (i, 0)) for wd, _ in outs],
        out_shape=[jax.ShapeDtypeStruct((t, wd), dt) for wd, dt in outs],
        compiler_params=_cparams("parallel"),
        name="proj_bcd",
    )(h, w)


def _proj_a_kernel(x_ref, g_ref, w_ref, o1_ref, o2_ref, o3_ref, hout_ref, hn_ref, h_ref, *, tm):
    j = pl.program_id(1)
    chunks = D_MODEL // LANES

    @pl.when(j == 0)
    def _():
        hn = _rms(x_ref[...], g_ref[...])
        for c in range(chunks):
            hn_ref[c] = hn[:, c * LANES:(c + 1) * LANES]
        h_ref[...] = hn.astype(BF16)
        hout_ref[...] = hn.astype(BF16)

    for g, o_ref in enumerate((o1_ref, o2_ref, o3_ref)):
        dil = DIL_CONFIGS[g][1]
        n = tm // dil

        @pl.when(j == g)
        def _(o_ref=o_ref, dil=dil, n=n):
            if dil > 1:
                for r in range(dil):
                    for c in range(chunks):
                        h_ref[r * n:(r + 1) * n, c * LANES:(c + 1) * LANES] = (
                            hn_ref[c, pl.ds(r, n, stride=dil), :].astype(BF16))
            res = jnp.dot(h_ref[...], w_ref[...], preferred_element_type=F32)
            for r in range(dil):
                o_ref[0, r] = res[r * n:(r + 1) * n].astype(BF16)


def _proj_a(x, g, w, batch, seq):
    t = x.shape[0]
    tm = _pick(seq, 512)
    nseq = seq // tm
    gw = A_COLS // N_DIL
    out_specs = [pl.BlockSpec((1, dil, tm // dil, gw), lambda i, j: (i // nseq, 0, i % nseq, 0))
                 for _, dil in DIL_CONFIGS]
    out_shape = [jax.ShapeDtypeStruct((batch, dil, seq // dil, gw), BF16) for _, dil in DIL_CONFIGS]
    out_specs.append(pl.BlockSpec((tm, D_MODEL), lambda i, j: (i, 0)))
    out_shape.append(jax.ShapeDtypeStruct((t, D_MODEL), BF16))
    return pl.pallas_call(
        functools.partial(_proj_a_kernel, tm=tm),
        grid=(t // tm, N_DIL),
        in_specs=[pl.BlockSpec((tm, D_MODEL), lambda i, j: (i, 0)),
                  pl.BlockSpec((1, D_MODEL), lambda i, j: (0, 0)),
                  pl.BlockSpec((D_MODEL, gw), lambda i, j: (0, j))],
        out_specs=out_specs,
        out_shape=out_shape,
        scratch_shapes=[pltpu.VMEM((D_MODEL // LANES, tm, LANES), F32), pltpu.VMEM((tm, D_MODEL), BF16)],
        compiler_params=_cparams("parallel", "arbitrary"),
        name="proj_a",
    )(x, g, w)


def _band_kernel(*refs, n_heads, n_kv, radius, sb, tq, has_sink, emit_lse):
    if has_sink:
        sink_ref, refs = refs[0], refs[1:]
    bias_ref, q_ref, kp_ref, kc_ref, kn_ref, vp_ref, vc_ref, vn_ref = refs[:8]
    o_ref = refs[8]
    kx_ref, vx_ref, vt_ref, qt_ref = refs[-4:]
    for x_ref, (p_ref, c_ref, n_ref) in ((kx_ref, (kp_ref, kc_ref, kn_ref)), (vx_ref, (vp_ref, vc_ref, vn_ref))):
        x_ref[0:radius, :] = p_ref[0, 0, tq - radius:, :]
        x_ref[radius:radius + tq, :] = c_ref[0, 0]
        x_ref[radius + tq:, :] = n_ref[0, 0, :radius, :]
    vt_ref[...] = vx_ref[...].astype(F32).T.astype(BF16)
    qt_ref[...] = (q_ref[0, 0].astype(F32) * (HEAD_DIM ** -0.5)).T.astype(BF16)
    w = sb + 2 * radius
    rep = n_heads // n_kv
    heads = range(n_heads)
    zeros = jnp.zeros((HEAD_DIM, sb), BF16)
    nsub = tq // sb
    tile, last_tile = pl.program_id(2), pl.num_programs(2) - 1
    for u in range(nsub):
        rows = slice(u * sb, (u + 1) * sb)
        win = slice(u * sb, u * sb + w)
        kind = 0
        if u == 0:
            kind = kind + (tile == 0).astype(jnp.int32)
        if u == nsub - 1:
            kind = kind + 2 * (tile == last_tile).astype(jnp.int32)
        scores = []
        for h in heads:
            g = h // rep
            qh = qt_ref[h * HEAD_DIM:(h + 1) * HEAD_DIM, rows]
            rhs = jnp.concatenate([qh, zeros] if g % 2 == 0 else [zeros, qh], axis=0)
            pair = slice((g // 2) * 2 * HEAD_DIM, (g // 2 + 1) * 2 * HEAD_DIM)
            scores.append(jnp.dot(kx_ref[win, pair], rhs, preferred_element_type=F32) + bias_ref[kind, h])
        probs, stats = [], []
        for h in heads:
            m = jnp.max(scores[h], axis=0, keepdims=True)
            if has_sink:
                m = jnp.maximum(m, sink_ref[h])
            p = jnp.exp(scores[h] - m)
            l = jnp.sum(p, axis=0, keepdims=True)
            if has_sink:
                l = l + jnp.exp(sink_ref[h] - m)
            probs.append(p.astype(BF16))
            stats.append((m, l))
        outs, lses = [], []
        for h in heads:
            g = h // rep
            m, l = stats[h]
            outs.append(jnp.dot(vt_ref[g * HEAD_DIM:(g + 1) * HEAD_DIM, win], probs[h],
                                preferred_element_type=F32) / l)
            lses.append(jnp.broadcast_to(m + jnp.log(l), (HEAD_DIM, sb)))
        o_ref[0, 0, rows, :] = jnp.concatenate(outs, axis=0).T.astype(o_ref.dtype)
        if emit_lse:
            refs[9][0, 0, rows, :] = jnp.concatenate(lses, axis=0).T


def _alibi(n):
    return np.asarray(2.0 ** (-8.0 * np.arange(1, n + 1) / n), dtype=np.float32)


def _band_bias(n_heads, radius, step, sb):
    w = sb + 2 * radius
    shape = (4, n_heads, w, sb)
    kind, h, kj, qi = (lax.broadcasted_iota(jnp.int32, shape, d) for d in range(4))
    rel = jnp.abs(kj - radius - qi)
    kpos = kj - radius
    ok = (rel <= radius) & ((kpos >= 0) | (kind % 2 == 0)) & ((kpos < sb) | (kind < 2))
    slopes = jnp.asarray(_alibi(n_heads))[h]
    return jnp.where(ok, -slopes * (step * rel).astype(F32), NEG)


def _band_attention(arr, *, q_col, k_col, v_col, kv_width, n_heads, n_kv, radius, step, sink, out_dtype,
                    emit_lse):
    batch, n_classes, length, _ = arr.shape
    qw = n_heads * HEAD_DIM
    tq = _pick(length, 512)
    sb = min(tq, 128)
    nblk = length // tq

    def kv_spec(col, shift):
        return pl.BlockSpec((1, 1, tq, kv_width), lambda b, r, i: (b, r, jnp.clip(i + shift, 0, nblk - 1), col))

    bias = _band_bias(n_heads, radius, step, sb)
    in_specs = [pl.BlockSpec(bias.shape, lambda b, r, i: (0, 0, 0, 0)),
                pl.BlockSpec((1, 1, tq, qw), lambda b, r, i: (b, r, i, q_col)),
                kv_spec(k_col, -1), kv_spec(k_col, 0), kv_spec(k_col, 1),
                kv_spec(v_col, -1), kv_spec(v_col, 0), kv_spec(v_col, 1)]
    args = [bias] + [arr] * 7
    if sink is not None:
        in_specs = [pl.BlockSpec(memory_space=pltpu.SMEM)] + in_specs
        args = [sink] + args
    o_spec = pl.BlockSpec((1, 1, tq, qw), lambda b, r, i: (b, r, i, 0))
    o_shape = jax.ShapeDtypeStruct((batch, n_classes, length, qw), out_dtype)
    out_specs, out_shape = [o_spec], [o_shape]
    if emit_lse:
        out_specs, out_shape = [o_spec, o_spec], [o_shape, jax.ShapeDtypeStruct(o_shape.shape, F32)]
    return pl.pallas_call(
        functools.partial(_band_kernel, n_heads=n_heads, n_kv=n_kv, radius=radius, sb=sb, tq=tq,
                          has_sink=sink is not None, emit_lse=emit_lse),
        grid=(batch, n_classes, nblk),
        in_specs=in_specs,
        out_specs=out_specs,
        out_shape=out_shape,
        scratch_shapes=[pltpu.VMEM((tq + 2 * radius, kv_width), BF16), pltpu.VMEM((tq + 2 * radius, kv_width), BF16),
                        pltpu.VMEM((kv_width, tq + 2 * radius), BF16), pltpu.VMEM((qw, tq), BF16)],
        compiler_params=_cparams("parallel", "parallel", "arbitrary"),
        name="band_attn",
    )(*args)


def _rope_slot(x, cos_t, sin_t):
    lane = lax.broadcasted_iota(jnp.int32, x.shape, 1)
    first = (lane >= QK_NOPE) & (lane < QK_NOPE + QK_ROPE // 2)
    partner = jnp.where(first, pltpu.roll(x, SLOT - QK_ROPE // 2, 1), pltpu.roll(x, QK_ROPE // 2, 1))
    return x * cos_t + partner * sin_t


def _mla_prep_kernel(pb_ref, qn_ref, kvn_ref, wq_ref, wk_ref, wv_ref, cq_ref, sq_ref, ck_ref, sk_ref,
                     q_out, k_out, vt_out):
    pb = pb_ref[...]
    cq = _rms(pb[:, :Q_LORA], qn_ref[...]).astype(BF16)
    ckv = _rms(pb[:, Q_LORA:Q_LORA + KV_LORA], kvn_ref[...]).astype(BF16)
    kr = pb[:, Q_LORA + KV_LORA:]
    q = jnp.dot(cq, wq_ref[...], preferred_element_type=F32)
    k = jnp.dot(ckv, wk_ref[...], preferred_element_type=F32)
    vt_out[0] = jnp.dot(ckv, wv_ref[...], preferred_element_type=F32).T.astype(BF16)
    kr = _rope_slot(kr, ck_ref[...], sk_ref[...])
    for h in range(B_HEADS):
        sl = slice(h * SLOT, (h + 1) * SLOT)
        q_out[:, sl] = _rope_slot(q[:, sl], cq_ref[...], sq_ref[...]).astype(BF16)
        k_out[:, sl] = (k[:, sl] + kr).astype(BF16)


def _mla_prep(pb, qn, kvn, wq, wk, wv, tabs, batch, seq):
    t = pb.shape[0]
    tm = _pick(seq, 512)
    nseq = seq // tm
    row = lambda i: (i, 0)
    fixed = lambda i: (0, 0)
    tab = pl.BlockSpec((tm, SLOT), lambda i: (i % nseq, 0))
    return pl.pallas_call(
        _mla_prep_kernel,
        grid=(t // tm,),
        in_specs=[pl.BlockSpec((tm, B_PAD_COLS), row),
                  pl.BlockSpec((1, Q_LORA), fixed), pl.BlockSpec((1, KV_LORA), fixed),
                  pl.BlockSpec((Q_LORA, B_HEADS * SLOT), fixed),
                  pl.BlockSpec((KV_LORA, B_HEADS * SLOT), fixed),
                  pl.BlockSpec((KV_LORA, B_HEADS * V_DIM), fixed),
                  tab, tab, tab, tab],
        out_specs=[pl.BlockSpec((tm, B_HEADS * SLOT), row), pl.BlockSpec((tm, B_HEADS * SLOT), row),
                   pl.BlockSpec((1, B_HEADS * V_DIM, tm), lambda i: (i // nseq, 0, i % nseq))],
        out_shape=[jax.ShapeDtypeStruct((t, B_HEADS * SLOT), BF16), jax.ShapeDtypeStruct((t, B_HEADS * SLOT), BF16),
                   jax.ShapeDtypeStruct((batch, B_HEADS * V_DIM, seq), BF16)],
        compiler_params=_cparams("parallel"),
        name="mla_prep",
    )(pb, qn, kvn, wq, wk, wv, *tabs)


def _mla_attn_kernel(q_ref, k_ref, vt_ref, o_ref, st_ref, *, seq, kc):
    chunks = [slice(c * kc, (c + 1) * kc) for c in range(seq // kc)]

    def qk(hh, rows):
        sl = slice(hh * SLOT, (hh + 1) * SLOT)
        st = lax.dot_general(k_ref[0, rows, sl], q_ref[0, :, sl], (((1,), (1,)), ((), ())),
                             preferred_element_type=F32)
        st_ref[hh, rows, :] = st
        return jnp.max(st, axis=0, keepdims=True)

    def pv(hh, rows, m):
        p = jnp.exp2(st_ref[hh, rows, :] - m)
        return (jnp.sum(p, axis=0, keepdims=True),
                jnp.dot(vt_ref[0, :, rows], p.astype(BF16), preferred_element_type=F32))

    def add(acc, new):
        return new if acc is None else (acc[0] + new[0], acc[1] + new[1])

    m0 = functools.reduce(jnp.maximum, [qk(0, rows) for rows in chunks])
    m1 = acc0 = acc1 = None
    for rows in chunks:
        acc0 = add(acc0, pv(0, rows, m0))
        mc = qk(1, rows)
        m1 = mc if m1 is None else jnp.maximum(m1, mc)
    for rows in chunks:
        acc1 = add(acc1, pv(1, rows, m1))
    outs = [acc[1][hh * V_DIM:(hh + 1) * V_DIM] / acc[0] for hh, acc in enumerate((acc0, acc1))]
    o_ref[0] = jnp.concatenate(outs, axis=0).T.astype(o_ref.dtype)


def _mla_attn(q, k, vt, batch, seq):
    tq = _pick(seq, 512)
    kc = _pick(seq, 256)
    pairs = B_HEADS // 2
    return pl.pallas_call(
        functools.partial(_mla_attn_kernel, seq=seq, kc=kc),
        scratch_shapes=[pltpu.VMEM((2, seq, tq), F32)],
        grid=(batch, pairs, seq // tq),
        in_specs=[pl.BlockSpec((1, tq, 2 * SLOT), lambda b, hp, i: (b, i, hp)),
                  pl.BlockSpec((1, seq, 2 * SLOT), lambda b, hp, i: (b, 0, hp)),
                  pl.BlockSpec((1, 2 * V_DIM, seq), lambda b, hp, i: (b, hp, 0))],
        out_specs=pl.BlockSpec((1, tq, 2 * V_DIM), lambda b, hp, i: (b, i, hp)),
        out_shape=jax.ShapeDtypeStruct((batch, seq, B_HEADS * V_DIM), BF16),
        compiler_params=_cparams("parallel", "parallel", "arbitrary"),
        name="mla_attn",
    )(q, k, vt)


def _sigmoid_tanh(x):
    return 0.5 * (jnp.tanh(0.5 * x) + 1.0)


def _softplus(x):
    return jnp.maximum(x, 0.0) + jnp.log1p(jnp.exp(-jnp.abs(x)))


def _lru_kernel(xf_ref, xfp_ref, xfn_ref, xr_ref, xrp_ref, xrn_ref, cw_ref, cb_ref, wg_ref, bg_ref, lam_ref,
                hf_ref, hr_ref, ext_ref, af_ref, uf_ref, ar_ref, ur_ref, cf_ref, cr_ref, *, ts):
    i = pl.program_id(1)
    nt = pl.num_programs(1)
    halo = SUBLANES

    @pl.when(i == 0)
    def _():
        cf_ref[...] = jnp.zeros_like(cf_ref)
        cr_ref[...] = jnp.zeros_like(cr_ref)

    def gates(d, tile, x_ref, xp_ref, xn_ref, a_ref, u_ref):
        ext_ref[0:halo, :] = jnp.where(tile > 0, xp_ref[0], 0.0)
        ext_ref[halo:halo + ts, :] = x_ref[0]
        ext_ref[halo + ts:, :] = jnp.where(tile < nt - 1, xn_ref[0], 0.0)
        xc = cb_ref[...]
        for tap in range(CONV_W):
            off = halo - CONV_PAD_L + tap
            xc = xc + ext_ref[off:off + ts, :] * cw_ref[tap:tap + 1, :]
        g = jnp.dot(xc.astype(BF16), wg_ref[d], preferred_element_type=F32) + bg_ref[d:d + 1, :]
        r = _sigmoid_tanh(g[:, :LRU_WIDTH])
        ig = _sigmoid_tanh(g[:, LRU_WIDTH:])
        log_a = -LRU_C * r * _softplus(-lam_ref[d:d + 1, :])
        a = jnp.exp(log_a)
        a_ref[...] = a
        u_ref[...] = jnp.sqrt(-jnp.tanh(log_a) * (1.0 + a * a)) * (ig * xc)

    gates(0, i, xf_ref, xfp_ref, xfn_ref, af_ref, uf_ref)
    gates(1, nt - 1 - i, xr_ref, xrp_ref, xrn_ref, ar_ref, ur_ref)

    ng = ts // SUBLANES
    row = lax.broadcasted_iota(jnp.int32, (SUBLANES, LRU_WIDTH), 0)

    def group_scan(a, u, carry, reverse):
        for d in (1, 2, 4):
            if reverse:
                keep, sh = row < SUBLANES - d, SUBLANES - d
            else:
                keep, sh = row >= d, d
            ap = jnp.where(keep, pltpu.roll(a, sh, 0), 1.0)
            up = jnp.where(keep, pltpu.roll(u, sh, 0), 0.0)
            u = a * up + u
            a = a * ap
        return u + a * carry

    def body(g, carry):
        cf, cr = carry
        r0 = pl.multiple_of(g * SUBLANES, SUBLANES)
        hf = group_scan(af_ref[pl.ds(r0, SUBLANES), :], uf_ref[pl.ds(r0, SUBLANES), :], cf, False)
        hf_ref[0, pl.ds(r0, SUBLANES), :] = hf
        r1 = pl.multiple_of((ng - 1 - g) * SUBLANES, SUBLANES)
        hr = group_scan(ar_ref[pl.ds(r1, SUBLANES), :], ur_ref[pl.ds(r1, SUBLANES), :], cr, True)
        hr_ref[0, pl.ds(r1, SUBLANES), :] = hr
        return (jnp.broadcast_to(hf[SUBLANES - 1:, :], hf.shape), jnp.broadcast_to(hr[:1, :], hr.shape))

    cf, cr = lax.fori_loop(0, ng, body, (cf_ref[...], cr_ref[...]))
    cf_ref[...] = cf
    cr_ref[...] = cr


def _lru(pd, cw, cb, wg, bg, lam):
    batch, seq, _ = pd.shape
    ts = _pick(seq, 512)
    nt = seq // ts
    per = ts // SUBLANES
    last8 = seq // SUBLANES - 1

    def tile_specs(tile):
        return [pl.BlockSpec((1, ts, LRU_WIDTH), lambda b, i: (b, tile(i), 0)),
                pl.BlockSpec((1, SUBLANES, LRU_WIDTH), lambda b, i: (b, jnp.maximum(tile(i) * per - 1, 0), 0)),
                pl.BlockSpec((1, SUBLANES, LRU_WIDTH), lambda b, i: (b, jnp.minimum((tile(i) + 1) * per, last8), 0))]

    fwd_tile = lambda i: i
    rev_tile = lambda i: nt - 1 - i
    fixed2 = lambda b, i: (0, 0)
    shp = jax.ShapeDtypeStruct((batch, seq, LRU_WIDTH), F32)
    tile_f32 = pltpu.VMEM((ts, LRU_WIDTH), F32)
    return pl.pallas_call(
        functools.partial(_lru_kernel, ts=ts),
        grid=(batch, nt),
        in_specs=tile_specs(fwd_tile) + tile_specs(rev_tile)
                 + [pl.BlockSpec((CONV_W, LRU_WIDTH), fixed2), pl.BlockSpec((1, LRU_WIDTH), fixed2),
                    pl.BlockSpec((2, LRU_WIDTH, 2 * LRU_WIDTH), lambda b, i: (0, 0, 0)),
                    pl.BlockSpec((2, 2 * LRU_WIDTH), fixed2), pl.BlockSpec((2, LRU_WIDTH), fixed2)],
        out_specs=[pl.BlockSpec((1, ts, LRU_WIDTH), lambda b, i: (b, i, 0)),
                   pl.BlockSpec((1, ts, LRU_WIDTH), lambda b, i: (b, nt - 1 - i, 0))],
        out_shape=[shp, shp],
        scratch_shapes=[pltpu.VMEM((ts + 2 * SUBLANES, LRU_WIDTH), F32), tile_f32, tile_f32, tile_f32, tile_f32,
                        pltpu.VMEM((SUBLANES, LRU_WIDTH), F32), pltpu.VMEM((SUBLANES, LRU_WIDTH), F32)],
        compiler_params=_cparams("parallel", "arbitrary"),
        name="lru",
    )(pd, pd, pd, pd, pd, pd, cw, cb, wg, bg, lam)


def _gelu_tanh(x):
    return 0.5 * x * (1.0 + jnp.tanh(np.sqrt(2.0 / np.pi).astype(np.float32) * (x + 0.044715 * (x * x * x))))


def _branch_out_kernel(o0, o1, o2, l0, l1, l2, hf_ref, hr_ref, gd_ref, ya_ref, yd_ref, *nat_refs, tm):
    def natural(ref, g, buf):
        dil = DIL_CONFIGS[g][1]
        if dil == 1:
            return ref[0, 0]
        chunks = BRANCH_W // LANES
        for r in range(dil):
            for c in range(chunks):
                buf[c, pl.ds(r, tm // dil, stride=dil), :] = ref[0, r, :, c * LANES:(c + 1) * LANES]
        return jnp.concatenate([buf[c] for c in range(chunks)], axis=1)

    outs = (o0[0, 0], natural(o1, 1, nat_refs[0]), natural(o2, 2, nat_refs[1]))
    lse = (l0[0, 0], natural(l1, 1, nat_refs[2]), natural(l2, 2, nat_refs[3]))
    m = jnp.maximum(jnp.maximum(lse[0], lse[1]), lse[2])
    e = [jnp.exp(v - m) for v in lse]
    tot = e[0] + e[1] + e[2]
    ya = (e[0] / tot) * outs[0] + (e[1] / tot) * outs[1] + (e[2] / tot) * outs[2]
    ya_ref[...] = ya.astype(BF16)
    yd_ref[...] = ((hf_ref[...] + hr_ref[...]) * _gelu_tanh(gd_ref[...])).astype(BF16)


def _branch_out(oa, la, hf, hr, pd, seq):
    t = hf.shape[0]
    tm = _pick(seq, 256)
    nseq = seq // tm
    blk = pl.BlockSpec((tm, BRANCH_W), lambda i: (i, 0))
    cls = [pl.BlockSpec((1, dil, tm // dil, BRANCH_W), lambda i: (i // nseq, 0, i % nseq, 0))
           for _, dil in DIL_CONFIGS]
    shp = jax.ShapeDtypeStruct((t, BRANCH_W), BF16)
    return pl.pallas_call(
        functools.partial(_branch_out_kernel, tm=tm),
        grid=(t // tm,),
        in_specs=cls + cls + [blk, blk, pl.BlockSpec((tm, BRANCH_W), lambda i: (i, 1))],
        out_specs=[blk, blk],
        out_shape=[shp, shp],
        scratch_shapes=[pltpu.VMEM((BRANCH_W // LANES, tm, LANES), F32)] * 4,
        compiler_params=_cparams("parallel"),
        name="branch_out",
    )(*oa, *la, hf, hr, pd)


def _gate_merge_out_kernel(x_ref, h_ref, ya, yb, yc, yd, g0, g1, g2, g3, wb_ref, wo_ref, o_ref):
    j = pl.program_id(1)

    @pl.when(j == 0)
    def _():
        o_ref[...] = jnp.zeros_like(o_ref)

    h = h_ref[...]
    merged = None
    for nbr, (y_ref, wg_ref) in enumerate(zip((ya, yb, yc, yd), (g0, g1, g2, g3))):
        gate = jax.nn.sigmoid(jnp.dot(h, wg_ref[...], preferred_element_type=F32))
        term = gate * jnp.dot(y_ref[...], wb_ref[nbr], preferred_element_type=F32)
        merged = term if merged is None else merged + term
    o_ref[...] += jnp.dot(merged.astype(BF16), wo_ref[...], preferred_element_type=F32)

    @pl.when(j == pl.num_programs(1) - 1)
    def _():
        o_ref[...] = x_ref[...] + o_ref[...]


def _gate_merge_out(x, h, ys, wg, wb, wo):
    t = x.shape[0]
    tm, tn = _pick(t, 512), 512
    nj = D_MODEL // tn
    row = lambda i, j: (i, 0)
    gate_specs = [pl.BlockSpec((D_MODEL, tn), functools.partial(lambda i, j, nbr: (0, nbr * nj + j), nbr=nbr))
                  for nbr in range(N_BRANCH)]
    return pl.pallas_call(
        _gate_merge_out_kernel,
        grid=(t // tm, nj),
        in_specs=[pl.BlockSpec((tm, D_MODEL), row), pl.BlockSpec((tm, D_MODEL), row)]
                 + [pl.BlockSpec((tm, BRANCH_W), row)] * N_BRANCH + gate_specs
                 + [pl.BlockSpec((N_BRANCH, BRANCH_W, tn), lambda i, j: (0, 0, j)),
                    pl.BlockSpec((tn, D_MODEL), lambda i, j: (j, 0))],
        out_specs=pl.BlockSpec((tm, D_MODEL), row),
        out_shape=jax.ShapeDtypeStruct((t, D_MODEL), F32),
        compiler_params=_cparams("parallel", "arbitrary"),
        name="gate_merge_out",
    )(x, h, *ys, wg, wg, wg, wg, wb, wo)


def _block_diag(w):
    eye = jnp.eye(LRU_BLOCKS, dtype=w.dtype)
    return jnp.einsum('nef,nm->nemf', w, eye).reshape(LRU_WIDTH, LRU_WIDTH)


def _prep_layer(l, ffn1_norm, ffn1_w1, ffn1_w3, ffn1_w2, mix_norm, w_in, mla_q_norm, mla_w_uq, mla_kv_norm,
                mla_w_ukv, lru_conv_w, lru_conv_b, lru_w_a, lru_b_a, lru_w_x, lru_b_x, lru_lambda, sink_logits,
                w_branch, w_out, ffn2_norm, ffn2_w1, ffn2_w3, ffn2_w2):
    row = lambda v: v.reshape(1, -1)
    wi = w_in[l]
    c0, c1, c2, c3 = A_COLS, A_COLS + B_COLS, A_COLS + B_COLS + C_COLS, A_COLS + B_COLS + C_COLS + D_COLS
    w_a = wi[:, :c0].reshape(D_MODEL, 3, N_DIL, A_HEADS * HEAD_DIM).transpose(0, 2, 1, 3).reshape(D_MODEL, A_COLS)
    wb = wi[:, c0:c1]
    zeros = lambda n: jnp.zeros((D_MODEL, n), wi.dtype)
    w_b = jnp.concatenate([wb[:, :Q_LORA + KV_LORA], zeros(QK_NOPE), wb[:, Q_LORA + KV_LORA:],
                           zeros(SLOT - QK_NOPE - QK_ROPE)], axis=1)
    w_cbd = jnp.concatenate([wi[:, c1:c2], w_b, wi[:, c2:c3]], axis=1)
    wq = mla_w_uq[l].reshape(Q_LORA, B_HEADS, QK_NOPE + QK_ROPE)
    wq = jnp.pad(wq, ((0, 0), (0, 0), (0, SLOT - QK_NOPE - QK_ROPE))).reshape(Q_LORA, B_HEADS * SLOT)
    wkv = mla_w_ukv[l].reshape(KV_LORA, B_HEADS, QK_NOPE + V_DIM)
    wk = jnp.pad(wkv[:, :, :QK_NOPE], ((0, 0), (0, 0), (0, SLOT - QK_NOPE))).reshape(KV_LORA, B_HEADS * SLOT)
    wv = wkv[:, :, QK_NOPE:].reshape(KV_LORA, B_HEADS * V_DIM)
    wg = jnp.stack([jnp.concatenate([_block_diag(lru_w_a[l, d]), _block_diag(lru_w_x[l, d])], axis=1)
                    for d in range(2)])
    bg = jnp.concatenate([lru_b_a[l], lru_b_x[l]], axis=1)
    return dict(
        ffn1=(row(ffn1_norm[l]), ffn1_w1[l].astype(BF16), ffn1_w3[l].astype(BF16), ffn1_w2[l].astype(BF16)),
        ffn2=(row(ffn2_norm[l]), ffn2_w1[l].astype(BF16), ffn2_w3[l].astype(BF16), ffn2_w2[l].astype(BF16)),
        mix_norm=row(mix_norm[l]),
        w_a=w_a.astype(BF16), w_cbd=w_cbd.astype(BF16), w_g=wi[:, c3:].astype(BF16),
        qn=row(mla_q_norm[l]), kvn=row(mla_kv_norm[l]),
        wq=wq.astype(BF16), wk=wk.astype(BF16), wv=wv.astype(BF16),
        conv_w=lru_conv_w[l], conv_b=row(lru_conv_b[l]), wg=wg.astype(BF16), bg=bg, lam=lru_lambda[l],
        sink=sink_logits[l], w_branch=w_branch[l].astype(BF16), w_out=w_out[l].astype(BF16),
    )


def _rope_slot_tables(seq):
    inv = ROPE_THETA ** (-jnp.arange(0, QK_ROPE, 2, dtype=F32) / QK_ROPE)
    ang = jnp.arange(seq, dtype=F32)[:, None] * inv[None, :]
    cos, sin = jnp.cos(ang), jnp.sin(ang)
    scale = (QK_NOPE + QK_ROPE) ** -0.5 * np.log2(np.e)
    z = lambda n: jnp.zeros((seq, n), F32)
    tail = SLOT - QK_NOPE - QK_ROPE
    cos_q = jnp.concatenate([jnp.full((seq, QK_NOPE), scale, F32), cos * scale, cos * scale, z(tail)], axis=1)
    sin_q = jnp.concatenate([z(QK_NOPE), -sin * scale, sin * scale, z(tail)], axis=1)
    cos_k = jnp.concatenate([z(QK_NOPE), cos, cos, z(tail)], axis=1)
    sin_k = jnp.concatenate([z(QK_NOPE), -sin, sin, z(tail)], axis=1)
    return cos_q, sin_q, cos_k, sin_k


def _layer(x, w, batch, seq, final_g):
    t = batch * seq
    x = _ffn(x, *w['ffn1'], final_g, False)
    *pas, h = _proj_a(x, w['mix_norm'], w['w_a'], batch, seq)
    pc, pb, pd = _proj_bcd(h, w['w_cbd'], ((C_COLS, BF16), (B_PAD_COLS, F32), (D_COLS, F32)))

    oa, la = [], []
    for pa, (window, dil) in zip(pas, DIL_CONFIGS):
        o, lse = _band_attention(
            pa, q_col=0, k_col=1, v_col=2, kv_width=A_HEADS * HEAD_DIM,
            n_heads=A_HEADS, n_kv=A_HEADS, radius=window // (2 * dil), step=dil, sink=None,
            out_dtype=F32, emit_lse=True)
        oa.append(o)
        la.append(lse)

    q, k, vt = _mla_prep(pb, w['qn'], w['kvn'], w['wq'], w['wk'], w['wv'], _rope_slot_tables(seq), batch, seq)
    yb = _mla_attn(q.reshape(batch, seq, -1), k.reshape(batch, seq, -1), vt, batch, seq)

    kvw = C_KV_HEADS * HEAD_DIM
    yc = _band_attention(
        pc.reshape(batch, 1, seq, C_COLS),
        q_col=0, k_col=C_HEADS * HEAD_DIM // kvw, v_col=C_HEADS * HEAD_DIM // kvw + 1, kv_width=kvw,
        n_heads=C_HEADS, n_kv=C_KV_HEADS, radius=C_RADIUS, step=1, sink=w['sink'],
        out_dtype=BF16, emit_lse=False)[0]

    hf, hr = _lru(pd.reshape(batch, seq, D_COLS), w['conv_w'], w['conv_b'], w['wg'], w['bg'], w['lam'])

    ya, yd = _branch_out(oa, la, hf.reshape(t, LRU_WIDTH), hr.reshape(t, LRU_WIDTH), pd, seq)
    x = _gate_merge_out(x, h, (ya, yb.reshape(t, BRANCH_W), yc.reshape(t, BRANCH_W), yd),
                        w['w_g'], w['w_branch'], w['w_out'])
    return _ffn(x, *w['ffn2'], final_g, final_g is not None and w.get('last', False))


def _trunk(x, layers, final_norm):
    batch, seq, _ = x.shape
    h = x.reshape(batch * seq, D_MODEL)
    fg = final_norm.reshape(1, -1)
    for l, w in enumerate(layers):
        h = _layer(h, dict(w, last=(l == len(layers) - 1)), batch, seq, fg)
    return h.reshape(batch, seq, D_MODEL)


def kernel(x_prompt, x_sample, ffn1_norm, ffn1_w1, ffn1_w3, ffn1_w2, mix_norm, w_in, mla_q_norm, mla_w_uq, mla_kv_norm, mla_w_ukv, lru_conv_w, lru_conv_b, lru_w_a, lru_b_a, lru_w_x, lru_b_x, lru_lambda, sink_logits, w_branch, w_out, ffn2_norm, ffn2_w1, ffn2_w3, ffn2_w2, final_norm):
    layers = [_prep_layer(l, ffn1_norm, ffn1_w1, ffn1_w3, ffn1_w2, mix_norm, w_in, mla_q_norm, mla_w_uq,
                          mla_kv_norm, mla_w_ukv, lru_conv_w, lru_conv_b, lru_w_a, lru_b_a, lru_w_x, lru_b_x,
                          lru_lambda, sink_logits, w_branch, w_out, ffn2_norm, ffn2_w1, ffn2_w3, ffn2_w2)
              for l in range(DEPTH)]
    return (_trunk(x_prompt, layers, final_norm), _trunk(x_sample, layers, final_norm))
```

```python
import functools

import numpy as np
import jax
import jax.numpy as jnp
from jax import lax
from jax.experimental import pallas as pl
from jax.experimental.pallas import tpu as pltpu

F32 = jnp.float32
BF16 = jnp.bfloat16

D_MODEL = 2048
DEPTH = 2
HEAD_DIM = 64
N_BRANCH = 4
BRANCH_W = 512
DIL_CONFIGS = ((128, 1), (512, 4), (2048, 16))
N_DIL = 3
A_HEADS = 8
B_HEADS = 8
Q_LORA = 384
KV_LORA = 128
QK_NOPE = 64
QK_ROPE = 32
V_DIM = 64
ROPE_THETA = 10000.0
C_HEADS = 8
C_KV_HEADS = 2
C_RADIUS = 128
LRU_WIDTH = 512
LRU_BLOCKS = 8
LRU_BLOCK = 64
CONV_W = 4
CONV_PAD_L = 2
LRU_C = 8.0
D_FF = 5632
EPS = 1e-6
NEG = -1e30

A_COLS = 3 * N_DIL * A_HEADS * HEAD_DIM
B_COLS = Q_LORA + KV_LORA + QK_ROPE
C_COLS = (C_HEADS + 2 * C_KV_HEADS) * HEAD_DIM
D_COLS = 2 * LRU_WIDTH
G_COLS = N_BRANCH * D_MODEL

LANES = 128
SUBLANES = 8
SLOT = LANES
B_PAD_COLS = Q_LORA + KV_LORA + SLOT
ROW_CHUNK = 128
VMEM_LIMIT = 52 * 1024 * 1024
VMEM_LIMIT_FFN = 56 * 1024 * 1024


def _cparams(*sem, vmem=VMEM_LIMIT):
    return pltpu.CompilerParams(dimension_semantics=sem, vmem_limit_bytes=vmem)


def _rms(x, g):
    return x * lax.rsqrt(jnp.mean(x * x, axis=-1, keepdims=True) + EPS) * g


def _pick(n, pref):
    t = min(n, pref)
    while n % t:
        t //= 2
    return t


def _ffn_kernel(x_ref, g_ref, w1_ref, w3_ref, w2_ref, fg_ref, o_ref, h_ref, *, final_norm):
    j = pl.program_id(1)
    last = pl.num_programs(1) - 1
    tm = x_ref.shape[0]
    halves = (slice(0, tm // 2), slice(tm // 2, tm))

    def down(h):
        a = jnp.dot(h, w1_ref[...], preferred_element_type=F32)
        b = jnp.dot(h, w3_ref[...], preferred_element_type=F32)
        act = (a * jax.nn.sigmoid(a) * b).astype(BF16)
        return jnp.dot(act, w2_ref[...], preferred_element_type=F32)

    @pl.when(j == 0)
    def _():
        for rows in halves:
            h = _rms(x_ref[rows, :], g_ref[...]).astype(BF16)
            h_ref[rows, :] = h
            o_ref[rows, :] = down(h)

    @pl.when((j > 0) & (j < last))
    def _():
        for rows in halves:
            o_ref[rows, :] += down(h_ref[rows, :])

    @pl.when(j == last)
    def _():
        for rows in halves:
            y = x_ref[rows, :] + 0.5 * (o_ref[rows, :] + down(h_ref[rows, :]))
            if final_norm:
                y = _rms(y, fg_ref[...])
            o_ref[rows, :] = y


def _ffn(x, g, w1, w3, w2, fg, final_norm):
    t = x.shape[0]
    tm, tf = _pick(t, 1024), 512
    return pl.pallas_call(
        functools.partial(_ffn_kernel, final_norm=final_norm),
        grid=(t // tm, D_FF // tf),
        in_specs=[
            pl.BlockSpec((tm, D_MODEL), lambda i, j: (i, 0)),
            pl.BlockSpec((1, D_MODEL), lambda i, j: (0, 0)),
            pl.BlockSpec((D_MODEL, tf), lambda i, j: (0, j)),
            pl.BlockSpec((D_MODEL, tf), lambda i, j: (0, j)),
            pl.BlockSpec((tf, D_MODEL), lambda i, j: (j, 0)),
            pl.BlockSpec((1, D_MODEL), lambda i, j: (0, 0)),
        ],
        out_specs=pl.BlockSpec((tm, D_MODEL), lambda i, j: (i, 0)),
        out_shape=jax.ShapeDtypeStruct((t, D_MODEL), F32),
        scratch_shapes=[pltpu.VMEM((tm, D_MODEL), BF16)],
        compiler_params=_cparams("parallel", "arbitrary", vmem=VMEM_LIMIT_FFN),
        name="ffn",
    )(x, g, w1, w3, w2, fg)


def _proj_bcd_kernel(h_ref, w_ref, *o_refs):
    r = jnp.dot(h_ref[...], w_ref[...], preferred_element_type=F32)
    c = 0
    for o_ref in o_refs:
        wd = o_ref.shape[1]
        o_ref[...] = r[:, c:c + wd].astype(o_ref.dtype)
        c += wd


def _proj_bcd(h, w, outs):
    t, n = h.shape[0], w.shape[1]
    tm = _pick(t, 512)
    return pl.pallas_call(
        _proj_bcd_kernel,
        grid=(t // tm,),
        in_specs=[pl.BlockSpec((tm, D_MODEL), lambda i: (i, 0)),
                  pl.BlockSpec((D_MODEL, n), lambda i: (0, 0))],
        out_specs=[pl.BlockSpec((tm, wd), lambda i: (i, 0)) for wd, _ in outs],
        out_shape=[jax.ShapeDtypeStruct((t, wd), dt) for wd, dt in outs],
        compiler_params=_cparams("parallel"),
        name="proj_bcd",
    )(h, w)


def _proj_a_kernel(x_ref, g_ref, w_ref, o1_ref, o2_ref, o3_ref, hout_ref, hn_ref, h_ref, *, tm):
    j = pl.program_id(1)
    chunks = D_MODEL // LANES

    @pl.when(j == 0)
    def _():
        hn = _rms(x_ref[...], g_ref[...])
        for c in range(chunks):
            hn_ref[c] = hn[:, c * LANES:(c + 1) * LANES]
        h_ref[...] = hn.astype(BF16)
        hout_ref[...] = hn.astype(BF16)

    for g, o_ref in enumerate((o1_ref, o2_ref, o3_ref)):
        dil = DIL_CONFIGS[g][1]
        n = tm // dil

        @pl.when(j == g)
        def _(o_ref=o_ref, dil=dil, n=n):
            if dil > 1:
                for r in range(dil):
                    for c in range(chunks):
                        h_ref[r * n:(r + 1) * n, c * LANES:(c + 1) * LANES] = (
                            hn_ref[c, pl.ds(r, n, stride=dil), :].astype(BF16))
            res = jnp.dot(h_ref[...], w_ref[...], preferred_element_type=F32)
            for r in range(dil):
                o_ref[0, r] = res[r * n:(r + 1) * n].astype(BF16)


def _proj_a(x, g, w, batch, seq):
    t = x.shape[0]
    tm = _pick(seq, 512)
    nseq = seq // tm
    gw = A_COLS // N_DIL
    out_specs = [pl.BlockSpec((1, dil, tm // dil, gw), lambda i, j: (i // nseq, 0, i % nseq, 0))
                 for _, dil in DIL_CONFIGS]
    out_shape = [jax.ShapeDtypeStruct((batch, dil, seq // dil, gw), BF16) for _, dil in DIL_CONFIGS]
    out_specs.append(pl.BlockSpec((tm, D_MODEL), lambda i, j: (i, 0)))
    out_shape.append(jax.ShapeDtypeStruct((t, D_MODEL), BF16))
    return pl.pallas_call(
        functools.partial(_proj_a_kernel, tm=tm),
        grid=(t // tm, N_DIL),
        in_specs=[pl.BlockSpec((tm, D_MODEL), lambda i, j: (i, 0)),
                  pl.BlockSpec((1, D_MODEL), lambda i, j: (0, 0)),
                  pl.BlockSpec((D_MODEL, gw), lambda i, j: (0, j))],
        out_specs=out_specs,
        out_shape=out_shape,
        scratch_shapes=[pltpu.VMEM((D_MODEL // LANES, tm, LANES), F32), pltpu.VMEM((tm, D_MODEL), BF16)],
        compiler_params=_cparams("parallel", "arbitrary"),
        name="proj_a",
    )(x, g, w)


def _band_kernel(*refs, n_heads, n_kv, radius, sb, tq, has_sink, emit_lse):
    if has_sink:
        sink_ref, refs = refs[0], refs[1:]
    bias_ref, q_ref, kp_ref, kc_ref, kn_ref, vp_ref, vc_ref, vn_ref = refs[:8]
    o_ref = refs[8]
    kx_ref, vx_ref, vt_ref, qt_ref = refs[-4:]
    for x_ref, (p_ref, c_ref, n_ref) in ((kx_ref, (kp_ref, kc_ref, kn_ref)), (vx_ref, (vp_ref, vc_ref, vn_ref))):
        x_ref[0:radius, :] = p_ref[0, 0, tq - radius:, :]
        x_ref[radius:radius + tq, :] = c_ref[0, 0]
        x_ref[radius + tq:, :] = n_ref[0, 0, :radius, :]
    vt_ref[...] = vx_ref[...].astype(F32).T.astype(BF16)
    qt_ref[...] = (q_ref[0, 0].astype(F32) * (HEAD_DIM ** -0.5)).T.astype(BF16)
    w = sb + 2 * radius
    rep = n_heads // n_kv
    heads = range(n_heads)
    zeros = jnp.zeros((HEAD_DIM, sb), BF16)
    nsub = tq // sb
    tile, last_tile = pl.program_id(2), pl.num_programs(2) - 1
    for u in range(nsub):
        rows = slice(u * sb, (u + 1) * sb)
        win = slice(u * sb, u * sb + w)
        kind = 0
        if u == 0:
            kind = kind + (tile == 0).astype(jnp.int32)
        if u == nsub - 1:
            kind = kind + 2 * (tile == last_tile).astype(jnp.int32)
        scores = []
        for h in heads:
            g = h // rep
            qh = qt_ref[h * HEAD_DIM:(h + 1) * HEAD_DIM, rows]
            rhs = jnp.concatenate([qh, zeros] if g % 2 == 0 else [zeros, qh], axis=0)
            pair = slice((g // 2) * 2 * HEAD_DIM, (g // 2 + 1) * 2 * HEAD_DIM)
            scores.append(jnp.dot(kx_ref[win, pair], rhs, preferred_element_type=F32) + bias_ref[kind, h])
        probs, stats = [], []
        for h in heads:
            m = jnp.max(scores[h], axis=0, keepdims=True)
            if has_sink:
                m = jnp.maximum(m, sink_ref[h])
            p = jnp.exp(scores[h] - m)
            l = jnp.sum(p, axis=0, keepdims=True)
            if has_sink:
                l = l + jnp.exp(sink_ref[h] - m)
            probs.append(p.astype(BF16))
            stats.append((m, l))
        outs, lses = [], []
        for h in heads:
            g = h // rep
            m, l = stats[h]
            outs.append(jnp.dot(vt_ref[g * HEAD_DIM:(g + 1) * HEAD_DIM, win], probs[h],
                                preferred_element_type=F32) / l)
            lses.append(jnp.broadcast_to(m + jnp.log(l), (HEAD_DIM, sb)))
        o_ref[0, 0, rows, :] = jnp.concatenate(outs, axis=0).T.astype(o_ref.dtype)
        if emit_lse:
            refs[9][0, 0, rows, :] = jnp.concatenate(lses, axis=0).T


def _alibi(n):
    return np.asarray(2.0 ** (-8.0 * np.arange(1, n + 1) / n), dtype=np.float32)


def _band_bias(n_heads, radius, step, sb):
    w = sb + 2 * radius
    shape = (4, n_heads, w, sb)
    kind, h, kj, qi = (lax.broadcasted_iota(jnp.int32, shape, d) for d in range(4))
    rel = jnp.abs(kj - radius - qi)
    kpos = kj - radius
    ok = (rel <= radius) & ((kpos >= 0) | (kind % 2 == 0)) & ((kpos < sb) | (kind < 2))
    slopes = jnp.asarray(_alibi(n_heads))[h]
    return jnp.where(ok, -slopes * (step * rel).astype(F32), NEG)


def _band_attention(arr, *, q_col, k_col, v_col, kv_width, n_heads, n_kv, radius, step, sink, out_dtype,
                    emit_lse):
    batch, n_classes, length, _ = arr.shape
    qw = n_heads * HEAD_DIM
    tq = _pick(length, 512)
    sb = min(tq, 128)
    nblk = length // tq

    def kv_spec(col, shift):
        return pl.BlockSpec((1, 1, tq, kv_width), lambda b, r, i: (b, r, jnp.clip(i + shift, 0, nblk - 1), col))

    bias = _band_bias(n_heads, radius, step, sb)
    in_specs = [pl.BlockSpec(bias.shape, lambda b, r, i: (0, 0, 0, 0)),
                pl.BlockSpec((1, 1, tq, qw), lambda b, r, i: (b, r, i, q_col)),
                kv_spec(k_col, -1), kv_spec(k_col, 0), kv_spec(k_col, 1),
                kv_spec(v_col, -1), kv_spec(v_col, 0), kv_spec(v_col, 1)]
    args = [bias] + [arr] * 7
    if sink is not None:
        in_specs = [pl.BlockSpec(memory_space=pltpu.SMEM)] + in_specs
        args = [sink] + args
    o_spec = pl.BlockSpec((1, 1, tq, qw), lambda b, r, i: (b, r, i, 0))
    o_shape = jax.ShapeDtypeStruct((batch, n_classes, length, qw), out_dtype)
    out_specs, out_shape = [o_spec], [o_shape]
    if emit_lse:
        out_specs, out_shape = [o_spec, o_spec], [o_shape, jax.ShapeDtypeStruct(o_shape.shape, F32)]
    return pl.pallas_call(
        functools.partial(_band_kernel, n_heads=n_heads, n_kv=n_kv, radius=radius, sb=sb, tq=tq,
                          has_sink=sink is not None, emit_lse=emit_lse),
        grid=(batch, n_classes, nblk),
        in_specs=in_specs,
        out_specs=out_specs,
        out_shape=out_shape,
        scratch_shapes=[pltpu.VMEM((tq + 2 * radius, kv_width), BF16), pltpu.VMEM((tq + 2 * radius, kv_width), BF16),
                        pltpu.VMEM((kv_width, tq + 2 * radius), BF16), pltpu.VMEM((qw, tq), BF16)],
        compiler_params=_cparams("parallel", "parallel", "arbitrary"),
        name="band_attn",
    )(*args)


def _rope_slot(x, cos_t, sin_t):
    lane = lax.broadcasted_iota(jnp.int32, x.shape, 1)
    first = (lane >= QK_NOPE) & (lane < QK_NOPE + QK_ROPE // 2)
    partner = jnp.where(first, pltpu.roll(x, SLOT - QK_ROPE // 2, 1), pltpu.roll(x, QK_ROPE // 2, 1))
    return x * cos_t + partner * sin_t


def _mla_prep_kernel(pb_ref, qn_ref, kvn_ref, wq_ref, wk_ref, wv_ref, cq_ref, sq_ref, ck_ref, sk_ref,
                     q_out, k_out, vt_out):
    pb = pb_ref[...]
    cq = _rms(pb[:, :Q_LORA], qn_ref[...]).astype(BF16)
    ckv = _rms(pb[:, Q_LORA:Q_LORA + KV_LORA], kvn_ref[...]).astype(BF16)
    kr = pb[:, Q_LORA + KV_LORA:]
    q = jnp.dot(cq, wq_ref[...], preferred_element_type=F32)
    k = jnp.dot(ckv, wk_ref[...], preferred_element_type=F32)
    vt_out[0] = jnp.dot(ckv, wv_ref[...], preferred_element_type=F32).T.astype(BF16)
    kr = _rope_slot(kr, ck_ref[...], sk_ref[...])
    for h in range(B_HEADS):
        sl = slice(h * SLOT, (h + 1) * SLOT)
        q_out[:, sl] = _rope_slot(q[:, sl], cq_ref[...], sq_ref[...]).astype(BF16)
        k_out[:, sl] = (k[:, sl] + kr).astype(BF16)


def _mla_prep(pb, qn, kvn, wq, wk, wv, tabs, batch, seq):
    t = pb.shape[0]
    tm = _pick(seq, 512)
    nseq = seq // tm
    row = lambda i: (i, 0)
    fixed = lambda i: (0, 0)
    tab = pl.BlockSpec((tm, SLOT), lambda i: (i % nseq, 0))
    return pl.pallas_call(
        _mla_prep_kernel,
        grid=(t // tm,),
        in_specs=[pl.BlockSpec((tm, B_PAD_COLS), row),
                  pl.BlockSpec((1, Q_LORA), fixed), pl.BlockSpec((1, KV_LORA), fixed),
                  pl.BlockSpec((Q_LORA, B_HEADS * SLOT), fixed),
                  pl.BlockSpec((KV_LORA, B_HEADS * SLOT), fixed),
                  pl.BlockSpec((KV_LORA, B_HEADS * V_DIM), fixed),
                  tab, tab, tab, tab],
        out_specs=[pl.BlockSpec((tm, B_HEADS * SLOT), row), pl.BlockSpec((tm, B_HEADS * SLOT), row),
                   pl.BlockSpec((1, B_HEADS * V_DIM, tm), lambda i: (i // nseq, 0, i % nseq))],
        out_shape=[jax.ShapeDtypeStruct((t, B_HEADS * SLOT), BF16), jax.ShapeDtypeStruct((t, B_HEADS * SLOT), BF16),
                   jax.ShapeDtypeStruct((batch, B_HEADS * V_DIM, seq), BF16)],
        compiler_params=_cparams("parallel"),
        name="mla_prep",
    )(pb, qn, kvn, wq, wk, wv, *tabs)


def _mla_attn_kernel(q_ref, k_ref, vt_ref, o_ref, st_ref, *, seq, kc):
    chunks = [slice(c * kc, (c + 1) * kc) for c in range(seq // kc)]

    def qk(hh, rows):
        sl = slice(hh * SLOT, (hh + 1) * SLOT)
        st = lax.dot_general(k_ref[0, rows, sl], q_ref[0, :, sl], (((1,), (1,)), ((), ())),
                             preferred_element_type=F32)
        st_ref[hh, rows, :] = st
        return jnp.max(st, axis=0, keepdims=True)

    def pv(hh, rows, m):
        p = jnp.exp2(st_ref[hh, rows, :] - m)
        return (jnp.sum(p, axis=0, keepdims=True),
                jnp.dot(vt_ref[0, :, rows], p.astype(BF16), preferred_element_type=F32))

    def add(acc, new):
        return new if acc is None else (acc[0] + new[0], acc[1] + new[1])

    m0 = functools.reduce(jnp.maximum, [qk(0, rows) for rows in chunks])
    m1 = acc0 = acc1 = None
    for rows in chunks:
        acc0 = add(acc0, pv(0, rows, m0))
        mc = qk(1, rows)
        m1 = mc if m1 is None else jnp.maximum(m1, mc)
    for rows in chunks:
        acc1 = add(acc1, pv(1, rows, m1))
    outs = [acc[1][hh * V_DIM:(hh + 1) * V_DIM] / acc[0] for hh, acc in enumerate((acc0, acc1))]
    o_ref[0] = jnp.concatenate(outs, axis=0).T.astype(o_ref.dtype)


def _mla_attn(q, k, vt, batch, seq):
    tq = _pick(seq, 512)
    kc = _pick(seq, 256)
    pairs = B_HEADS // 2
    return pl.pallas_call(
        functools.partial(_mla_attn_kernel, seq=seq, kc=kc),
        scratch_shapes=[pltpu.VMEM((2, seq, tq), F32)],
        grid=(batch, pairs, seq // tq),
        in_specs=[pl.BlockSpec((1, tq, 2 * SLOT), lambda b, hp, i: (b, i, hp)),
                  pl.BlockSpec((1, seq, 2 * SLOT), lambda b, hp, i: (b, 0, hp)),
                  pl.BlockSpec((1, 2 * V_DIM, seq), lambda b, hp, i: (b, hp, 0))],
        out_specs=pl.BlockSpec((1, tq, 2 * V_DIM), lambda b, hp, i: (b, i, hp)),
        out_shape=jax.ShapeDtypeStruct((batch, seq, B_HEADS * V_DIM), BF16),
        compiler_params=_cparams("parallel", "parallel", "arbitrary"),
        name="mla_attn",
    )(q, k, vt)


def _sigmoid_tanh(x):
    return 0.5 * (jnp.tanh(0.5 * x) + 1.0)


def _softplus(x):
    return jnp.maximum(x, 0.0) + jnp.log1p(jnp.exp(-jnp.abs(x)))


def _lru_kernel(xf_ref, xfp_ref, xfn_ref, xr_ref, xrp_ref, xrn_ref, cw_ref, cb_ref, wg_ref, bg_ref, lam_ref,
                hf_ref, hr_ref, ext_ref, af_ref, uf_ref, ar_ref, ur_ref, cf_ref, cr_ref, *, ts):
    i = pl.program_id(1)
    nt = pl.num_programs(1)
    halo = SUBLANES

    @pl.when(i == 0)
    def _():
        cf_ref[...] = jnp.zeros_like(cf_ref)
        cr_ref[...] = jnp.zeros_like(cr_ref)

    def gates(d, tile, x_ref, xp_ref, xn_ref, a_ref, u_ref):
        ext_ref[0:halo, :] = jnp.where(tile > 0, xp_ref[0], 0.0)
        ext_ref[halo:halo + ts, :] = x_ref[0]
        ext_ref[halo + ts:, :] = jnp.where(tile < nt - 1, xn_ref[0], 0.0)
        xc = cb_ref[...]
        for tap in range(CONV_W):
            off = halo - CONV_PAD_L + tap
            xc = xc + ext_ref[off:off + ts, :] * cw_ref[tap:tap + 1, :]
        g = jnp.dot(xc.astype(BF16), wg_ref[d], preferred_element_type=F32) + bg_ref[d:d + 1, :]
        r = _sigmoid_tanh(g[:, :LRU_WIDTH])
        ig = _sigmoid_tanh(g[:, LRU_WIDTH:])
        log_a = -LRU_C * r * _softplus(-lam_ref[d:d + 1, :])
        a = jnp.exp(log_a)
        a_ref[...] = a
        u_ref[...] = jnp.sqrt(-jnp.tanh(log_a) * (1.0 + a * a)) * (ig * xc)

    gates(0, i, xf_ref, xfp_ref, xfn_ref, af_ref, uf_ref)
    gates(1, nt - 1 - i, xr_ref, xrp_ref, xrn_ref, ar_ref, ur_ref)

    ng = ts // SUBLANES
    row = lax.broadcasted_iota(jnp.int32, (SUBLANES, LRU_WIDTH), 0)

    def group_scan(a, u, carry, reverse):
        for d in (1, 2, 4):
            if reverse:
                keep, sh = row < SUBLANES - d, SUBLANES - d
            else:
                keep, sh = row >= d, d
            ap = jnp.where(keep, pltpu.roll(a, sh, 0), 1.0)
            up = jnp.where(keep, pltpu.roll(u, sh, 0), 0.0)
            u = a * up + u
            a = a * ap
        return u + a * carry

    def body(g, carry):
        cf, cr = carry
        r0 = pl.multiple_of(g * SUBLANES, SUBLANES)
        hf = group_scan(af_ref[pl.ds(r0, SUBLANES), :], uf_ref[pl.ds(r0, SUBLANES), :], cf, False)
        hf_ref[0, pl.ds(r0, SUBLANES), :] = hf
        r1 = pl.multiple_of((ng - 1 - g) * SUBLANES, SUBLANES)
        hr = group_scan(ar_ref[pl.ds(r1, SUBLANES), :], ur_ref[pl.ds(r1, SUBLANES), :], cr, True)
        hr_ref[0, pl.ds(r1, SUBLANES), :] = hr
        return (jnp.broadcast_to(hf[SUBLANES - 1:, :], hf.shape), jnp.broadcast_to(hr[:1, :], hr.shape))

    cf, cr = lax.fori_loop(0, ng, body, (cf_ref[...], cr_ref[...]))
    cf_ref[...] = cf
    cr_ref[...] = cr


def _lru(pd, cw, cb, wg, bg, lam):
    batch, seq, _ = pd.shape
    ts = _pick(seq, 512)
    nt = seq // ts
    per = ts // SUBLANES
    last8 = seq // SUBLANES - 1

    def tile_specs(tile):
        return [pl.BlockSpec((1, ts, LRU_WIDTH), lambda b, i: (b, tile(i), 0)),
                pl.BlockSpec((1, SUBLANES, LRU_WIDTH), lambda b, i: (b, jnp.maximum(tile(i) * per - 1, 0), 0)),
                pl.BlockSpec((1, SUBLANES, LRU_WIDTH), lambda b, i: (b, jnp.minimum((tile(i) + 1) * per, last8), 0))]

    fwd_tile = lambda i: i
    rev_tile = lambda i: nt - 1 - i
    fixed2 = lambda b, i: (0, 0)
    shp = jax.ShapeDtypeStruct((batch, seq, LRU_WIDTH), F32)
    tile_f32 = pltpu.VMEM((ts, LRU_WIDTH), F32)
    return pl.pallas_call(
        functools.partial(_lru_kernel, ts=ts),
        grid=(batch, nt),
        in_specs=tile_specs(fwd_tile) + tile_specs(rev_tile)
                 + [pl.BlockSpec((CONV_W, LRU_WIDTH), fixed2), pl.BlockSpec((1, LRU_WIDTH), fixed2),
                    pl.BlockSpec((2, LRU_WIDTH, 2 * LRU_WIDTH), lambda b, i: (0, 0, 0)),
                    pl.BlockSpec((2, 2 * LRU_WIDTH), fixed2), pl.BlockSpec((2, LRU_WIDTH), fixed2)],
        out_specs=[pl.BlockSpec((1, ts, LRU_WIDTH), lambda b, i: (b, i, 0)),
                   pl.BlockSpec((1, ts, LRU_WIDTH), lambda b, i: (b, nt - 1 - i, 0))],
        out_shape=[shp, shp],
        scratch_shapes=[pltpu.VMEM((ts + 2 * SUBLANES, LRU_WIDTH), F32), tile_f32, tile_f32, tile_f32, tile_f32,
                        pltpu.VMEM((SUBLANES, LRU_WIDTH), F32), pltpu.VMEM((SUBLANES, LRU_WIDTH), F32)],
        compiler_params=_cparams("parallel", "arbitrary"),
        name="lru",
    )(pd, pd, pd, pd, pd, pd, cw, cb, wg, bg, lam)


def _gelu_tanh(x):
    return 0.5 * x * (1.0 + jnp.tanh(np.sqrt(2.0 / np.pi).astype(np.float32) * (x + 0.044715 * (x * x * x))))


def _branch_out_kernel(o0, o1, o2, l0, l1, l2, hf_ref, hr_ref, gd_ref, ya_ref, yd_ref, *nat_refs, tm):
    def natural(ref, g, buf):
        dil = DIL_CONFIGS[g][1]
        if dil == 1:
            return ref[0, 0]
        chunks = BRANCH_W // LANES
        for r in range(dil):
            for c in range(chunks):
                buf[c, pl.ds(r, tm // dil, stride=dil), :] = ref[0, r, :, c * LANES:(c + 1) * LANES]
        return jnp.concatenate([buf[c] for c in range(chunks)], axis=1)

    outs = (o0[0, 0], natural(o1, 1, nat_refs[0]), natural(o2, 2, nat_refs[1]))
    lse = (l0[0, 0], natural(l1, 1, nat_refs[2]), natural(l2, 2, nat_refs[3]))
    m = jnp.maximum(jnp.maximum(lse[0], lse[1]), lse[2])
    e = [jnp.exp(v - m) for v in lse]
    tot = e[0] + e[1] + e[2]
    ya = (e[0] / tot) * outs[0] + (e[1] / tot) * outs[1] + (e[2] / tot) * outs[2]
    ya_ref[...] = ya.astype(BF16)
    yd_ref[...] = ((hf_ref[...] + hr_ref[...]) * _gelu_tanh(gd_ref[...])).astype(BF16)


def _branch_out(oa, la, hf, hr, pd, seq):
    t = hf.shape[0]
    tm = _pick(seq, 256)
    nseq = seq // tm
    blk = pl.BlockSpec((tm, BRANCH_W), lambda i: (i, 0))
    cls = [pl.BlockSpec((1, dil, tm // dil, BRANCH_W), lambda i: (i // nseq, 0, i % nseq, 0))
           for _, dil in DIL_CONFIGS]
    shp = jax.ShapeDtypeStruct((t, BRANCH_W), BF16)
    return pl.pallas_call(
        functools.partial(_branch_out_kernel, tm=tm),
        grid=(t // tm,),
        in_specs=cls + cls + [blk, blk, pl.BlockSpec((tm, BRANCH_W), lambda i: (i, 1))],
        out_specs=[blk, blk],
        out_shape=[shp, shp],
        scratch_shapes=[pltpu.VMEM((BRANCH_W // LANES, tm, LANES), F32)] * 4,
        compiler_params=_cparams("parallel"),
        name="branch_out",
    )(*oa, *la, hf, hr, pd)


def _gate_merge_out_kernel(x_ref, h_ref, ya, yb, yc, yd, g0, g1, g2, g3, wb_ref, wo_ref, o_ref):
    j = pl.program_id(1)

    @pl.when(j == 0)
    def _():
        o_ref[...] = jnp.zeros_like(o_ref)

    h = h_ref[...]
    merged = None
    for nbr, (y_ref, wg_ref) in enumerate(zip((ya, yb, yc, yd), (g0, g1, g2, g3))):
        gate = jax.nn.sigmoid(jnp.dot(h, wg_ref[...], preferred_element_type=F32))
        term = gate * jnp.dot(y_ref[...], wb_ref[nbr], preferred_element_type=F32)
        merged = term if merged is None else merged + term
    o_ref[...] += jnp.dot(merged.astype(BF16), wo_ref[...], preferred_element_type=F32)

    @pl.when(j == pl.num_programs(1) - 1)
    def _():
        o_ref[...] = x_ref[...] + o_ref[...]


def _gate_merge_out(x, h, ys, wg, wb, wo):
    t = x.shape[0]
    tm, tn = _pick(t, 512), 512
    nj = D_MODEL // tn
    row = lambda i, j: (i, 0)
    gate_specs = [pl.BlockSpec((D_MODEL, tn), functools.partial(lambda i, j, nbr: (0, nbr * nj + j), nbr=nbr))
                  for nbr in range(N_BRANCH)]
    return pl.pallas_call(
        _gate_merge_out_kernel,
        grid=(t // tm, nj),
        in_specs=[pl.BlockSpec((tm, D_MODEL), row), pl.BlockSpec((tm, D_MODEL), row)]
                 + [pl.BlockSpec((tm, BRANCH_W), row)] * N_BRANCH + gate_specs
                 + [pl.BlockSpec((N_BRANCH, BRANCH_W, tn), lambda i, j: (0, 0, j)),
                    pl.BlockSpec((tn, D_MODEL), lambda i, j: (j, 0))],
        out_specs=pl.BlockSpec((tm, D_MODEL), row),
        out_shape=jax.ShapeDtypeStruct((t, D_MODEL), F32),
        compiler_params=_cparams("parallel", "arbitrary"),
        name="gate_merge_out",
    )(x, h, *ys, wg, wg, wg, wg, wb, wo)


def _block_diag(w):
    eye = jnp.eye(LRU_BLOCKS, dtype=w.dtype)
    return jnp.einsum('nef,nm->nemf', w, eye).reshape(LRU_WIDTH, LRU_WIDTH)


def _prep_layer(l, ffn1_norm, ffn1_w1, ffn1_w3, ffn1_w2, mix_norm, w_in, mla_q_norm, mla_w_uq, mla_kv_norm,
                mla_w_ukv, lru_conv_w, lru_conv_b, lru_w_a, lru_b_a, lru_w_x, lru_b_x, lru_lambda, sink_logits,
                w_branch, w_out, ffn2_norm, ffn2_w1, ffn2_w3, ffn2_w2):
    row = lambda v: v.reshape(1, -1)
    wi = w_in[l]
    c0, c1, c2, c3 = A_COLS, A_COLS + B_COLS, A_COLS + B_COLS + C_COLS, A_COLS + B_COLS + C_COLS + D_COLS
    w_a = wi[:, :c0].reshape(D_MODEL, 3, N_DIL, A_HEADS * HEAD_DIM).transpose(0, 2, 1, 3).reshape(D_MODEL, A_COLS)
    wb = wi[:, c0:c1]
    zeros = lambda n: jnp.zeros((D_MODEL, n), wi.dtype)
    w_b = jnp.concatenate([wb[:, :Q_LORA + KV_LORA], zeros(QK_NOPE), wb[:, Q_LORA + KV_LORA:],
                           zeros(SLOT - QK_NOPE - QK_ROPE)], axis=1)
    w_cbd = jnp.concatenate([wi[:, c1:c2], w_b, wi[:, c2:c3]], axis=1)
    wq = mla_w_uq[l].reshape(Q_LORA, B_HEADS, QK_NOPE + QK_ROPE)
    wq = jnp.pad(wq, ((0, 0), (0, 0), (0, SLOT - QK_NOPE - QK_ROPE))).reshape(Q_LORA, B_HEADS * SLOT)
    wkv = mla_w_ukv[l].reshape(KV_LORA, B_HEADS, QK_NOPE + V_DIM)
    wk = jnp.pad(wkv[:, :, :QK_NOPE], ((0, 0), (0, 0), (0, SLOT - QK_NOPE))).reshape(KV_LORA, B_HEADS * SLOT)
    wv = wkv[:, :, QK_NOPE:].reshape(KV_LORA, B_HEADS * V_DIM)
    wg = jnp.stack([jnp.concatenate([_block_diag(lru_w_a[l, d]), _block_diag(lru_w_x[l, d])], axis=1)
                    for d in range(2)])
    bg = jnp.concatenate([lru_b_a[l], lru_b_x[l]], axis=1)
    return dict(
        ffn1=(row(ffn1_norm[l]), ffn1_w1[l].astype(BF16), ffn1_w3[l].astype(BF16), ffn1_w2[l].astype(BF16)),
        ffn2=(row(ffn2_norm[l]), ffn2_w1[l].astype(BF16), ffn2_w3[l].astype(BF16), ffn2_w2[l].astype(BF16)),
        mix_norm=row(mix_norm[l]),
        w_a=w_a.astype(BF16), w_cbd=w_cbd.astype(BF16), w_g=wi[:, c3:].astype(BF16),
        qn=row(mla_q_norm[l]), kvn=row(mla_kv_norm[l]),
        wq=wq.astype(BF16), wk=wk.astype(BF16), wv=wv.astype(BF16),
        conv_w=lru_conv_w[l], conv_b=row(lru_conv_b[l]), wg=wg.astype(BF16), bg=bg, lam=lru_lambda[l],
        sink=sink_logits[l], w_branch=w_branch[l].astype(BF16), w_out=w_out[l].astype(BF16),
    )


def _rope_slot_tables(seq):
    inv = ROPE_THETA ** (-jnp.arange(0, QK_ROPE, 2, dtype=F32) / QK_ROPE)
    ang = jnp.arange(seq, dtype=F32)[:, None] * inv[None, :]
    cos, sin = jnp.cos(ang), jnp.sin(ang)
    scale = (QK_NOPE + QK_ROPE) ** -0.5 * np.log2(np.e)
    z = lambda n: jnp.zeros((seq, n), F32)
    tail = SLOT - QK_NOPE - QK_ROPE
    cos_q = jnp.concatenate([jnp.full((seq, QK_NOPE), scale, F32), cos * scale, cos * scale, z(tail)], axis=1)
    sin_q = jnp.concatenate([z(QK_NOPE), -sin * scale, sin * scale, z(tail)], axis=1)
    cos_k = jnp.concatenate([z(QK_NOPE), cos, cos, z(tail)], axis=1)
    sin_k = jnp.concatenate([z(QK_NOPE), -sin, sin, z(tail)], axis=1)
    return cos_q, sin_q, cos_k, sin_k


def _layer(x, w, batch, seq, final_g):
    t = batch * seq
    x = _ffn(x, *w['ffn1'], final_g, False)
    *pas, h = _proj_a(x, w['mix_norm'], w['w_a'], batch, seq)
    pc, pb, pd = _proj_bcd(h, w['w_cbd'], ((C_COLS, BF16), (B_PAD_COLS, F32), (D_COLS, F32)))

    oa, la = [], []
    for pa, (window, dil) in zip(pas, DIL_CONFIGS):
        o, lse = _band_attention(
            pa, q_col=0, k_col=1, v_col=2, kv_width=A_HEADS * HEAD_DIM,
            n_heads=A_HEADS, n_kv=A_HEADS, radius=window // (2 * dil), step=dil, sink=None,
            out_dtype=F32, emit_lse=True)
        oa.append(o)
        la.append(lse)

    q, k, vt = _mla_prep(pb, w['qn'], w['kvn'], w['wq'], w['wk'], w['wv'], _rope_slot_tables(seq), batch, seq)
    yb = _mla_attn(q.reshape(batch, seq, -1), k.reshape(batch, seq, -1), vt, batch, seq)

    kvw = C_KV_HEADS * HEAD_DIM
    yc = _band_attention(
        pc.reshape(batch, 1, seq, C_COLS),
        q_col=0, k_col=C_HEADS * HEAD_DIM // kvw, v_col=C_HEADS * HEAD_DIM // kvw + 1, kv_width=kvw,
        n_heads=C_HEADS, n_kv=C_KV_HEADS, radius=C_RADIUS, step=1, sink=w['sink'],
        out_dtype=BF16, emit_lse=False)[0]

    hf, hr = _lru(pd.reshape(batch, seq, D_COLS), w['conv_w'], w['conv_b'], w['wg'], w['bg'], w['lam'])

    ya, yd = _branch_out(oa, la, hf.reshape(t, LRU_WIDTH), hr.reshape(t, LRU_WIDTH), pd, seq)
    x = _gate_merge_out(x, h, (ya, yb.reshape(t, BRANCH_W), yc.reshape(t, BRANCH_W), yd),
                        w['w_g'], w['w_branch'], w['w_out'])
    return _ffn(x, *w['ffn2'], final_g, final_g is not None and w.get('last', False))


def _trunk(x, layers, final_norm):
    batch, seq, _ = x.shape
    h = x.reshape(batch * seq, D_MODEL)
    fg = final_norm.reshape(1, -1)
    for l, w in enumerate(layers):
        h = _layer(h, dict(w, last=(l == len(layers) - 1)), batch, seq, fg)
    return h.reshape(batch, seq, D_MODEL)


def kernel(x_prompt, x_sample, ffn1_norm, ffn1_w1, ffn1_w3, ffn1_w2, mix_norm, w_in, mla_q_norm, mla_w_uq, mla_kv_norm, mla_w_ukv, lru_conv_w, lru_conv_b, lru_w_a, lru_b_a, lru_w_x, lru_b_x, lru_lambda, sink_logits, w_branch, w_out, ffn2_norm, ffn2_w1, ffn2_w3, ffn2_w2, final_norm):
    layers = [_prep_layer(l, ffn1_norm, ffn1_w1, ffn1_w3, ffn1_w2, mix_norm, w_in, mla_q_norm, mla_w_uq,
                          mla_kv_norm, mla_w_ukv, lru_conv_w, lru_conv_b, lru_w_a, lru_b_a, lru_w_x, lru_b_x,
                          lru_lambda, sink_logits, w_branch, w_out, ffn2_norm, ffn2_w1, ffn2_w3, ffn2_w2)
              for l in range(DEPTH)]
    return (_trunk(x_prompt, layers, final_norm), _trunk(x_sample, layers, final_norm))
```

```python
import functools

import numpy as np
import jax
import jax.numpy as jnp
from jax import lax
from jax.experimental import pallas as pl
from jax.experimental.pallas import tpu as pltpu

F32 = jnp.float32
BF16 = jnp.bfloat16

D_MODEL = 2048
DEPTH = 2
HEAD_DIM = 64
N_BRANCH = 4
BRANCH_W = 512
DIL_CONFIGS = ((128, 1), (512, 4), (2048, 16))
N_DIL = 3
A_HEADS = 8
B_HEADS = 8
Q_LORA = 384
KV_LORA = 128
QK_NOPE = 64
QK_ROPE = 32
V_DIM = 64
ROPE_THETA = 10000.0
C_HEADS = 8
C_KV_HEADS = 2
C_RADIUS = 128
LRU_WIDTH = 512
LRU_BLOCKS = 8
LRU_BLOCK = 64
CONV_W = 4
CONV_PAD_L = 2
LRU_C = 8.0
D_FF = 5632
EPS = 1e-6
NEG = -1e30

A_COLS = 3 * N_DIL * A_HEADS * HEAD_DIM
B_COLS = Q_LORA + KV_LORA + QK_ROPE
C_COLS = (C_HEADS + 2 * C_KV_HEADS) * HEAD_DIM
D_COLS = 2 * LRU_WIDTH
G_COLS = N_BRANCH * D_MODEL

LANES = 128
SUBLANES = 8
SLOT = LANES
B_PAD_COLS = Q_LORA + KV_LORA + SLOT
ROW_CHUNK = 128
VMEM_LIMIT = 52 * 1024 * 1024
VMEM_LIMIT_FFN = 56 * 1024 * 1024


def _cparams(*sem, vmem=VMEM_LIMIT):
    return pltpu.CompilerParams(dimension_semantics=sem, vmem_limit_bytes=vmem)


def _rms(x, g):
    return x * lax.rsqrt(jnp.mean(x * x, axis=-1, keepdims=True) + EPS) * g


def _pick(n, pref):
    t = min(n, pref)
    while n % t:
        t //= 2
    return t


def _ffn_kernel(x_ref, g_ref, w1_ref, w3_ref, w2_ref, fg_ref, o_ref, h_ref, *, final_norm):
    j = pl.program_id(1)
    last = pl.num_programs(1) - 1
    tm = x_ref.shape[0]
    halves = (slice(0, tm // 2), slice(tm // 2, tm))

    def down(h):
        a = jnp.dot(h, w1_ref[...], preferred_element_type=F32)
        b = jnp.dot(h, w3_ref[...], preferred_element_type=F32)
        act = (a * jax.nn.sigmoid(a) * b).astype(BF16)
        return jnp.dot(act, w2_ref[...], preferred_element_type=F32)

    @pl.when(j == 0)
    def _():
        for rows in halves:
            h = _rms(x_ref[rows, :], g_ref[...]).astype(BF16)
            h_ref[rows, :] = h
            o_ref[rows, :] = down(h)

    @pl.when((j > 0) & (j < last))
    def _():
        for rows in halves:
            o_ref[rows, :] += down(h_ref[rows, :])

    @pl.when(j == last)
    def _():
        for rows in halves:
            y = x_ref[rows, :] + 0.5 * (o_ref[rows, :] + down(h_ref[rows, :]))
            if final_norm:
                y = _rms(y, fg_ref[...])
            o_ref[rows, :] = y


def _ffn(x, g, w1, w3, w2, fg, final_norm):
    t = x.shape[0]
    tm, tf = _pick(t, 1024), 512
    return pl.pallas_call(
        functools.partial(_ffn_kernel, final_norm=final_norm),
        grid=(t // tm, D_FF // tf),
        in_specs=[
            pl.BlockSpec((tm, D_MODEL), lambda i, j: (i, 0)),
            pl.BlockSpec((1, D_MODEL), lambda i, j: (0, 0)),
            pl.BlockSpec((D_MODEL, tf), lambda i, j: (0, j)),
            pl.BlockSpec((D_MODEL, tf), lambda i, j: (0, j)),
            pl.BlockSpec((tf, D_MODEL), lambda i, j: (j, 0)),
            pl.BlockSpec((1, D_MODEL), lambda i, j: (0, 0)),
        ],
        out_specs=pl.BlockSpec((tm, D_MODEL), lambda i, j: (i, 0)),
        out_shape=jax.ShapeDtypeStruct((t, D_MODEL), F32),
        scratch_shapes=[pltpu.VMEM((tm, D_MODEL), BF16)],
        compiler_params=_cparams("parallel", "arbitrary", vmem=VMEM_LIMIT_FFN),
        name="ffn",
    )(x, g, w1, w3, w2, fg)


def _proj_bcd_kernel(h_ref, w_ref, *o_refs):
    r = jnp.dot(h_ref[...], w_ref[...], preferred_element_type=F32)
    c = 0
    for o_ref in o_refs:
        wd = o_ref.shape[1]
        o_ref[...] = r[:, c:c + wd].astype(o_ref.dtype)
        c += wd


def _proj_bcd(h, w, outs):
    t, n = h.shape[0], w.shape[1]
    tm = _pick(t, 512)
    return pl.pallas_call(
        _proj_bcd_kernel,
        grid=(t // tm,),
        in_specs=[pl.BlockSpec((tm, D_MODEL), lambda i: (i, 0)),
                  pl.BlockSpec((D_MODEL, n), lambda i: (0, 0))],
        out_specs=[pl.BlockSpec((tm, wd), lambda i: (i, 0)) for wd, _ in outs],
        out_shape=[jax.ShapeDtypeStruct((t, wd), dt) for wd, dt in outs],
        compiler_params=_cparams("parallel"),
        name="proj_bcd",
    )(h, w)


def _proj_a_kernel(x_ref, g_ref, w_ref, o1_ref, o2_ref, o3_ref, hout_ref, hn_ref, h_ref, *, tm):
    j = pl.program_id(1)
    chunks = D_MODEL // LANES

    @pl.when(j == 0)
    def _():
        hn = _rms(x_ref[...], g_ref[...])
        for c in range(chunks):
            hn_ref[c] = hn[:, c * LANES:(c + 1) * LANES]
        h_ref[...] = hn.astype(BF16)
        hout_ref[...] = hn.astype(BF16)

    for g, o_ref in enumerate((o1_ref, o2_ref, o3_ref)):
        dil = DIL_CONFIGS[g][1]
        n = tm // dil

        @pl.when(j == g)
        def _(o_ref=o_ref, dil=dil, n=n):
            if dil > 1:
                for r in range(dil):
                    for c in range(chunks):
                        h_ref[r * n:(r + 1) * n, c * LANES:(c + 1) * LANES] = (
                            hn_ref[c, pl.ds(r, n, stride=dil), :].astype(BF16))
            res = jnp.dot(h_ref[...], w_ref[...], preferred_element_type=F32)
            for r in range(dil):
                o_ref[0, r] = res[r * n:(r + 1) * n].astype(BF16)


def _proj_a(x, g, w, batch, seq):
    t = x.shape[0]
    tm = _pick(seq, 512)
    nseq = seq // tm
    gw = A_COLS // N_DIL
    out_specs = [pl.BlockSpec((1, dil, tm // dil, gw), lambda i, j: (i // nseq, 0, i % nseq, 0))
                 for _, dil in DIL_CONFIGS]
    out_shape = [jax.ShapeDtypeStruct((batch, dil, seq // dil, gw), BF16) for _, dil in DIL_CONFIGS]
    out_specs.append(pl.BlockSpec((tm, D_MODEL), lambda i, j: (i, 0)))
    out_shape.append(jax.ShapeDtypeStruct((t, D_MODEL), BF16))
    return pl.pallas_call(
        functools.partial(_proj_a_kernel, tm=tm),
        grid=(t // tm, N_DIL),
        in_specs=[pl.BlockSpec((tm, D_MODEL), lambda i, j: (i, 0)),
                  pl.BlockSpec((1, D_MODEL), lambda i, j: (0, 0)),
                  pl.BlockSpec((D_MODEL, gw), lambda i, j: (0, j))],
        out_specs=out_specs,
        out_shape=out_shape,
        scratch_shapes=[pltpu.VMEM((D_MODEL // LANES, tm, LANES), F32), pltpu.VMEM((tm, D_MODEL), BF16)],
        compiler_params=_cparams("parallel", "arbitrary"),
        name="proj_a",
    )(x, g, w)


def _band_kernel(*refs, n_heads, n_kv, radius, sb, tq, has_sink, emit_lse):
    if has_sink:
        sink_ref, refs = refs[0], refs[1:]
    bias_ref, q_ref, kp_ref, kc_ref, kn_ref, vp_ref, vc_ref, vn_ref = refs[:8]
    o_ref = refs[8]
    kx_ref, vx_ref, vt_ref, qt_ref = refs[-4:]
    for x_ref, (p_ref, c_ref, n_ref) in ((kx_ref, (kp_ref, kc_ref, kn_ref)), (vx_ref, (vp_ref, vc_ref, vn_ref))):
        x_ref[0:radius, :] = p_ref[0, 0, tq - radius:, :]
        x_ref[radius:radius + tq, :] = c_ref[0, 0]
        x_ref[radius + tq:, :] = n_ref[0, 0, :radius, :]
    vt_ref[...] = vx_ref[...].astype(F32).T.astype(BF16)
    qt_ref[...] = (q_ref[0, 0].astype(F32) * (HEAD_DIM ** -0.5)).T.astype(BF16)
    w = sb + 2 * radius
    rep = n_heads // n_kv
    heads = range(n_heads)
    zeros = jnp.zeros((HEAD_DIM, sb), BF16)
    nsub = tq // sb
    tile, last_tile = pl.program_id(2), pl.num_programs(2) - 1
    for u in range(nsub):
        rows = slice(u * sb, (u + 1) * sb)
        win = slice(u * sb, u * sb + w)
        kind = 0
        if u == 0:
            kind = kind + (tile == 0).astype(jnp.int32)
        if u == nsub - 1:
            kind = kind + 2 * (tile == last_tile).astype(jnp.int32)
        scores = []
        for h in heads:
            g = h // rep
            qh = qt_ref[h * HEAD_DIM:(h + 1) * HEAD_DIM, rows]
            rhs = jnp.concatenate([qh, zeros] if g % 2 == 0 else [zeros, qh], axis=0)
            pair = slice((g // 2) * 2 * HEAD_DIM, (g // 2 + 1) * 2 * HEAD_DIM)
            scores.append(jnp.dot(kx_ref[win, pair], rhs, preferred_element_type=F32) + bias_ref[kind, h])
        probs, stats = [], []
        for h in heads:
            m = jnp.max(scores[h], axis=0, keepdims=True)
            if has_sink:
                m = jnp.maximum(m, sink_ref[h])
            p = jnp.exp(scores[h] - m)
            l = jnp.sum(p, axis=0, keepdims=True)
            if has_sink:
                l = l + jnp.exp(sink_ref[h] - m)
            probs.append(p.astype(BF16))
            stats.append((m, l))
        outs, lses = [], []
        for h in heads:
            g = h // rep
            m, l = stats[h]
            outs.append(jnp.dot(vt_ref[g * HEAD_DIM:(g + 1) * HEAD_DIM, win], probs[h],
                                preferred_element_type=F32) / l)
            lses.append(jnp.broadcast_to(m + jnp.log(l), (HEAD_DIM, sb)))
        o_ref[0, 0, rows, :] = jnp.concatenate(outs, axis=0).T.astype(o_ref.dtype)
        if emit_lse:
            refs[9][0, 0, rows, :] = jnp.concatenate(lses, axis=0).T


def _alibi(n):
    return np.asarray(2.0 ** (-8.0 * np.arange(1, n + 1) / n), dtype=np.float32)


def _band_bias(n_heads, radius, step, sb):
    w = sb + 2 * radius
    shape = (4, n_heads, w, sb)
    kind, h, kj, qi = (lax.broadcasted_iota(jnp.int32, shape, d) for d in range(4))
    rel = jnp.abs(kj - radius - qi)
    kpos = kj - radius
    ok = (rel <= radius) & ((kpos >= 0) | (kind % 2 == 0)) & ((kpos < sb) | (kind < 2))
    slopes = jnp.asarray(_alibi(n_heads))[h]
    return jnp.where(ok, -slopes * (step * rel).astype(F32), NEG)


def _band_attention(arr, *, q_col, k_col, v_col, kv_width, n_heads, n_kv, radius, step, sink, out_dtype,
                    emit_lse):
    batch, n_classes, length, _ = arr.shape
    qw = n_heads * HEAD_DIM
    tq = _pick(length, 512)
    sb = min(tq, 128)
    nblk = length // tq

    def kv_spec(col, shift):
        return pl.BlockSpec((1, 1, tq, kv_width), lambda b, r, i: (b, r, jnp.clip(i + shift, 0, nblk - 1), col))

    bias = _band_bias(n_heads, radius, step, sb)
    in_specs = [pl.BlockSpec(bias.shape, lambda b, r, i: (0, 0, 0, 0)),
                pl.BlockSpec((1, 1, tq, qw), lambda b, r, i: (b, r, i, q_col)),
                kv_spec(k_col, -1), kv_spec(k_col, 0), kv_spec(k_col, 1),
                kv_spec(v_col, -1), kv_spec(v_col, 0), kv_spec(v_col, 1)]
    args = [bias] + [arr] * 7
    if sink is not None:
        in_specs = [pl.BlockSpec(memory_space=pltpu.SMEM)] + in_specs
        args = [sink] + args
    o_spec = pl.BlockSpec((1, 1, tq, qw), lambda b, r, i: (b, r, i, 0))
    o_shape = jax.ShapeDtypeStruct((batch, n_classes, length, qw), out_dtype)
    out_specs, out_shape = [o_spec], [o_shape]
    if emit_lse:
        out_specs, out_shape = [o_spec, o_spec], [o_shape, jax.ShapeDtypeStruct(o_shape.shape, F32)]
    return pl.pallas_call(
        functools.partial(_band_kernel, n_heads=n_heads, n_kv=n_kv, radius=radius, sb=sb, tq=tq,
                          has_sink=sink is not None, emit_lse=emit_lse),
        grid=(batch, n_classes, nblk),
        in_specs=in_specs,
        out_specs=out_specs,
        out_shape=out_shape,
        scratch_shapes=[pltpu.VMEM((tq + 2 * radius, kv_width), BF16), pltpu.VMEM((tq + 2 * radius, kv_width), BF16),
                        pltpu.VMEM((kv_width, tq + 2 * radius), BF16), pltpu.VMEM((qw, tq), BF16)],
        compiler_params=_cparams("parallel", "parallel", "arbitrary"),
        name="band_attn",
    )(*args)


def _rope_slot(x, cos_t, sin_t):
    lane = lax.broadcasted_iota(jnp.int32, x.shape, 1)
    first = (lane >= QK_NOPE) & (lane < QK_NOPE + QK_ROPE // 2)
    partner = jnp.where(first, pltpu.roll(x, SLOT - QK_ROPE // 2, 1), pltpu.roll(x, QK_ROPE // 2, 1))
    return x * cos_t + partner * sin_t


def _mla_prep_kernel(pb_ref, qn_ref, kvn_ref, wq_ref, wk_ref, wv_ref, cq_ref, sq_ref, ck_ref, sk_ref,
                     q_out, k_out, vt_out):
    pb = pb_ref[...]
    cq = _rms(pb[:, :Q_LORA], qn_ref[...]).astype(BF16)
    ckv = _rms(pb[:, Q_LORA:Q_LORA + KV_LORA], kvn_ref[...]).astype(BF16)
    kr = pb[:, Q_LORA + KV_LORA:]
    q = jnp.dot(cq, wq_ref[...], preferred_element_type=F32)
    k = jnp.dot(ckv, wk_ref[...], preferred_element_type=F32)
    vt_out[0] = jnp.dot(ckv, wv_ref[...], preferred_element_type=F32).T.astype(BF16)
    kr = _rope_slot(kr, ck_ref[...], sk_ref[...])
    for h in range(B_HEADS):
        sl = slice(h * SLOT, (h + 1) * SLOT)
        q_out[:, sl] = _rope_slot(q[:, sl], cq_ref[...], sq_ref[...]).astype(BF16)
        k_out[:, sl] = (k[:, sl] + kr).astype(BF16)


def _mla_prep(pb, qn, kvn, wq, wk, wv, tabs, batch, seq):
    t = pb.shape[0]
    tm = _pick(seq, 512)
    nseq = seq // tm
    row = lambda i: (i, 0)
    fixed = lambda i: (0, 0)
    tab = pl.BlockSpec((tm, SLOT), lambda i: (i % nseq, 0))
    return pl.pallas_call(
        _mla_prep_kernel,
        grid=(t // tm,),
        in_specs=[pl.BlockSpec((tm, B_PAD_COLS), row),
                  pl.BlockSpec((1, Q_LORA), fixed), pl.BlockSpec((1, KV_LORA), fixed),
                  pl.BlockSpec((Q_LORA, B_HEADS * SLOT), fixed),
                  pl.BlockSpec((KV_LORA, B_HEADS * SLOT), fixed),
                  pl.BlockSpec((KV_LORA, B_HEADS * V_DIM), fixed),
                  tab, tab, tab, tab],
        out_specs=[pl.BlockSpec((tm, B_HEADS * SLOT), row), pl.BlockSpec((tm, B_HEADS * SLOT), row),
                   pl.BlockSpec((1, B_HEADS * V_DIM, tm), lambda i: (i // nseq, 0, i % nseq))],
        out_shape=[jax.ShapeDtypeStruct((t, B_HEADS * SLOT), BF16), jax.ShapeDtypeStruct((t, B_HEADS * SLOT), BF16),
                   jax.ShapeDtypeStruct((batch, B_HEADS * V_DIM, seq), BF16)],
        compiler_params=_cparams("parallel"),
        name="mla_prep",
    )(pb, qn, kvn, wq, wk, wv, *tabs)


def _mla_attn_kernel(q_ref, k_ref, vt_ref, o_ref, st_ref, *, seq, kc):
    chunks = [slice(c * kc, (c + 1) * kc) for c in range(seq // kc)]

    def qk(h, rows):
        sl = slice(h * SLOT, (h + 1) * SLOT)
        st = lax.dot_general(k_ref[0, rows, sl], q_ref[0, :, sl], (((1,), (1,)), ((), ())),
                             preferred_element_type=F32)
        st_ref[h % 2, rows, :] = st
        return jnp.max(st, axis=0, keepdims=True)

    def pv(h, rows, m):
        p = jnp.exp2(st_ref[h % 2, rows, :] - m)
        return (jnp.sum(p, axis=0, keepdims=True),
                jnp.dot(vt_ref[0, h * V_DIM:(h + 1) * V_DIM, rows], p.astype(BF16), preferred_element_type=F32))

    def add(acc, new):
        return new if acc is None else (acc[0] + new[0], acc[1] + new[1])

    n_heads = q_ref.shape[2] // SLOT
    m = functools.reduce(jnp.maximum, [qk(0, rows) for rows in chunks])
    outs = []
    for h in range(n_heads):
        acc = m_next = None
        for rows in chunks:
            acc = add(acc, pv(h, rows, m))
            if h + 1 < n_heads:
                mc = qk(h + 1, rows)
                m_next = mc if m_next is None else jnp.maximum(m_next, mc)
        outs.append(acc[1] / acc[0])
        m = m_next
    o_ref[0] = jnp.concatenate(outs, axis=0).T.astype(o_ref.dtype)


def _mla_attn(q, k, vt, batch, seq):
    tq = _pick(seq, 512)
    kc = _pick(seq, 256)
    hg = B_HEADS // 2
    return pl.pallas_call(
        functools.partial(_mla_attn_kernel, seq=seq, kc=kc),
        scratch_shapes=[pltpu.VMEM((2, seq, tq), F32)],
        grid=(batch, B_HEADS // hg, seq // tq),
        in_specs=[pl.BlockSpec((1, tq, hg * SLOT), lambda b, g, i: (b, i, g)),
                  pl.BlockSpec((1, seq, hg * SLOT), lambda b, g, i: (b, 0, g)),
                  pl.BlockSpec((1, hg * V_DIM, seq), lambda b, g, i: (b, g, 0))],
        out_specs=pl.BlockSpec((1, tq, hg * V_DIM), lambda b, g, i: (b, i, g)),
        out_shape=jax.ShapeDtypeStruct((batch, seq, B_HEADS * V_DIM), BF16),
        compiler_params=_cparams("parallel", "parallel", "arbitrary"),
        name="mla_attn",
    )(q, k, vt)


def _sigmoid_tanh(x):
    return 0.5 * (jnp.tanh(0.5 * x) + 1.0)


def _softplus(x):
    return jnp.maximum(x, 0.0) + jnp.log1p(jnp.exp(-jnp.abs(x)))


def _lru_kernel(xf_ref, xfp_ref, xfn_ref, xr_ref, xrp_ref, xrn_ref, cw_ref, cb_ref, wg_ref, bg_ref, lam_ref,
                hf_ref, hr_ref, ext_ref, af_ref, uf_ref, ar_ref, ur_ref, cf_ref, cr_ref, *, ts):
    i = pl.program_id(1)
    nt = pl.num_programs(1)
    halo = SUBLANES

    @pl.when(i == 0)
    def _():
        cf_ref[...] = jnp.zeros_like(cf_ref)
        cr_ref[...] = jnp.zeros_like(cr_ref)

    def gates(d, tile, x_ref, xp_ref, xn_ref, a_ref, u_ref):
        ext_ref[0:halo, :] = jnp.where(tile > 0, xp_ref[0], 0.0)
        ext_ref[halo:halo + ts, :] = x_ref[0]
        ext_ref[halo + ts:, :] = jnp.where(tile < nt - 1, xn_ref[0], 0.0)
        xc = cb_ref[...]
        for tap in range(CONV_W):
            off = halo - CONV_PAD_L + tap
            xc = xc + ext_ref[off:off + ts, :] * cw_ref[tap:tap + 1, :]
        g = jnp.dot(xc.astype(BF16), wg_ref[d], preferred_element_type=F32) + bg_ref[d:d + 1, :]
        r = _sigmoid_tanh(g[:, :LRU_WIDTH])
        ig = _sigmoid_tanh(g[:, LRU_WIDTH:])
        log_a = -LRU_C * r * _softplus(-lam_ref[d:d + 1, :])
        a = jnp.exp(log_a)
        a_ref[...] = a
        u_ref[...] = jnp.sqrt(-jnp.tanh(log_a) * (1.0 + a * a)) * (ig * xc)

    gates(0, i, xf_ref, xfp_ref, xfn_ref, af_ref, uf_ref)
    gates(1, nt - 1 - i, xr_ref, xrp_ref, xrn_ref, ar_ref, ur_ref)

    ng = ts // SUBLANES
    row = lax.broadcasted_iota(jnp.int32, (SUBLANES, LRU_WIDTH), 0)

    def group_scan(a, u, carry, reverse):
        for d in (1, 2, 4):
            if reverse:
                keep, sh = row < SUBLANES - d, SUBLANES - d
            else:
                keep, sh = row >= d, d
            ap = jnp.where(keep, pltpu.roll(a, sh, 0), 1.0)
            up = jnp.where(keep, pltpu.roll(u, sh, 0), 0.0)
            u = a * up + u
            a = a * ap
        return u + a * carry

    def body(g, carry):
        cf, cr = carry
        r0 = pl.multiple_of(g * SUBLANES, SUBLANES)
        hf = group_scan(af_ref[pl.ds(r0, SUBLANES), :], uf_ref[pl.ds(r0, SUBLANES), :], cf, False)
        hf_ref[0, pl.ds(r0, SUBLANES), :] = hf
        r1 = pl.multiple_of((ng - 1 - g) * SUBLANES, SUBLANES)
        hr = group_scan(ar_ref[pl.ds(r1, SUBLANES), :], ur_ref[pl.ds(r1, SUBLANES), :], cr, True)
        hr_ref[0, pl.ds(r1, SUBLANES), :] = hr
        return (jnp.broadcast_to(hf[SUBLANES - 1:, :], hf.shape), jnp.broadcast_to(hr[:1, :], hr.shape))

    cf, cr = lax.fori_loop(0, ng, body, (cf_ref[...], cr_ref[...]))
    cf_ref[...] = cf
    cr_ref[...] = cr


def _lru(pd, cw, cb, wg, bg, lam):
    batch, seq, _ = pd.shape
    ts = _pick(seq, 512)
    nt = seq // ts
    per = ts // SUBLANES
    last8 = seq // SUBLANES - 1

    def tile_specs(tile):
        return [pl.BlockSpec((1, ts, LRU_WIDTH), lambda b, i: (b, tile(i), 0)),
                pl.BlockSpec((1, SUBLANES, LRU_WIDTH), lambda b, i: (b, jnp.maximum(tile(i) * per - 1, 0), 0)),
                pl.BlockSpec((1, SUBLANES, LRU_WIDTH), lambda b, i: (b, jnp.minimum((tile(i) + 1) * per, last8), 0))]

    fwd_tile = lambda i: i
    rev_tile = lambda i: nt - 1 - i
    fixed2 = lambda b, i: (0, 0)
    shp = jax.ShapeDtypeStruct((batch, seq, LRU_WIDTH), F32)
    tile_f32 = pltpu.VMEM((ts, LRU_WIDTH), F32)
    return pl.pallas_call(
        functools.partial(_lru_kernel, ts=ts),
        grid=(batch, nt),
        in_specs=tile_specs(fwd_tile) + tile_specs(rev_tile)
                 + [pl.BlockSpec((CONV_W, LRU_WIDTH), fixed2), pl.BlockSpec((1, LRU_WIDTH), fixed2),
                    pl.BlockSpec((2, LRU_WIDTH, 2 * LRU_WIDTH), lambda b, i: (0, 0, 0)),
                    pl.BlockSpec((2, 2 * LRU_WIDTH), fixed2), pl.BlockSpec((2, LRU_WIDTH), fixed2)],
        out_specs=[pl.BlockSpec((1, ts, LRU_WIDTH), lambda b, i: (b, i, 0)),
                   pl.BlockSpec((1, ts, LRU_WIDTH), lambda b, i: (b, nt - 1 - i, 0))],
        out_shape=[shp, shp],
        scratch_shapes=[pltpu.VMEM((ts + 2 * SUBLANES, LRU_WIDTH), F32), tile_f32, tile_f32, tile_f32, tile_f32,
                        pltpu.VMEM((SUBLANES, LRU_WIDTH), F32), pltpu.VMEM((SUBLANES, LRU_WIDTH), F32)],
        compiler_params=_cparams("parallel", "arbitrary"),
        name="lru",
    )(pd, pd, pd, pd, pd, pd, cw, cb, wg, bg, lam)


def _gelu_tanh(x):
    return 0.5 * x * (1.0 + jnp.tanh(np.sqrt(2.0 / np.pi).astype(np.float32) * (x + 0.044715 * (x * x * x))))


def _branch_out_kernel(o0, o1, o2, l0, l1, l2, hf_ref, hr_ref, gd_ref, ya_ref, yd_ref, *nat_refs, tm):
    def natural(ref, g, buf):
        dil = DIL_CONFIGS[g][1]
        if dil == 1:
            return ref[0, 0]
        chunks = BRANCH_W // LANES
        for r in range(dil):
            for c in range(chunks):
                buf[c, pl.ds(r, tm // dil, stride=dil), :] = ref[0, r, :, c * LANES:(c + 1) * LANES]
        return jnp.concatenate([buf[c] for c in range(chunks)], axis=1)

    outs = (o0[0, 0], natural(o1, 1, nat_refs[0]), natural(o2, 2, nat_refs[1]))
    lse = (l0[0, 0], natural(l1, 1, nat_refs[2]), natural(l2, 2, nat_refs[3]))
    m = jnp.maximum(jnp.maximum(lse[0], lse[1]), lse[2])
    e = [jnp.exp(v - m) for v in lse]
    tot = e[0] + e[1] + e[2]
    ya = (e[0] / tot) * outs[0] + (e[1] / tot) * outs[1] + (e[2] / tot) * outs[2]
    ya_ref[...] = ya.astype(BF16)
    yd_ref[...] = ((hf_ref[...] + hr_ref[...]) * _gelu_tanh(gd_ref[...])).astype(BF16)


def _branch_out(oa, la, hf, hr, pd, seq):
    t = hf.shape[0]
    tm = _pick(seq, 256)
    nseq = seq // tm
    blk = pl.BlockSpec((tm, BRANCH_W), lambda i: (i, 0))
    cls = [pl.BlockSpec((1, dil, tm // dil, BRANCH_W), lambda i: (i // nseq, 0, i % nseq, 0))
           for _, dil in DIL_CONFIGS]
    shp = jax.ShapeDtypeStruct((t, BRANCH_W), BF16)
    return pl.pallas_call(
        functools.partial(_branch_out_kernel, tm=tm),
        grid=(t // tm,),
        in_specs=cls + cls + [blk, blk, pl.BlockSpec((tm, BRANCH_W), lambda i: (i, 1))],
        out_specs=[blk, blk],
        out_shape=[shp, shp],
        scratch_shapes=[pltpu.VMEM((BRANCH_W // LANES, tm, LANES), F32)] * 4,
        compiler_params=_cparams("parallel"),
        name="branch_out",
    )(*oa, *la, hf, hr, pd)


def _gate_merge_out_kernel(x_ref, h_ref, ya, yb, yc, yd, g0, g1, g2, g3, wb_ref, wo_ref, o_ref):
    j = pl.program_id(1)
    last = pl.num_programs(1) - 1

    def partial_out():
        h = h_ref[...]
        merged = None
        for nbr, (y_ref, wg_ref) in enumerate(zip((ya, yb, yc, yd), (g0, g1, g2, g3))):
            gate = jax.nn.sigmoid(jnp.dot(h, wg_ref[...], preferred_element_type=F32))
            term = gate * jnp.dot(y_ref[...], wb_ref[nbr], preferred_element_type=F32)
            merged = term if merged is None else merged + term
        return jnp.dot(merged.astype(BF16), wo_ref[...], preferred_element_type=F32)

    @pl.when(j == 0)
    def _():
        o_ref[...] = partial_out()

    @pl.when((j > 0) & (j < last))
    def _():
        o_ref[...] += partial_out()

    @pl.when(j == last)
    def _():
        o_ref[...] = x_ref[...] + (o_ref[...] + partial_out())


def _gate_merge_out(x, h, ys, wg, wb, wo):
    t = x.shape[0]
    tm, tn = _pick(t, 512), 512
    nj = D_MODEL // tn
    row = lambda i, j: (i, 0)
    gate_specs = [pl.BlockSpec((D_MODEL, tn), functools.partial(lambda i, j, nbr: (0, nbr * nj + j), nbr=nbr))
                  for nbr in range(N_BRANCH)]
    return pl.pallas_call(
        _gate_merge_out_kernel,
        grid=(t // tm, nj),
        in_specs=[pl.BlockSpec((tm, D_MODEL), row), pl.BlockSpec((tm, D_MODEL), row)]
                 + [pl.BlockSpec((tm, BRANCH_W), row)] * N_BRANCH + gate_specs
                 + [pl.BlockSpec((N_BRANCH, BRANCH_W, tn), lambda i, j: (0, 0, j)),
                    pl.BlockSpec((tn, D_MODEL), lambda i, j: (j, 0))],
        out_specs=pl.BlockSpec((tm, D_MODEL), row),
        out_shape=jax.ShapeDtypeStruct((t, D_MODEL), F32),
        compiler_params=_cparams("parallel", "arbitrary"),
        name="gate_merge_out",
    )(x, h, *ys, wg, wg, wg, wg, wb, wo)


def _block_diag(w):
    eye = jnp.eye(LRU_BLOCKS, dtype=w.dtype)
    return jnp.einsum('nef,nm->nemf', w, eye).reshape(LRU_WIDTH, LRU_WIDTH)


def _prep_layer(l, ffn1_norm, ffn1_w1, ffn1_w3, ffn1_w2, mix_norm, w_in, mla_q_norm, mla_w_uq, mla_kv_norm,
                mla_w_ukv, lru_conv_w, lru_conv_b, lru_w_a, lru_b_a, lru_w_x, lru_b_x, lru_lambda, sink_logits,
                w_branch, w_out, ffn2_norm, ffn2_w1, ffn2_w3, ffn2_w2):
    row = lambda v: v.reshape(1, -1)
    wi = w_in[l]
    c0, c1, c2, c3 = A_COLS, A_COLS + B_COLS, A_COLS + B_COLS + C_COLS, A_COLS + B_COLS + C_COLS + D_COLS
    w_a = wi[:, :c0].reshape(D_MODEL, 3, N_DIL, A_HEADS * HEAD_DIM).transpose(0, 2, 1, 3).reshape(D_MODEL, A_COLS)
    wb = wi[:, c0:c1]
    zeros = lambda n: jnp.zeros((D_MODEL, n), wi.dtype)
    w_b = jnp.concatenate([wb[:, :Q_LORA + KV_LORA], zeros(QK_NOPE), wb[:, Q_LORA + KV_LORA:],
                           zeros(SLOT - QK_NOPE - QK_ROPE)], axis=1)
    w_cbd = jnp.concatenate([wi[:, c1:c2], w_b, wi[:, c2:c3]], axis=1)
    wq = mla_w_uq[l].reshape(Q_LORA, B_HEADS, QK_NOPE + QK_ROPE)
    wq = jnp.pad(wq, ((0, 0), (0, 0), (0, SLOT - QK_NOPE - QK_ROPE))).reshape(Q_LORA, B_HEADS * SLOT)
    wkv = mla_w_ukv[l].reshape(KV_LORA, B_HEADS, QK_NOPE + V_DIM)
    wk = jnp.pad(wkv[:, :, :QK_NOPE], ((0, 0), (0, 0), (0, SLOT - QK_NOPE))).reshape(KV_LORA, B_HEADS * SLOT)
    wv = wkv[:, :, QK_NOPE:].reshape(KV_LORA, B_HEADS * V_DIM)
    wg = jnp.stack([jnp.concatenate([_block_diag(lru_w_a[l, d]), _block_diag(lru_w_x[l, d])], axis=1)
                    for d in range(2)])
    bg = jnp.concatenate([lru_b_a[l], lru_b_x[l]], axis=1)
    return dict(
        ffn1=(row(ffn1_norm[l]), ffn1_w1[l].astype(BF16), ffn1_w3[l].astype(BF16), ffn1_w2[l].astype(BF16)),
        ffn2=(row(ffn2_norm[l]), ffn2_w1[l].astype(BF16), ffn2_w3[l].astype(BF16), ffn2_w2[l].astype(BF16)),
        mix_norm=row(mix_norm[l]),
        w_a=w_a.astype(BF16), w_cbd=w_cbd.astype(BF16), w_g=wi[:, c3:].astype(BF16),
        qn=row(mla_q_norm[l]), kvn=row(mla_kv_norm[l]),
        wq=wq.astype(BF16), wk=wk.astype(BF16), wv=wv.astype(BF16),
        conv_w=lru_conv_w[l], conv_b=row(lru_conv_b[l]), wg=wg.astype(BF16), bg=bg, lam=lru_lambda[l],
        sink=sink_logits[l], w_branch=w_branch[l].astype(BF16), w_out=w_out[l].astype(BF16),
    )


def _rope_slot_tables(seq):
    inv = ROPE_THETA ** (-jnp.arange(0, QK_ROPE, 2, dtype=F32) / QK_ROPE)
    ang = jnp.arange(seq, dtype=F32)[:, None] * inv[None, :]
    cos, sin = jnp.cos(ang), jnp.sin(ang)
    scale = (QK_NOPE + QK_ROPE) ** -0.5 * np.log2(np.e)
    z = lambda n: jnp.zeros((seq, n), F32)
    tail = SLOT - QK_NOPE - QK_ROPE
    cos_q = jnp.concatenate([jnp.full((seq, QK_NOPE), scale, F32), cos * scale, cos * scale, z(tail)], axis=1)
    sin_q = jnp.concatenate([z(QK_NOPE), -sin * scale, sin * scale, z(tail)], axis=1)
    cos_k = jnp.concatenate([z(QK_NOPE), cos, cos, z(tail)], axis=1)
    sin_k = jnp.concatenate([z(QK_NOPE), -sin, sin, z(tail)], axis=1)
    return cos_q, sin_q, cos_k, sin_k


def _layer(x, w, batch, seq, final_g):
    t = batch * seq
    x = _ffn(x, *w['ffn1'], final_g, False)
    *pas, h = _proj_a(x, w['mix_norm'], w['w_a'], batch, seq)
    pc, pb, pd = _proj_bcd(h, w['w_cbd'], ((C_COLS, BF16), (B_PAD_COLS, F32), (D_COLS, F32)))

    oa, la = [], []
    for pa, (window, dil) in zip(pas, DIL_CONFIGS):
        o, lse = _band_attention(
            pa, q_col=0, k_col=1, v_col=2, kv_width=A_HEADS * HEAD_DIM,
            n_heads=A_HEADS, n_kv=A_HEADS, radius=window // (2 * dil), step=dil, sink=None,
            out_dtype=F32, emit_lse=True)
        oa.append(o)
        la.append(lse)

    q, k, vt = _mla_prep(pb, w['qn'], w['kvn'], w['wq'], w['wk'], w['wv'], _rope_slot_tables(seq), batch, seq)
    yb = _mla_attn(q.reshape(batch, seq, -1), k.reshape(batch, seq, -1), vt, batch, seq)

    kvw = C_KV_HEADS * HEAD_DIM
    yc = _band_attention(
        pc.reshape(batch, 1, seq, C_COLS),
        q_col=0, k_col=C_HEADS * HEAD_DIM // kvw, v_col=C_HEADS * HEAD_DIM // kvw + 1, kv_width=kvw,
        n_heads=C_HEADS, n_kv=C_KV_HEADS, radius=C_RADIUS, step=1, sink=w['sink'],
        out_dtype=BF16, emit_lse=False)[0]

    hf, hr = _lru(pd.reshape(batch, seq, D_COLS), w['conv_w'], w['conv_b'], w['wg'], w['bg'], w['lam'])

    ya, yd = _branch_out(oa, la, hf.reshape(t, LRU_WIDTH), hr.reshape(t, LRU_WIDTH), pd, seq)
    x = _gate_merge_out(x, h, (ya, yb.reshape(t, BRANCH_W), yc.reshape(t, BRANCH_W), yd),
                        w['w_g'], w['w_branch'], w['w_out'])
    return _ffn(x, *w['ffn2'], final_g, final_g is not None and w.get('last', False))


def _trunk(x, layers, final_norm):
    batch, seq, _ = x.shape
    h = x.reshape(batch * seq, D_MODEL)
    fg = final_norm.reshape(1, -1)
    for l, w in enumerate(layers):
        h = _layer(h, dict(w, last=(l == len(layers) - 1)), batch, seq, fg)
    return h.reshape(batch, seq, D_MODEL)


def kernel(x_prompt, x_sample, ffn1_norm, ffn1_w1, ffn1_w3, ffn1_w2, mix_norm, w_in, mla_q_norm, mla_w_uq, mla_kv_norm, mla_w_ukv, lru_conv_w, lru_conv_b, lru_w_a, lru_b_a, lru_w_x, lru_b_x, lru_lambda, sink_logits, w_branch, w_out, ffn2_norm, ffn2_w1, ffn2_w3, ffn2_w2, final_norm):
    layers = [_prep_layer(l, ffn1_norm, ffn1_w1, ffn1_w3, ffn1_w2, mix_norm, w_in, mla_q_norm, mla_w_uq,
                          mla_kv_norm, mla_w_ukv, lru_conv_w, lru_conv_b, lru_w_a, lru_b_a, lru_w_x, lru_b_x,
                          lru_lambda, sink_logits, w_branch, w_out, ffn2_norm, ffn2_w1, ffn2_w3, ffn2_w2)
              for l in range(DEPTH)]
    return (_trunk(x_prompt, layers, final_norm), _trunk(x_sample, layers, final_norm))
```

```python
import functools

import numpy as np
import jax
import jax.numpy as jnp
from jax import lax
from jax.experimental import pallas as pl
from jax.experimental.pallas import tpu as pltpu

F32 = jnp.float32
BF16 = jnp.bfloat16

D_MODEL = 2048
DEPTH = 2
HEAD_DIM = 64
N_BRANCH = 4
BRANCH_W = 512
DIL_CONFIGS = ((128, 1), (512, 4), (2048, 16))
N_DIL = 3
A_HEADS = 8
B_HEADS = 8
Q_LORA = 384
KV_LORA = 128
QK_NOPE = 64
QK_ROPE = 32
V_DIM = 64
ROPE_THETA = 10000.0
C_HEADS = 8
C_KV_HEADS = 2
C_RADIUS = 128
LRU_WIDTH = 512
LRU_BLOCKS = 8
LRU_BLOCK = 64
CONV_W = 4
CONV_PAD_L = 2
LRU_C = 8.0
D_FF = 5632
EPS = 1e-6
NEG = -1e30

A_COLS = 3 * N_DIL * A_HEADS * HEAD_DIM
B_COLS = Q_LORA + KV_LORA + QK_ROPE
C_COLS = (C_HEADS + 2 * C_KV_HEADS) * HEAD_DIM
D_COLS = 2 * LRU_WIDTH
G_COLS = N_BRANCH * D_MODEL

LANES = 128
SUBLANES = 8
SLOT = LANES
B_PAD_COLS = Q_LORA + KV_LORA + SLOT
ROW_CHUNK = 128
VMEM_LIMIT = 52 * 1024 * 1024
VMEM_LIMIT_FFN = 56 * 1024 * 1024


def _cparams(*sem, vmem=VMEM_LIMIT):
    return pltpu.CompilerParams(dimension_semantics=sem, vmem_limit_bytes=vmem)


def _rms(x, g):
    return x * lax.rsqrt(jnp.mean(x * x, axis=-1, keepdims=True) + EPS) * g


def _pick(n, pref):
    t = min(n, pref)
    while n % t:
        t //= 2
    return t


def _ffn_kernel(x_ref, g_ref, w1_ref, w3_ref, w2_ref, fg_ref, o_ref, h_ref, *, final_norm):
    j = pl.program_id(1)
    last = pl.num_programs(1) - 1
    tm = x_ref.shape[0]
    halves = (slice(0, tm // 2), slice(tm // 2, tm))

    def down(h):
        a = jnp.dot(h, w1_ref[...], preferred_element_type=F32)
        b = jnp.dot(h, w3_ref[...], preferred_element_type=F32)
        act = (a * jax.nn.sigmoid(a) * b).astype(BF16)
        return jnp.dot(act, w2_ref[...], preferred_element_type=F32)

    @pl.when(j == 0)
    def _():
        for rows in halves:
            h = _rms(x_ref[rows, :], g_ref[...]).astype(BF16)
            h_ref[rows, :] = h
            o_ref[rows, :] = down(h)

    @pl.when((j > 0) & (j < last))
    def _():
        for rows in halves:
            o_ref[rows, :] += down(h_ref[rows, :])

    @pl.when(j == last)
    def _():
        for rows in halves:
            y = x_ref[rows, :] + 0.5 * (o_ref[rows, :] + down(h_ref[rows, :]))
            if final_norm:
                y = _rms(y, fg_ref[...])
            o_ref[rows, :] = y


def _ffn(x, g, w1, w3, w2, fg, final_norm):
    t = x.shape[0]
    tm, tf = _pick(t, 1024), 512
    return pl.pallas_call(
        functools.partial(_ffn_kernel, final_norm=final_norm),
        grid=(t // tm, D_FF // tf),
        in_specs=[
            pl.BlockSpec((tm, D_MODEL), lambda i, j: (i, 0)),
            pl.BlockSpec((1, D_MODEL), lambda i, j: (0, 0)),
            pl.BlockSpec((D_MODEL, tf), lambda i, j: (0, j)),
            pl.BlockSpec((D_MODEL, tf), lambda i, j: (0, j)),
            pl.BlockSpec((tf, D_MODEL), lambda i, j: (j, 0)),
            pl.BlockSpec((1, D_MODEL), lambda i, j: (0, 0)),
        ],
        out_specs=pl.BlockSpec((tm, D_MODEL), lambda i, j: (i, 0)),
        out_shape=jax.ShapeDtypeStruct((t, D_MODEL), F32),
        scratch_shapes=[pltpu.VMEM((tm, D_MODEL), BF16)],
        compiler_params=_cparams("parallel", "arbitrary", vmem=VMEM_LIMIT_FFN),
        name="ffn",
    )(x, g, w1, w3, w2, fg)


def _proj_bcd_kernel(h_ref, w_ref, *o_refs):
    r = jnp.dot(h_ref[...], w_ref[...], preferred_element_type=F32)
    c = 0
    for o_ref in o_refs:
        wd = o_ref.shape[1]
        o_ref[...] = r[:, c:c + wd].astype(o_ref.dtype)
        c += wd


def _proj_bcd(h, w, outs):
    t, n = h.shape[0], w.shape[1]
    tm = _pick(t, 512)
    return pl.pallas_call(
        _proj_bcd_kernel,
        grid=(t // tm,),
        in_specs=[pl.BlockSpec((tm, D_MODEL), lambda i: (i, 0)),
                  pl.BlockSpec((D_MODEL, n), lambda i: (0, 0))],
        out_specs=[pl.BlockSpec((tm, wd), lambda i: (i, 0)) for wd, _ in outs],
        out_shape=[jax.ShapeDtypeStruct((t, wd), dt) for wd, dt in outs],
        compiler_params=_cparams("parallel"),
        name="proj_bcd",
    )(h, w)


def _proj_a_kernel(x_ref, g_ref, w_ref, o1_ref, o2_ref, o3_ref, hout_ref, hn_ref, h_ref, *, tm):
    j = pl.program_id(1)
    chunks = D_MODEL // LANES

    @pl.when(j == 0)
    def _():
        hn = _rms(x_ref[...], g_ref[...])
        for c in range(chunks):
            hn_ref[c] = hn[:, c * LANES:(c + 1) * LANES]
        h_ref[...] = hn.astype(BF16)
        hout_ref[...] = hn.astype(BF16)

    for g, o_ref in enumerate((o1_ref, o2_ref, o3_ref)):
        dil = DIL_CONFIGS[g][1]
        n = tm // dil

        @pl.when(j == g)
        def _(o_ref=o_ref, dil=dil, n=n):
            if dil > 1:
                for r in range(dil):
                    for c in range(chunks):
                        h_ref[r * n:(r + 1) * n, c * LANES:(c + 1) * LANES] = (
                            hn_ref[c, pl.ds(r, n, stride=dil), :].astype(BF16))
            res = jnp.dot(h_ref[...], w_ref[...], preferred_element_type=F32)
            for r in range(dil):
                o_ref[0, r] = res[r * n:(r + 1) * n].astype(BF16)


def _proj_a(x, g, w, batch, seq):
    t = x.shape[0]
    tm = _pick(seq, 512)
    nseq = seq // tm
    gw = A_COLS // N_DIL
    out_specs = [pl.BlockSpec((1, dil, tm // dil, gw), lambda i, j: (i // nseq, 0, i % nseq, 0))
                 for _, dil in DIL_CONFIGS]
    out_shape = [jax.ShapeDtypeStruct((batch, dil, seq // dil, gw), BF16) for _, dil in DIL_CONFIGS]
    out_specs.append(pl.BlockSpec((tm, D_MODEL), lambda i, j: (i, 0)))
    out_shape.append(jax.ShapeDtypeStruct((t, D_MODEL), BF16))
    return pl.pallas_call(
        functools.partial(_proj_a_kernel, tm=tm),
        grid=(t // tm, N_DIL),
        in_specs=[pl.BlockSpec((tm, D_MODEL), lambda i, j: (i, 0)),
                  pl.BlockSpec((1, D_MODEL), lambda i, j: (0, 0)),
                  pl.BlockSpec((D_MODEL, gw), lambda i, j: (0, j))],
        out_specs=out_specs,
        out_shape=out_shape,
        scratch_shapes=[pltpu.VMEM((D_MODEL // LANES, tm, LANES), F32), pltpu.VMEM((tm, D_MODEL), BF16)],
        compiler_params=_cparams("parallel", "arbitrary"),
        name="proj_a",
    )(x, g, w)


def _band_kernel(*refs, n_heads, n_kv, radius, sb, tq, has_sink, emit_lse):
    if has_sink:
        sink_ref, refs = refs[0], refs[1:]
    bias_ref, q_ref, kp_ref, kc_ref, kn_ref, vp_ref, vc_ref, vn_ref = refs[:8]
    o_ref = refs[8]
    kx_ref, vx_ref, vt_ref, qt_ref = refs[-4:]
    for x_ref, (p_ref, c_ref, n_ref) in ((kx_ref, (kp_ref, kc_ref, kn_ref)), (vx_ref, (vp_ref, vc_ref, vn_ref))):
        x_ref[0:radius, :] = p_ref[0, 0]
        x_ref[radius:radius + tq, :] = c_ref[0, 0]
        x_ref[radius + tq:, :] = n_ref[0, 0]
    vt_ref[...] = vx_ref[...].astype(F32).T.astype(BF16)
    qt_ref[...] = (q_ref[0, 0].astype(F32) * (HEAD_DIM ** -0.5)).T.astype(BF16)
    w = sb + 2 * radius
    rep = n_heads // n_kv
    heads = range(n_heads)
    zeros = jnp.zeros((HEAD_DIM, sb), BF16)
    nsub = tq // sb
    tile, last_tile = pl.program_id(2), pl.num_programs(2) - 1
    for u in range(nsub):
        rows = slice(u * sb, (u + 1) * sb)
        win = slice(u * sb, u * sb + w)
        kind = 0
        if u == 0:
            kind = kind + (tile == 0).astype(jnp.int32)
        if u == nsub - 1:
            kind = kind + 2 * (tile == last_tile).astype(jnp.int32)
        scores = []
        for h in heads:
            g = h // rep
            qh = qt_ref[h * HEAD_DIM:(h + 1) * HEAD_DIM, rows]
            rhs = jnp.concatenate([qh, zeros] if g % 2 == 0 else [zeros, qh], axis=0)
            pair = slice((g // 2) * 2 * HEAD_DIM, (g // 2 + 1) * 2 * HEAD_DIM)
            scores.append(jnp.dot(kx_ref[win, pair], rhs, preferred_element_type=F32) + bias_ref[kind, h])
        probs, stats = [], []
        for h in heads:
            m = jnp.max(scores[h], axis=0, keepdims=True)
            if has_sink:
                m = jnp.maximum(m, sink_ref[h])
            p = jnp.exp(scores[h] - m)
            l = jnp.sum(p, axis=0, keepdims=True)
            if has_sink:
                l = l + jnp.exp(sink_ref[h] - m)
            probs.append(p.astype(BF16))
            stats.append((m, l))
        outs, lses = [], []
        for h in heads:
            g = h // rep
            m, l = stats[h]
            outs.append(jnp.dot(vt_ref[g * HEAD_DIM:(g + 1) * HEAD_DIM, win], probs[h],
                                preferred_element_type=F32) / l)
            lses.append(jnp.broadcast_to(m + jnp.log(l), (HEAD_DIM, sb)))
        o_ref[0, 0, rows, :] = jnp.concatenate(outs, axis=0).T.astype(o_ref.dtype)
        if emit_lse:
            refs[9][0, 0, rows, :] = jnp.concatenate(lses, axis=0).T


def _alibi(n):
    return np.asarray(2.0 ** (-8.0 * np.arange(1, n + 1) / n), dtype=np.float32)


def _band_bias(n_heads, radius, step, sb):
    w = sb + 2 * radius
    shape = (4, n_heads, w, sb)
    kind, h, kj, qi = (lax.broadcasted_iota(jnp.int32, shape, d) for d in range(4))
    rel = jnp.abs(kj - radius - qi)
    kpos = kj - radius
    ok = (rel <= radius) & ((kpos >= 0) | (kind % 2 == 0)) & ((kpos < sb) | (kind < 2))
    slopes = jnp.asarray(_alibi(n_heads))[h]
    return jnp.where(ok, -slopes * (step * rel).astype(F32), NEG)


def _band_attention(arr, *, q_col, k_col, v_col, kv_width, n_heads, n_kv, radius, step, sink, out_dtype,
                    emit_lse):
    batch, n_classes, length, _ = arr.shape
    qw = n_heads * HEAD_DIM
    tq = _pick(length, 512)
    sb = min(tq, 128)
    nblk = length // tq

    per = tq // radius
    last_halo = length // radius - 1

    def kv_spec(col, shift):
        if shift == 0:
            return pl.BlockSpec((1, 1, tq, kv_width), lambda b, r, i: (b, r, i, col))
        if shift < 0:
            return pl.BlockSpec((1, 1, radius, kv_width), lambda b, r, i: (b, r, jnp.maximum(i * per - 1, 0), col))
        return pl.BlockSpec((1, 1, radius, kv_width),
                            lambda b, r, i: (b, r, jnp.minimum((i + 1) * per, last_halo), col))

    bias = _band_bias(n_heads, radius, step, sb)
    in_specs = [pl.BlockSpec(bias.shape, lambda b, r, i: (0, 0, 0, 0)),
                pl.BlockSpec((1, 1, tq, qw), lambda b, r, i: (b, r, i, q_col)),
                kv_spec(k_col, -1), kv_spec(k_col, 0), kv_spec(k_col, 1),
                kv_spec(v_col, -1), kv_spec(v_col, 0), kv_spec(v_col, 1)]
    args = [bias] + [arr] * 7
    if sink is not None:
        in_specs = [pl.BlockSpec(memory_space=pltpu.SMEM)] + in_specs
        args = [sink] + args
    o_spec = pl.BlockSpec((1, 1, tq, qw), lambda b, r, i: (b, r, i, 0))
    o_shape = jax.ShapeDtypeStruct((batch, n_classes, length, qw), out_dtype)
    out_specs, out_shape = [o_spec], [o_shape]
    if emit_lse:
        out_specs, out_shape = [o_spec, o_spec], [o_shape, jax.ShapeDtypeStruct(o_shape.shape, F32)]
    return pl.pallas_call(
        functools.partial(_band_kernel, n_heads=n_heads, n_kv=n_kv, radius=radius, sb=sb, tq=tq,
                          has_sink=sink is not None, emit_lse=emit_lse),
        grid=(batch, n_classes, nblk),
        in_specs=in_specs,
        out_specs=out_specs,
        out_shape=out_shape,
        scratch_shapes=[pltpu.VMEM((tq + 2 * radius, kv_width), BF16), pltpu.VMEM((tq + 2 * radius, kv_width), BF16),
                        pltpu.VMEM((kv_width, tq + 2 * radius), BF16), pltpu.VMEM((qw, tq), BF16)],
        compiler_params=_cparams("parallel", "parallel", "arbitrary"),
        name="band_attn",
    )(*args)


def _rope_slot(x, cos_t, sin_t):
    lane = lax.broadcasted_iota(jnp.int32, x.shape, 1)
    first = (lane >= QK_NOPE) & (lane < QK_NOPE + QK_ROPE // 2)
    partner = jnp.where(first, pltpu.roll(x, SLOT - QK_ROPE // 2, 1), pltpu.roll(x, QK_ROPE // 2, 1))
    return x * cos_t + partner * sin_t


def _mla_prep_kernel(pb_ref, qn_ref, kvn_ref, wq_ref, wk_ref, wv_ref, cq_ref, sq_ref, ck_ref, sk_ref,
                     q_out, k_out, vt_out):
    pb = pb_ref[...]
    cq = _rms(pb[:, :Q_LORA], qn_ref[...]).astype(BF16)
    ckv = _rms(pb[:, Q_LORA:Q_LORA + KV_LORA], kvn_ref[...]).astype(BF16)
    kr = pb[:, Q_LORA + KV_LORA:]
    q = jnp.dot(cq, wq_ref[...], preferred_element_type=F32)
    k = jnp.dot(ckv, wk_ref[...], preferred_element_type=F32)
    vt_out[0] = jnp.dot(ckv, wv_ref[...], preferred_element_type=F32).T.astype(BF16)
    kr = _rope_slot(kr, ck_ref[...], sk_ref[...])
    for h in range(B_HEADS):
        sl = slice(h * SLOT, (h + 1) * SLOT)
        q_out[:, sl] = _rope_slot(q[:, sl], cq_ref[...], sq_ref[...]).astype(BF16)
        k_out[:, sl] = (k[:, sl] + kr).astype(BF16)


def _mla_prep(pb, qn, kvn, wq, wk, wv, tabs, batch, seq):
    t = pb.shape[0]
    tm = _pick(seq, 512)
    nseq = seq // tm
    row = lambda i: (i, 0)
    fixed = lambda i: (0, 0)
    tab = pl.BlockSpec((tm, SLOT), lambda i: (i % nseq, 0))
    return pl.pallas_call(
        _mla_prep_kernel,
        grid=(t // tm,),
        in_specs=[pl.BlockSpec((tm, B_PAD_COLS), row),
                  pl.BlockSpec((1, Q_LORA), fixed), pl.BlockSpec((1, KV_LORA), fixed),
                  pl.BlockSpec((Q_LORA, B_HEADS * SLOT), fixed),
                  pl.BlockSpec((KV_LORA, B_HEADS * SLOT), fixed),
                  pl.BlockSpec((KV_LORA, B_HEADS * V_DIM), fixed),
                  tab, tab, tab, tab],
        out_specs=[pl.BlockSpec((tm, B_HEADS * SLOT), row), pl.BlockSpec((tm, B_HEADS * SLOT), row),
                   pl.BlockSpec((1, B_HEADS * V_DIM, tm), lambda i: (i // nseq, 0, i % nseq))],
        out_shape=[jax.ShapeDtypeStruct((t, B_HEADS * SLOT), BF16), jax.ShapeDtypeStruct((t, B_HEADS * SLOT), BF16),
                   jax.ShapeDtypeStruct((batch, B_HEADS * V_DIM, seq), BF16)],
        compiler_params=_cparams("parallel"),
        name="mla_prep",
    )(pb, qn, kvn, wq, wk, wv, *tabs)


def _mla_attn_kernel(q_ref, k_ref, vt_ref, o_ref, st_ref, *, seq, kc):
    chunks = [slice(c * kc, (c + 1) * kc) for c in range(seq // kc)]

    def qk(h, rows):
        sl = slice(h * SLOT, (h + 1) * SLOT)
        st = lax.dot_general(k_ref[0, rows, sl], q_ref[0, :, sl], (((1,), (1,)), ((), ())),
                             preferred_element_type=F32)
        st_ref[h % 2, rows, :] = st
        return jnp.max(st, axis=0, keepdims=True)

    def pv(h, rows, m):
        p = jnp.exp2(st_ref[h % 2, rows, :] - m)
        return (jnp.sum(p, axis=0, keepdims=True),
                jnp.dot(vt_ref[0, h * V_DIM:(h + 1) * V_DIM, rows], p.astype(BF16), preferred_element_type=F32))

    def add(acc, new):
        return new if acc is None else (acc[0] + new[0], acc[1] + new[1])

    n_heads = q_ref.shape[2] // SLOT
    m = functools.reduce(jnp.maximum, [qk(0, rows) for rows in chunks])
    outs = []
    for h in range(n_heads):
        acc = m_next = None
        for rows in chunks:
            acc = add(acc, pv(h, rows, m))
            if h + 1 < n_heads:
                mc = qk(h + 1, rows)
                m_next = mc if m_next is None else jnp.maximum(m_next, mc)
        outs.append(acc[1] / acc[0])
        m = m_next
    o_ref[0] = jnp.concatenate(outs, axis=0).T.astype(o_ref.dtype)


def _mla_attn(q, k, vt, batch, seq):
    tq = _pick(seq, 512)
    kc = _pick(seq, 256)
    hg = B_HEADS // 2
    return pl.pallas_call(
        functools.partial(_mla_attn_kernel, seq=seq, kc=kc),
        scratch_shapes=[pltpu.VMEM((2, seq, tq), F32)],
        grid=(batch, B_HEADS // hg, seq // tq),
        in_specs=[pl.BlockSpec((1, tq, hg * SLOT), lambda b, g, i: (b, i, g)),
                  pl.BlockSpec((1, seq, hg * SLOT), lambda b, g, i: (b, 0, g)),
                  pl.BlockSpec((1, hg * V_DIM, seq), lambda b, g, i: (b, g, 0))],
        out_specs=pl.BlockSpec((1, tq, hg * V_DIM), lambda b, g, i: (b, i, g)),
        out_shape=jax.ShapeDtypeStruct((batch, seq, B_HEADS * V_DIM), BF16),
        compiler_params=_cparams("parallel", "parallel", "arbitrary"),
        name="mla_attn",
    )(q, k, vt)


def _sigmoid_tanh(x):
    return 0.5 * (jnp.tanh(0.5 * x) + 1.0)


def _softplus(x):
    return jnp.maximum(x, 0.0) + jnp.log1p(jnp.exp(-jnp.abs(x)))


def _lru_kernel(xf_ref, xfp_ref, xfn_ref, xr_ref, xrp_ref, xrn_ref, cw_ref, cb_ref, wg_ref, bg_ref, lam_ref,
                hf_ref, hr_ref, ext_ref, af_ref, uf_ref, ar_ref, ur_ref, hfs_ref, hrs_ref, cf_ref, cr_ref,
                *, ts, pitch):
    i = pl.program_id(0)
    nt = pl.num_programs(0)
    halo = SUBLANES
    nb = xf_ref.shape[0]
    slabs = LRU_WIDTH // LANES

    @pl.when(i == 0)
    def _():
        cf_ref[...] = jnp.zeros_like(cf_ref)
        cr_ref[...] = jnp.zeros_like(cr_ref)

    def gates(d, tile, x_ref, xp_ref, xn_ref, a_ref, u_ref):
        def one_batch(b, carry):
            ext_ref[0:halo, :] = jnp.where(tile > 0, xp_ref[b], 0.0)
            ext_ref[halo:halo + ts, :] = x_ref[b]
            ext_ref[halo + ts:, :] = jnp.where(tile < nt - 1, xn_ref[b], 0.0)
            xc = cb_ref[...]
            for tap in range(CONV_W):
                off = halo - CONV_PAD_L + tap
                xc = xc + ext_ref[off:off + ts, :] * cw_ref[tap:tap + 1, :]
            g = jnp.dot(xc.astype(BF16), wg_ref[d], preferred_element_type=F32) + bg_ref[d:d + 1, :]
            r = _sigmoid_tanh(g[:, :LRU_WIDTH])
            ig = _sigmoid_tanh(g[:, LRU_WIDTH:])
            log_a = -LRU_C * r * _softplus(-lam_ref[d:d + 1, :])
            a = jnp.exp(log_a)
            u = jnp.sqrt(-jnp.tanh(log_a) * (1.0 + a * a)) * (ig * xc)
            rows = pl.ds(pl.multiple_of(b * pitch, SUBLANES), ts)
            for c in range(slabs):
                a_ref[c, rows, :] = a[:, c * LANES:(c + 1) * LANES]
                u_ref[c, rows, :] = u[:, c * LANES:(c + 1) * LANES]
            return carry
        lax.fori_loop(0, nb, one_batch, 0)

    gates(0, i, xf_ref, xfp_ref, xfn_ref, af_ref, uf_ref)
    gates(1, nt - 1 - i, xr_ref, xrp_ref, xrn_ref, ar_ref, ur_ref)

    def time_step(t, carry):
        hf, hr = carry
        fwd = pl.ds(t, nb, stride=pitch)
        rev = pl.ds(ts - 1 - t, nb, stride=pitch)
        new_f, new_r = [], []
        for c in range(slabs):
            h = af_ref[c, fwd, :] * hf[c] + uf_ref[c, fwd, :]
            hfs_ref[c, fwd, :] = h
            new_f.append(h)
            h = ar_ref[c, rev, :] * hr[c] + ur_ref[c, rev, :]
            hrs_ref[c, rev, :] = h
            new_r.append(h)
        return tuple(new_f), tuple(new_r)

    init = (tuple(cf_ref[c] for c in range(slabs)), tuple(cr_ref[c] for c in range(slabs)))
    hf, hr = lax.fori_loop(0, ts, time_step, init, unroll=8)
    for c in range(slabs):
        cf_ref[c] = hf[c]
        cr_ref[c] = hr[c]
        for b in range(nb):
            hf_ref[b, :, c * LANES:(c + 1) * LANES] = hfs_ref[c, b * pitch:b * pitch + ts, :]
            hr_ref[b, :, c * LANES:(c + 1) * LANES] = hrs_ref[c, b * pitch:b * pitch + ts, :]


def _lru(pd, cw, cb, wg, bg, lam):
    batch, seq, _ = pd.shape
    ts = _pick(seq, 128)
    nt = seq // ts
    per = ts // SUBLANES
    last8 = seq // SUBLANES - 1
    pitch = ts + SUBLANES

    def tile_specs(tile):
        return [pl.BlockSpec((batch, ts, LRU_WIDTH), lambda i: (0, tile(i), 0)),
                pl.BlockSpec((batch, SUBLANES, LRU_WIDTH), lambda i: (0, jnp.maximum(tile(i) * per - 1, 0), 0)),
                pl.BlockSpec((batch, SUBLANES, LRU_WIDTH), lambda i: (0, jnp.minimum((tile(i) + 1) * per, last8), 0))]

    fwd_tile = lambda i: i
    rev_tile = lambda i: nt - 1 - i
    fixed2 = lambda i: (0, 0)
    shp = jax.ShapeDtypeStruct((batch, seq, LRU_WIDTH), F32)
    slab = pltpu.VMEM((LRU_WIDTH // LANES, batch * pitch, LANES), F32)
    carry = pltpu.VMEM((LRU_WIDTH // LANES, batch, LANES), F32)
    return pl.pallas_call(
        functools.partial(_lru_kernel, ts=ts, pitch=pitch),
        grid=(nt,),
        in_specs=tile_specs(fwd_tile) + tile_specs(rev_tile)
                 + [pl.BlockSpec((CONV_W, LRU_WIDTH), fixed2), pl.BlockSpec((1, LRU_WIDTH), fixed2),
                    pl.BlockSpec((2, LRU_WIDTH, 2 * LRU_WIDTH), lambda i: (0, 0, 0)),
                    pl.BlockSpec((2, 2 * LRU_WIDTH), fixed2), pl.BlockSpec((2, LRU_WIDTH), fixed2)],
        out_specs=[pl.BlockSpec((batch, ts, LRU_WIDTH), lambda i: (0, i, 0)),
                   pl.BlockSpec((batch, ts, LRU_WIDTH), lambda i: (0, nt - 1 - i, 0))],
        out_shape=[shp, shp],
        scratch_shapes=[pltpu.VMEM((ts + 2 * SUBLANES, LRU_WIDTH), F32), slab, slab, slab, slab, slab, slab,
                        carry, carry],
        compiler_params=_cparams("arbitrary"),
        name="lru",
    )(pd, pd, pd, pd, pd, pd, cw, cb, wg, bg, lam)


def _gelu_tanh(x):
    return 0.5 * x * (1.0 + jnp.tanh(np.sqrt(2.0 / np.pi).astype(np.float32) * (x + 0.044715 * (x * x * x))))


def _branch_out_kernel(o0, o1, o2, l0, l1, l2, hf_ref, hr_ref, gd_ref, ya_ref, yd_ref, *nat_refs, tm):
    def natural(ref, g, buf):
        dil = DIL_CONFIGS[g][1]
        if dil == 1:
            return ref[0, 0]
        chunks = BRANCH_W // LANES
        for r in range(dil):
            for c in range(chunks):
                buf[c, pl.ds(r, tm // dil, stride=dil), :] = ref[0, r, :, c * LANES:(c + 1) * LANES]
        return jnp.concatenate([buf[c] for c in range(chunks)], axis=1)

    outs = (o0[0, 0], natural(o1, 1, nat_refs[0]), natural(o2, 2, nat_refs[1]))
    lse = (l0[0, 0], natural(l1, 1, nat_refs[2]), natural(l2, 2, nat_refs[3]))
    m = jnp.maximum(jnp.maximum(lse[0], lse[1]), lse[2])
    e = [jnp.exp(v - m) for v in lse]
    tot = e[0] + e[1] + e[2]
    ya = (e[0] / tot) * outs[0] + (e[1] / tot) * outs[1] + (e[2] / tot) * outs[2]
    ya_ref[...] = ya.astype(BF16)
    yd_ref[...] = ((hf_ref[...] + hr_ref[...]) * _gelu_tanh(gd_ref[...])).astype(BF16)


def _branch_out(oa, la, hf, hr, pd, seq):
    t = hf.shape[0]
    tm = _pick(seq, 256)
    nseq = seq // tm
    blk = pl.BlockSpec((tm, BRANCH_W), lambda i: (i, 0))
    cls = [pl.BlockSpec((1, dil, tm // dil, BRANCH_W), lambda i: (i // nseq, 0, i % nseq, 0))
           for _, dil in DIL_CONFIGS]
    shp = jax.ShapeDtypeStruct((t, BRANCH_W), BF16)
    return pl.pallas_call(
        functools.partial(_branch_out_kernel, tm=tm),
        grid=(t // tm,),
        in_specs=cls + cls + [blk, blk, pl.BlockSpec((tm, BRANCH_W), lambda i: (i, 1))],
        out_specs=[blk, blk],
        out_shape=[shp, shp],
        scratch_shapes=[pltpu.VMEM((BRANCH_W // LANES, tm, LANES), F32)] * 4,
        compiler_params=_cparams("parallel"),
        name="branch_out",
    )(*oa, *la, hf, hr, pd)


def _gate_merge_out_kernel(x_ref, h_ref, ya, yb, yc, yd, g0, g1, g2, g3, wb_ref, wo_ref, o_ref):
    j = pl.program_id(1)
    last = pl.num_programs(1) - 1

    def partial_out():
        h = h_ref[...]
        merged = None
        for nbr, (y_ref, wg_ref) in enumerate(zip((ya, yb, yc, yd), (g0, g1, g2, g3))):
            gate = jax.nn.sigmoid(jnp.dot(h, wg_ref[...], preferred_element_type=F32))
            term = gate * jnp.dot(y_ref[...], wb_ref[nbr], preferred_element_type=F32)
            merged = term if merged is None else merged + term
        return jnp.dot(merged.astype(BF16), wo_ref[...], preferred_element_type=F32)

    @pl.when(j == 0)
    def _():
        o_ref[...] = partial_out()

    @pl.when((j > 0) & (j < last))
    def _():
        o_ref[...] += partial_out()

    @pl.when(j == last)
    def _():
        o_ref[...] = x_ref[...] + (o_ref[...] + partial_out())


def _gate_merge_out(x, h, ys, wg, wb, wo):
    t = x.shape[0]
    tm, tn = _pick(t, 512), 512
    nj = D_MODEL // tn
    row = lambda i, j: (i, 0)
    gate_specs = [pl.BlockSpec((D_MODEL, tn), functools.partial(lambda i, j, nbr: (0, nbr * nj + j), nbr=nbr))
                  for nbr in range(N_BRANCH)]
    return pl.pallas_call(
        _gate_merge_out_kernel,
        grid=(t // tm, nj),
        in_specs=[pl.BlockSpec((tm, D_MODEL), row), pl.BlockSpec((tm, D_MODEL), row)]
                 + [pl.BlockSpec((tm, BRANCH_W), row)] * N_BRANCH + gate_specs
                 + [pl.BlockSpec((N_BRANCH, BRANCH_W, tn), lambda i, j: (0, 0, j)),
                    pl.BlockSpec((tn, D_MODEL), lambda i, j: (j, 0))],
        out_specs=pl.BlockSpec((tm, D_MODEL), row),
        out_shape=jax.ShapeDtypeStruct((t, D_MODEL), F32),
        compiler_params=_cparams("parallel", "arbitrary"),
        name="gate_merge_out",
    )(x, h, *ys, wg, wg, wg, wg, wb, wo)


def _block_diag(w):
    eye = jnp.eye(LRU_BLOCKS, dtype=w.dtype)
    return jnp.einsum('nef,nm->nemf', w, eye).reshape(LRU_WIDTH, LRU_WIDTH)


def _prep_layer(l, ffn1_norm, ffn1_w1, ffn1_w3, ffn1_w2, mix_norm, w_in, mla_q_norm, mla_w_uq, mla_kv_norm,
                mla_w_ukv, lru_conv_w, lru_conv_b, lru_w_a, lru_b_a, lru_w_x, lru_b_x, lru_lambda, sink_logits,
                w_branch, w_out, ffn2_norm, ffn2_w1, ffn2_w3, ffn2_w2):
    row = lambda v: v.reshape(1, -1)
    wi = w_in[l]
    c0, c1, c2, c3 = A_COLS, A_COLS + B_COLS, A_COLS + B_COLS + C_COLS, A_COLS + B_COLS + C_COLS + D_COLS
    w_a = wi[:, :c0].reshape(D_MODEL, 3, N_DIL, A_HEADS * HEAD_DIM).transpose(0, 2, 1, 3).reshape(D_MODEL, A_COLS)
    wb = wi[:, c0:c1]
    zeros = lambda n: jnp.zeros((D_MODEL, n), wi.dtype)
    w_b = jnp.concatenate([wb[:, :Q_LORA + KV_LORA], zeros(QK_NOPE), wb[:, Q_LORA + KV_LORA:],
                           zeros(SLOT - QK_NOPE - QK_ROPE)], axis=1)
    w_cbd = jnp.concatenate([wi[:, c1:c2], w_b, wi[:, c2:c3]], axis=1)
    wq = mla_w_uq[l].reshape(Q_LORA, B_HEADS, QK_NOPE + QK_ROPE)
    wq = jnp.pad(wq, ((0, 0), (0, 0), (0, SLOT - QK_NOPE - QK_ROPE))).reshape(Q_LORA, B_HEADS * SLOT)
    wkv = mla_w_ukv[l].reshape(KV_LORA, B_HEADS, QK_NOPE + V_DIM)
    wk = jnp.pad(wkv[:, :, :QK_NOPE], ((0, 0), (0, 0), (0, SLOT - QK_NOPE))).reshape(KV_LORA, B_HEADS * SLOT)
    wv = wkv[:, :, QK_NOPE:].reshape(KV_LORA, B_HEADS * V_DIM)
    wg = jnp.stack([jnp.concatenate([_block_diag(lru_w_a[l, d]), _block_diag(lru_w_x[l, d])], axis=1)
                    for d in range(2)])
    bg = jnp.concatenate([lru_b_a[l], lru_b_x[l]], axis=1)
    return dict(
        ffn1=(row(ffn1_norm[l]), ffn1_w1[l].astype(BF16), ffn1_w3[l].astype(BF16), ffn1_w2[l].astype(BF16)),
        ffn2=(row(ffn2_norm[l]), ffn2_w1[l].astype(BF16), ffn2_w3[l].astype(BF16), ffn2_w2[l].astype(BF16)),
        mix_norm=row(mix_norm[l]),
        w_a=w_a.astype(BF16), w_cbd=w_cbd.astype(BF16), w_g=wi[:, c3:].astype(BF16),
        qn=row(mla_q_norm[l]), kvn=row(mla_kv_norm[l]),
        wq=wq.astype(BF16), wk=wk.astype(BF16), wv=wv.astype(BF16),
        conv_w=lru_conv_w[l], conv_b=row(lru_conv_b[l]), wg=wg.astype(BF16), bg=bg, lam=lru_lambda[l],
        sink=sink_logits[l], w_branch=w_branch[l].astype(BF16), w_out=w_out[l].astype(BF16),
    )


def _rope_slot_tables(seq):
    inv = ROPE_THETA ** (-jnp.arange(0, QK_ROPE, 2, dtype=F32) / QK_ROPE)
    ang = jnp.arange(seq, dtype=F32)[:, None] * inv[None, :]
    cos, sin = jnp.cos(ang), jnp.sin(ang)
    scale = (QK_NOPE + QK_ROPE) ** -0.5 * np.log2(np.e)
    z = lambda n: jnp.zeros((seq, n), F32)
    tail = SLOT - QK_NOPE - QK_ROPE
    cos_q = jnp.concatenate([jnp.full((seq, QK_NOPE), scale, F32), cos * scale, cos * scale, z(tail)], axis=1)
    sin_q = jnp.concatenate([z(QK_NOPE), -sin * scale, sin * scale, z(tail)], axis=1)
    cos_k = jnp.concatenate([z(QK_NOPE), cos, cos, z(tail)], axis=1)
    sin_k = jnp.concatenate([z(QK_NOPE), -sin, sin, z(tail)], axis=1)
    return cos_q, sin_q, cos_k, sin_k


def _layer(x, w, batch, seq, final_g):
    t = batch * seq
    x = _ffn(x, *w['ffn1'], final_g, False)
    *pas, h = _proj_a(x, w['mix_norm'], w['w_a'], batch, seq)
    pc, pb, pd = _proj_bcd(h, w['w_cbd'], ((C_COLS, BF16), (B_PAD_COLS, F32), (D_COLS, F32)))

    oa, la = [], []
    for pa, (window, dil) in zip(pas, DIL_CONFIGS):
        o, lse = _band_attention(
            pa, q_col=0, k_col=1, v_col=2, kv_width=A_HEADS * HEAD_DIM,
            n_heads=A_HEADS, n_kv=A_HEADS, radius=window // (2 * dil), step=dil, sink=None,
            out_dtype=F32, emit_lse=True)
        oa.append(o)
        la.append(lse)

    q, k, vt = _mla_prep(pb, w['qn'], w['kvn'], w['wq'], w['wk'], w['wv'], _rope_slot_tables(seq), batch, seq)
    yb = _mla_attn(q.reshape(batch, seq, -1), k.reshape(batch, seq, -1), vt, batch, seq)

    kvw = C_KV_HEADS * HEAD_DIM
    yc = _band_attention(
        pc.reshape(batch, 1, seq, C_COLS),
        q_col=0, k_col=C_HEADS * HEAD_DIM // kvw, v_col=C_HEADS * HEAD_DIM // kvw + 1, kv_width=kvw,
        n_heads=C_HEADS, n_kv=C_KV_HEADS, radius=C_RADIUS, step=1, sink=w['sink'],
        out_dtype=BF16, emit_lse=False)[0]

    hf, hr = _lru(pd.reshape(batch, seq, D_COLS), w['conv_w'], w['conv_b'], w['wg'], w['bg'], w['lam'])

    ya, yd = _branch_out(oa, la, hf.reshape(t, LRU_WIDTH), hr.reshape(t, LRU_WIDTH), pd, seq)
    x = _gate_merge_out(x, h, (ya, yb.reshape(t, BRANCH_W), yc.reshape(t, BRANCH_W), yd),
                        w['w_g'], w['w_branch'], w['w_out'])
    return _ffn(x, *w['ffn2'], final_g, final_g is not None and w.get('last', False))


def _trunk(x, layers, final_norm):
    batch, seq, _ = x.shape
    h = x.reshape(batch * seq, D_MODEL)
    fg = final_norm.reshape(1, -1)
    for l, w in enumerate(layers):
        h = _layer(h, dict(w, last=(l == len(layers) - 1)), batch, seq, fg)
    return h.reshape(batch, seq, D_MODEL)


def kernel(x_prompt, x_sample, ffn1_norm, ffn1_w1, ffn1_w3, ffn1_w2, mix_norm, w_in, mla_q_norm, mla_w_uq, mla_kv_norm, mla_w_ukv, lru_conv_w, lru_conv_b, lru_w_a, lru_b_a, lru_w_x, lru_b_x, lru_lambda, sink_logits, w_branch, w_out, ffn2_norm, ffn2_w1, ffn2_w3, ffn2_w2, final_norm):
    layers = [_prep_layer(l, ffn1_norm, ffn1_w1, ffn1_w3, ffn1_w2, mix_norm, w_in, mla_q_norm, mla_w_uq,
                          mla_kv_norm, mla_w_ukv, lru_conv_w, lru_conv_b, lru_w_a, lru_b_a, lru_w_x, lru_b_x,
                          lru_lambda, sink_logits, w_branch, w_out, ffn2_norm, ffn2_w1, ffn2_w3, ffn2_w2)
              for l in range(DEPTH)]
    return (_trunk(x_prompt, layers, final_norm), _trunk(x_sample, layers, final_norm))
```

```python
import functools

import numpy as np
import jax
import jax.numpy as jnp
from jax import lax
from jax.experimental import pallas as pl
from jax.experimental.pallas import tpu as pltpu

F32 = jnp.float32
BF16 = jnp.bfloat16

D_MODEL = 2048
DEPTH = 2
HEAD_DIM = 64
N_BRANCH = 4
BRANCH_W = 512
DIL_CONFIGS = ((128, 1), (512, 4), (2048, 16))
N_DIL = 3
A_HEADS = 8
B_HEADS = 8
Q_LORA = 384
KV_LORA = 128
QK_NOPE = 64
QK_ROPE = 32
V_DIM = 64
ROPE_THETA = 10000.0
C_HEADS = 8
C_KV_HEADS = 2
C_RADIUS = 128
LRU_WIDTH = 512
LRU_BLOCKS = 8
LRU_BLOCK = 64
CONV_W = 4
CONV_PAD_L = 2
LRU_C = 8.0
D_FF = 5632
EPS = 1e-6
NEG = -1e30

A_COLS = 3 * N_DIL * A_HEADS * HEAD_DIM
B_COLS = Q_LORA + KV_LORA + QK_ROPE
C_COLS = (C_HEADS + 2 * C_KV_HEADS) * HEAD_DIM
D_COLS = 2 * LRU_WIDTH
G_COLS = N_BRANCH * D_MODEL

LANES = 128
SUBLANES = 8
SLOT = LANES
B_PAD_COLS = Q_LORA + KV_LORA + SLOT
VMEM_LIMIT = 52 * 1024 * 1024
VMEM_LIMIT_FFN = 56 * 1024 * 1024


def _cparams(*sem, vmem=VMEM_LIMIT):
    return pltpu.CompilerParams(dimension_semantics=sem, vmem_limit_bytes=vmem)


def _rms(x, g):
    return x * lax.rsqrt(jnp.mean(x * x, axis=-1, keepdims=True) + EPS) * g


def _pick(n, pref):
    t = min(n, pref)
    while n % t:
        t //= 2
    return t


def _ffn_kernel(x_ref, g_ref, w1_ref, w3_ref, w2_ref, fg_ref, o_ref, h_ref, *, final_norm):
    j = pl.program_id(1)
    last = pl.num_programs(1) - 1
    tm = x_ref.shape[0]
    halves = (slice(0, tm // 2), slice(tm // 2, tm))

    def down(h):
        a = jnp.dot(h, w1_ref[...], preferred_element_type=F32)
        b = jnp.dot(h, w3_ref[...], preferred_element_type=F32)
        act = (a * jax.nn.sigmoid(a) * b).astype(BF16)
        return jnp.dot(act, w2_ref[...], preferred_element_type=F32)

    @pl.when(j == 0)
    def _():
        for rows in halves:
            h = _rms(x_ref[rows, :], g_ref[...]).astype(BF16)
            h_ref[rows, :] = h
            o_ref[rows, :] = down(h)

    @pl.when((j > 0) & (j < last))
    def _():
        for rows in halves:
            o_ref[rows, :] += down(h_ref[rows, :])

    @pl.when(j == last)
    def _():
        for rows in halves:
            y = x_ref[rows, :] + 0.5 * (o_ref[rows, :] + down(h_ref[rows, :]))
            if final_norm:
                y = _rms(y, fg_ref[...])
            o_ref[rows, :] = y


def _ffn(x, g, w1, w3, w2, fg, final_norm):
    t = x.shape[0]
    tm, tf = _pick(t, 1024), 512
    return pl.pallas_call(
        functools.partial(_ffn_kernel, final_norm=final_norm),
        grid=(t // tm, D_FF // tf),
        in_specs=[
            pl.BlockSpec((tm, D_MODEL), lambda i, j: (i, 0)),
            pl.BlockSpec((1, D_MODEL), lambda i, j: (0, 0)),
            pl.BlockSpec((D_MODEL, tf), lambda i, j: (0, j)),
            pl.BlockSpec((D_MODEL, tf), lambda i, j: (0, j)),
            pl.BlockSpec((tf, D_MODEL), lambda i, j: (j, 0)),
            pl.BlockSpec((1, D_MODEL), lambda i, j: (0, 0)),
        ],
        out_specs=pl.BlockSpec((tm, D_MODEL), lambda i, j: (i, 0)),
        out_shape=jax.ShapeDtypeStruct((t, D_MODEL), F32),
        scratch_shapes=[pltpu.VMEM((tm, D_MODEL), BF16)],
        compiler_params=_cparams("parallel", "arbitrary", vmem=VMEM_LIMIT_FFN),
        name="ffn",
    )(x, g, w1, w3, w2, fg)


def _proj_bcd_kernel(h_ref, w_ref, *o_refs):
    r = jnp.dot(h_ref[...], w_ref[...], preferred_element_type=F32)
    c = 0
    for o_ref in o_refs:
        wd = o_ref.shape[1]
        o_ref[...] = r[:, c:c + wd].astype(o_ref.dtype)
        c += wd


def _proj_bcd(h, w, outs):
    t, n = h.shape[0], w.shape[1]
    tm = _pick(t, 512)
    return pl.pallas_call(
        _proj_bcd_kernel,
        grid=(t // tm,),
        in_specs=[pl.BlockSpec((tm, D_MODEL), lambda i: (i, 0)),
                  pl.BlockSpec((D_MODEL, n), lambda i: (0, 0))],
        out_specs=[pl.BlockSpec((tm, wd), lambda i: (i, 0)) for wd, _ in outs],
        out_shape=[jax.ShapeDtypeStruct((t, wd), dt) for wd, dt in outs],
        compiler_params=_cparams("parallel"),
        name="proj_bcd",
    )(h, w)


def _proj_a_kernel(x_ref, g_ref, w_ref, o1_ref, o2_ref, o3_ref, hout_ref, hn_ref, hm_ref, h_ref, *, tm):
    j = pl.program_id(1)
    chunks = D_MODEL // LANES

    @pl.when(j == 0)
    def _():
        hn = _rms(x_ref[...], g_ref[...])
        for c in range(chunks):
            hn_ref[c] = hn[:, c * LANES:(c + 1) * LANES]
        h_ref[...] = hn.astype(BF16)
        hout_ref[...] = hn.astype(BF16)

    for g, o_ref in enumerate((o1_ref, o2_ref, o3_ref)):
        dil = DIL_CONFIGS[g][1]
        n = tm // dil

        @pl.when(j == g)
        def _(g=g, o_ref=o_ref, dil=dil, n=n):
            if g == 1:
                for r in range(dil):
                    for c in range(chunks):
                        rows = hn_ref[c, pl.ds(r, n, stride=dil), :]
                        hm_ref[c, r * n:(r + 1) * n, :] = rows
                        h_ref[r * n:(r + 1) * n, c * LANES:(c + 1) * LANES] = rows.astype(BF16)
            elif g == 2:
                prev = DIL_CONFIGS[g - 1][1]
                ratio, n_prev = dil // prev, tm // prev
                for r in range(dil):
                    for c in range(chunks):
                        rows = hm_ref[c, pl.ds((r % prev) * n_prev + r // prev, n, stride=ratio), :]
                        h_ref[r * n:(r + 1) * n, c * LANES:(c + 1) * LANES] = rows.astype(BF16)
            res = jnp.dot(h_ref[...], w_ref[...], preferred_element_type=F32)
            for r in range(dil):
                o_ref[0, r] = res[r * n:(r + 1) * n].astype(BF16)


def _proj_a(x, g, w, batch, seq):
    t = x.shape[0]
    tm = _pick(seq, 512)
    nseq = seq // tm
    gw = A_COLS // N_DIL
    out_specs = [pl.BlockSpec((1, dil, tm // dil, gw), lambda i, j: (i // nseq, 0, i % nseq, 0))
                 for _, dil in DIL_CONFIGS]
    out_shape = [jax.ShapeDtypeStruct((batch, dil, seq // dil, gw), BF16) for _, dil in DIL_CONFIGS]
    out_specs.append(pl.BlockSpec((tm, D_MODEL), lambda i, j: (i, 0)))
    out_shape.append(jax.ShapeDtypeStruct((t, D_MODEL), BF16))
    return pl.pallas_call(
        functools.partial(_proj_a_kernel, tm=tm),
        grid=(t // tm, N_DIL),
        in_specs=[pl.BlockSpec((tm, D_MODEL), lambda i, j: (i, 0)),
                  pl.BlockSpec((1, D_MODEL), lambda i, j: (0, 0)),
                  pl.BlockSpec((D_MODEL, gw), lambda i, j: (0, j))],
        out_specs=out_specs,
        out_shape=out_shape,
        scratch_shapes=[pltpu.VMEM((D_MODEL // LANES, tm, LANES), F32), pltpu.VMEM((D_MODEL // LANES, tm, LANES), F32),
                        pltpu.VMEM((tm, D_MODEL), BF16)],
        compiler_params=_cparams("parallel", "arbitrary"),
        name="proj_a",
    )(x, g, w)


def _band_kernel(*refs, n_heads, n_kv, radius, sb, tq, has_sink, emit_lse):
    if has_sink:
        sink_ref, refs = refs[0], refs[1:]
    bias_ref, q_ref, kp_ref, kc_ref, kn_ref, vp_ref, vc_ref, vn_ref = refs[:8]
    o_ref = refs[8]
    kx_ref, vx_ref, vt_ref, qt_ref = refs[-4:]
    for x_ref, (p_ref, c_ref, n_ref) in ((kx_ref, (kp_ref, kc_ref, kn_ref)), (vx_ref, (vp_ref, vc_ref, vn_ref))):
        x_ref[0:radius, :] = p_ref[0, 0]
        x_ref[radius:radius + tq, :] = c_ref[0, 0]
        x_ref[radius + tq:, :] = n_ref[0, 0]
    vt_ref[...] = vx_ref[...].astype(F32).T.astype(BF16)
    qt_ref[...] = (q_ref[0, 0].astype(F32) * (HEAD_DIM ** -0.5)).T.astype(BF16)
    w = sb + 2 * radius
    rep = n_heads // n_kv
    heads = range(n_heads)
    zeros = jnp.zeros((HEAD_DIM, sb), BF16)
    nsub = tq // sb
    tile, last_tile = pl.program_id(2), pl.num_programs(2) - 1
    for u in range(nsub):
        rows = slice(u * sb, (u + 1) * sb)
        win = slice(u * sb, u * sb + w)
        kind = 0
        if u == 0:
            kind = kind + (tile == 0).astype(jnp.int32)
        if u == nsub - 1:
            kind = kind + 2 * (tile == last_tile).astype(jnp.int32)
        scores = []
        for h in heads:
            g = h // rep
            qh = qt_ref[h * HEAD_DIM:(h + 1) * HEAD_DIM, rows]
            rhs = jnp.concatenate([qh, zeros] if g % 2 == 0 else [zeros, qh], axis=0)
            pair = slice((g // 2) * 2 * HEAD_DIM, (g // 2 + 1) * 2 * HEAD_DIM)
            scores.append(jnp.dot(kx_ref[win, pair], rhs, preferred_element_type=F32) + bias_ref[kind, h])
        probs, stats = [], []
        for h in heads:
            m = jnp.max(scores[h], axis=0, keepdims=True)
            if has_sink:
                m = jnp.maximum(m, sink_ref[h])
            p = jnp.exp(scores[h] - m)
            l = jnp.sum(p, axis=0, keepdims=True)
            if has_sink:
                l = l + jnp.exp(sink_ref[h] - m)
            probs.append(p.astype(BF16))
            stats.append((m, l))
        outs, lses = [], []
        for h in heads:
            g = h // rep
            m, l = stats[h]
            outs.append(jnp.dot(vt_ref[g * HEAD_DIM:(g + 1) * HEAD_DIM, win], probs[h],
                                preferred_element_type=F32) / l)
            lses.append(jnp.broadcast_to(m + jnp.log(l), (HEAD_DIM, sb)))
        o_ref[0, 0, rows, :] = jnp.concatenate(outs, axis=0).T.astype(o_ref.dtype)
        if emit_lse:
            refs[9][0, 0, rows, :] = jnp.concatenate(lses, axis=0).T


def _alibi(n):
    return np.asarray(2.0 ** (-8.0 * np.arange(1, n + 1) / n), dtype=np.float32)


def _band_bias(n_heads, radius, step, sb):
    w = sb + 2 * radius
    shape = (4, n_heads, w, sb)
    kind, h, kj, qi = (lax.broadcasted_iota(jnp.int32, shape, d) for d in range(4))
    rel = jnp.abs(kj - radius - qi)
    kpos = kj - radius
    ok = (rel <= radius) & ((kpos >= 0) | (kind % 2 == 0)) & ((kpos < sb) | (kind < 2))
    slopes = jnp.asarray(_alibi(n_heads))[h]
    return jnp.where(ok, -slopes * (step * rel).astype(F32), NEG)


def _band_attention(arr, *, q_col, k_col, v_col, kv_width, n_heads, n_kv, radius, step, sink, out_dtype,
                    emit_lse):
    batch, n_classes, length, _ = arr.shape
    qw = n_heads * HEAD_DIM
    tq = _pick(length, 512)
    sb = min(tq, 128)
    nblk = length // tq

    per = tq // radius
    last_halo = length // radius - 1

    def kv_spec(col, shift):
        if shift == 0:
            return pl.BlockSpec((1, 1, tq, kv_width), lambda b, r, i: (b, r, i, col))
        if shift < 0:
            return pl.BlockSpec((1, 1, radius, kv_width), lambda b, r, i: (b, r, jnp.maximum(i * per - 1, 0), col))
        return pl.BlockSpec((1, 1, radius, kv_width),
                            lambda b, r, i: (b, r, jnp.minimum((i + 1) * per, last_halo), col))

    bias = _band_bias(n_heads, radius, step, sb)
    in_specs = [pl.BlockSpec(bias.shape, lambda b, r, i: (0, 0, 0, 0)),
                pl.BlockSpec((1, 1, tq, qw), lambda b, r, i: (b, r, i, q_col)),
                kv_spec(k_col, -1), kv_spec(k_col, 0), kv_spec(k_col, 1),
                kv_spec(v_col, -1), kv_spec(v_col, 0), kv_spec(v_col, 1)]
    args = [bias] + [arr] * 7
    if sink is not None:
        in_specs = [pl.BlockSpec(memory_space=pltpu.SMEM)] + in_specs
        args = [sink] + args
    o_spec = pl.BlockSpec((1, 1, tq, qw), lambda b, r, i: (b, r, i, 0))
    o_shape = jax.ShapeDtypeStruct((batch, n_classes, length, qw), out_dtype)
    out_specs, out_shape = [o_spec], [o_shape]
    if emit_lse:
        out_specs, out_shape = [o_spec, o_spec], [o_shape, jax.ShapeDtypeStruct(o_shape.shape, F32)]
    return pl.pallas_call(
        functools.partial(_band_kernel, n_heads=n_heads, n_kv=n_kv, radius=radius, sb=sb, tq=tq,
                          has_sink=sink is not None, emit_lse=emit_lse),
        grid=(batch, n_classes, nblk),
        in_specs=in_specs,
        out_specs=out_specs,
        out_shape=out_shape,
        scratch_shapes=[pltpu.VMEM((tq + 2 * radius, kv_width), BF16), pltpu.VMEM((tq + 2 * radius, kv_width), BF16),
                        pltpu.VMEM((kv_width, tq + 2 * radius), BF16), pltpu.VMEM((qw, tq), BF16)],
        compiler_params=_cparams("parallel", "parallel", "arbitrary"),
        name="band_attn",
    )(*args)


def _rope_slot(x, cos_t, sin_t):
    lane = lax.broadcasted_iota(jnp.int32, x.shape, 1)
    first = (lane >= QK_NOPE) & (lane < QK_NOPE + QK_ROPE // 2)
    partner = jnp.where(first, pltpu.roll(x, SLOT - QK_ROPE // 2, 1), pltpu.roll(x, QK_ROPE // 2, 1))
    return x * cos_t + partner * sin_t


def _mla_prep_kernel(pb_ref, qn_ref, kvn_ref, wq_ref, wk_ref, wv_ref, cq_ref, sq_ref, ck_ref, sk_ref,
                     q_out, k_out, vt_out):
    pb = pb_ref[...]
    cq = _rms(pb[:, :Q_LORA], qn_ref[...]).astype(BF16)
    ckv = _rms(pb[:, Q_LORA:Q_LORA + KV_LORA], kvn_ref[...]).astype(BF16)
    kr = pb[:, Q_LORA + KV_LORA:]
    q = jnp.dot(cq, wq_ref[...], preferred_element_type=F32)
    k = jnp.dot(ckv, wk_ref[...], preferred_element_type=F32)
    vt_out[0] = jnp.dot(ckv, wv_ref[...], preferred_element_type=F32).T.astype(BF16)
    kr = _rope_slot(kr, ck_ref[...], sk_ref[...])
    for h in range(B_HEADS):
        sl = slice(h * SLOT, (h + 1) * SLOT)
        q_out[:, sl] = _rope_slot(q[:, sl], cq_ref[...], sq_ref[...]).astype(BF16)
        k_out[:, sl] = (k[:, sl] + kr).astype(BF16)


def _mla_prep(pb, qn, kvn, wq, wk, wv, tabs, batch, seq):
    t = pb.shape[0]
    tm = _pick(seq, 512)
    nseq = seq // tm
    row = lambda i: (i, 0)
    fixed = lambda i: (0, 0)
    tab = pl.BlockSpec((tm, SLOT), lambda i: (i % nseq, 0))
    return pl.pallas_call(
        _mla_prep_kernel,
        grid=(t // tm,),
        in_specs=[pl.BlockSpec((tm, B_PAD_COLS), row),
                  pl.BlockSpec((1, Q_LORA), fixed), pl.BlockSpec((1, KV_LORA), fixed),
                  pl.BlockSpec((Q_LORA, B_HEADS * SLOT), fixed),
                  pl.BlockSpec((KV_LORA, B_HEADS * SLOT), fixed),
                  pl.BlockSpec((KV_LORA, B_HEADS * V_DIM), fixed),
                  tab, tab, tab, tab],
        out_specs=[pl.BlockSpec((tm, B_HEADS * SLOT), row), pl.BlockSpec((tm, B_HEADS * SLOT), row),
                   pl.BlockSpec((1, B_HEADS * V_DIM, tm), lambda i: (i // nseq, 0, i % nseq))],
        out_shape=[jax.ShapeDtypeStruct((t, B_HEADS * SLOT), BF16), jax.ShapeDtypeStruct((t, B_HEADS * SLOT), BF16),
                   jax.ShapeDtypeStruct((batch, B_HEADS * V_DIM, seq), BF16)],
        compiler_params=_cparams("parallel"),
        name="mla_prep",
    )(pb, qn, kvn, wq, wk, wv, *tabs)


def _mla_attn_kernel(q_ref, k_ref, vt_ref, o_ref, st_ref, *, seq, kc):
    chunks = [slice(c * kc, (c + 1) * kc) for c in range(seq // kc)]

    def qk(h, rows):
        sl = slice(h * SLOT, (h + 1) * SLOT)
        st = lax.dot_general(k_ref[0, rows, sl], q_ref[0, :, sl], (((1,), (1,)), ((), ())),
                             preferred_element_type=F32)
        st_ref[h % 2, rows, :] = st
        return jnp.max(st, axis=0, keepdims=True)

    def pv(h, rows, m):
        p = jnp.exp2(st_ref[h % 2, rows, :] - m)
        return (jnp.sum(p, axis=0, keepdims=True),
                jnp.dot(vt_ref[0, h * V_DIM:(h + 1) * V_DIM, rows], p.astype(BF16), preferred_element_type=F32))

    def add(acc, new):
        return new if acc is None else (acc[0] + new[0], acc[1] + new[1])

    n_heads = q_ref.shape[2] // SLOT
    m = functools.reduce(jnp.maximum, [qk(0, rows) for rows in chunks])
    outs = []
    for h in range(n_heads):
        acc = m_next = None
        for rows in chunks:
            acc = add(acc, pv(h, rows, m))
            if h + 1 < n_heads:
                mc = qk(h + 1, rows)
                m_next = mc if m_next is None else jnp.maximum(m_next, mc)
        outs.append(acc[1] / acc[0])
        m = m_next
    o_ref[0] = jnp.concatenate(outs, axis=0).T.astype(o_ref.dtype)


def _mla_attn(q, k, vt, batch, seq):
    tq = _pick(seq, 512)
    kc = _pick(seq, 256)
    hg = B_HEADS // 2
    return pl.pallas_call(
        functools.partial(_mla_attn_kernel, seq=seq, kc=kc),
        scratch_shapes=[pltpu.VMEM((2, seq, tq), F32)],
        grid=(batch, B_HEADS // hg, seq // tq),
        in_specs=[pl.BlockSpec((1, tq, hg * SLOT), lambda b, g, i: (b, i, g)),
                  pl.BlockSpec((1, seq, hg * SLOT), lambda b, g, i: (b, 0, g)),
                  pl.BlockSpec((1, hg * V_DIM, seq), lambda b, g, i: (b, g, 0))],
        out_specs=pl.BlockSpec((1, tq, hg * V_DIM), lambda b, g, i: (b, i, g)),
        out_shape=jax.ShapeDtypeStruct((batch, seq, B_HEADS * V_DIM), BF16),
        compiler_params=_cparams("parallel", "parallel", "arbitrary"),
        name="mla_attn",
    )(q, k, vt)


def _sigmoid_tanh(x):
    return 0.5 * (jnp.tanh(0.5 * x) + 1.0)


def _softplus(x):
    return jnp.maximum(x, 0.0) + jnp.log1p(jnp.exp(-jnp.abs(x)))


def _lru_kernel(xf_ref, xfp_ref, xfn_ref, xr_ref, xrp_ref, xrn_ref, cw_ref, cb_ref, wg_ref, bg_ref, lam_ref,
                hf_ref, hr_ref, ext_ref, af_ref, uf_ref, ar_ref, ur_ref, hfs_ref, hrs_ref, cf_ref, cr_ref,
                *, ts, pitch):
    i = pl.program_id(0)
    nt = pl.num_programs(0)
    halo = SUBLANES
    nb = xf_ref.shape[0]
    slabs = LRU_WIDTH // LANES

    @pl.when(i == 0)
    def _():
        cf_ref[...] = jnp.zeros_like(cf_ref)
        cr_ref[...] = jnp.zeros_like(cr_ref)

    def gates(d, tile, x_ref, xp_ref, xn_ref, a_ref, u_ref):
        def one_batch(b, carry):
            ext_ref[0:halo, :] = jnp.where(tile > 0, xp_ref[b], 0.0)
            ext_ref[halo:halo + ts, :] = x_ref[b]
            ext_ref[halo + ts:, :] = jnp.where(tile < nt - 1, xn_ref[b], 0.0)
            xc = cb_ref[...]
            for tap in range(CONV_W):
                off = halo - CONV_PAD_L + tap
                xc = xc + ext_ref[off:off + ts, :] * cw_ref[tap:tap + 1, :]
            g = jnp.dot(xc.astype(BF16), wg_ref[d], preferred_element_type=F32) + bg_ref[d:d + 1, :]
            r = _sigmoid_tanh(g[:, :LRU_WIDTH])
            ig = _sigmoid_tanh(g[:, LRU_WIDTH:])
            log_a = -LRU_C * r * _softplus(-lam_ref[d:d + 1, :])
            a = jnp.exp(log_a)
            u = jnp.sqrt(-jnp.tanh(log_a) * (1.0 + a * a)) * (ig * xc)
            rows = pl.ds(pl.multiple_of(b * pitch, SUBLANES), ts)
            for c in range(slabs):
                a_ref[c, rows, :] = a[:, c * LANES:(c + 1) * LANES]
                u_ref[c, rows, :] = u[:, c * LANES:(c + 1) * LANES]
            return carry
        lax.fori_loop(0, nb, one_batch, 0)

    gates(0, i, xf_ref, xfp_ref, xfn_ref, af_ref, uf_ref)
    gates(1, nt - 1 - i, xr_ref, xrp_ref, xrn_ref, ar_ref, ur_ref)

    def time_step(t, carry):
        hf, hr = carry
        fwd = pl.ds(t, nb, stride=pitch)
        rev = pl.ds(ts - 1 - t, nb, stride=pitch)
        new_f, new_r = [], []
        for c in range(slabs):
            h = af_ref[c, fwd, :] * hf[c] + uf_ref[c, fwd, :]
            hfs_ref[c, fwd, :] = h
            new_f.append(h)
            h = ar_ref[c, rev, :] * hr[c] + ur_ref[c, rev, :]
            hrs_ref[c, rev, :] = h
            new_r.append(h)
        return tuple(new_f), tuple(new_r)

    init = (tuple(cf_ref[c] for c in range(slabs)), tuple(cr_ref[c] for c in range(slabs)))
    hf, hr = lax.fori_loop(0, ts, time_step, init, unroll=8)
    for c in range(slabs):
        cf_ref[c] = hf[c]
        cr_ref[c] = hr[c]
        for b in range(nb):
            hf_ref[b, :, c * LANES:(c + 1) * LANES] = hfs_ref[c, b * pitch:b * pitch + ts, :]
            hr_ref[b, :, c * LANES:(c + 1) * LANES] = hrs_ref[c, b * pitch:b * pitch + ts, :]


def _lru(pd, cw, cb, wg, bg, lam):
    batch, seq, _ = pd.shape
    ts = _pick(seq, 128)
    nt = seq // ts
    per = ts // SUBLANES
    last8 = seq // SUBLANES - 1
    pitch = ts + SUBLANES

    def tile_specs(tile):
        return [pl.BlockSpec((batch, ts, LRU_WIDTH), lambda i: (0, tile(i), 0)),
                pl.BlockSpec((batch, SUBLANES, LRU_WIDTH), lambda i: (0, jnp.maximum(tile(i) * per - 1, 0), 0)),
                pl.BlockSpec((batch, SUBLANES, LRU_WIDTH), lambda i: (0, jnp.minimum((tile(i) + 1) * per, last8), 0))]

    fwd_tile = lambda i: i
    rev_tile = lambda i: nt - 1 - i
    fixed2 = lambda i: (0, 0)
    shp = jax.ShapeDtypeStruct((batch, seq, LRU_WIDTH), F32)
    slab = pltpu.VMEM((LRU_WIDTH // LANES, batch * pitch, LANES), F32)
    carry = pltpu.VMEM((LRU_WIDTH // LANES, batch, LANES), F32)
    return pl.pallas_call(
        functools.partial(_lru_kernel, ts=ts, pitch=pitch),
        grid=(nt,),
        in_specs=tile_specs(fwd_tile) + tile_specs(rev_tile)
                 + [pl.BlockSpec((CONV_W, LRU_WIDTH), fixed2), pl.BlockSpec((1, LRU_WIDTH), fixed2),
                    pl.BlockSpec((2, LRU_WIDTH, 2 * LRU_WIDTH), lambda i: (0, 0, 0)),
                    pl.BlockSpec((2, 2 * LRU_WIDTH), fixed2), pl.BlockSpec((2, LRU_WIDTH), fixed2)],
        out_specs=[pl.BlockSpec((batch, ts, LRU_WIDTH), lambda i: (0, i, 0)),
                   pl.BlockSpec((batch, ts, LRU_WIDTH), lambda i: (0, nt - 1 - i, 0))],
        out_shape=[shp, shp],
        scratch_shapes=[pltpu.VMEM((ts + 2 * SUBLANES, LRU_WIDTH), F32), slab, slab, slab, slab, slab, slab,
                        carry, carry],
        compiler_params=_cparams("arbitrary"),
        name="lru",
    )(pd, pd, pd, pd, pd, pd, cw, cb, wg, bg, lam)


def _gelu_tanh(x):
    return 0.5 * x * (1.0 + jnp.tanh(np.sqrt(2.0 / np.pi).astype(np.float32) * (x + 0.044715 * (x * x * x))))


def _branch_out_kernel(o0, o1, o2, l0, l1, l2, hf_ref, hr_ref, gd_ref, ya_ref, yd_ref, *nat_refs, tm):
    def natural(ref, g, buf):
        dil = DIL_CONFIGS[g][1]
        if dil == 1:
            return ref[0, 0]
        chunks = BRANCH_W // LANES
        for r in range(dil):
            for c in range(chunks):
                buf[c, pl.ds(r, tm // dil, stride=dil), :] = ref[0, r, :, c * LANES:(c + 1) * LANES]
        return jnp.concatenate([buf[c] for c in range(chunks)], axis=1)

    outs = (o0[0, 0], natural(o1, 1, nat_refs[0]), natural(o2, 2, nat_refs[1]))
    lse = (l0[0, 0], natural(l1, 1, nat_refs[2]), natural(l2, 2, nat_refs[3]))
    m = jnp.maximum(jnp.maximum(lse[0], lse[1]), lse[2])
    e = [jnp.exp(v - m) for v in lse]
    tot = e[0] + e[1] + e[2]
    ya = (e[0] / tot) * outs[0] + (e[1] / tot) * outs[1] + (e[2] / tot) * outs[2]
    ya_ref[...] = ya.astype(BF16)
    yd_ref[...] = ((hf_ref[...] + hr_ref[...]) * _gelu_tanh(gd_ref[...])).astype(BF16)


def _branch_out(oa, la, hf, hr, pd, seq):
    t = hf.shape[0]
    tm = _pick(seq, 512)
    nseq = seq // tm
    blk = pl.BlockSpec((tm, BRANCH_W), lambda i: (i, 0))
    cls = [pl.BlockSpec((1, dil, tm // dil, BRANCH_W), lambda i: (i // nseq, 0, i % nseq, 0))
           for _, dil in DIL_CONFIGS]
    shp = jax.ShapeDtypeStruct((t, BRANCH_W), BF16)
    return pl.pallas_call(
        functools.partial(_branch_out_kernel, tm=tm),
        grid=(t // tm,),
        in_specs=cls + cls + [blk, blk, pl.BlockSpec((tm, BRANCH_W), lambda i: (i, 1))],
        out_specs=[blk, blk],
        out_shape=[shp, shp],
        scratch_shapes=[pltpu.VMEM((BRANCH_W // LANES, tm, LANES), F32)] * 4,
        compiler_params=_cparams("parallel"),
        name="branch_out",
    )(*oa, *la, hf, hr, pd)


def _gate_merge_out_kernel(x_ref, h_ref, ya, yb, yc, yd, g0, g1, g2, g3, wb_ref, wo_ref, o_ref):
    j = pl.program_id(1)
    last = pl.num_programs(1) - 1

    def partial_out():
        h = h_ref[...]
        merged = None
        for nbr, (y_ref, wg_ref) in enumerate(zip((ya, yb, yc, yd), (g0, g1, g2, g3))):
            gate = jax.nn.sigmoid(jnp.dot(h, wg_ref[...], preferred_element_type=F32))
            term = gate * jnp.dot(y_ref[...], wb_ref[nbr], preferred_element_type=F32)
            merged = term if merged is None else merged + term
        return jnp.dot(merged.astype(BF16), wo_ref[...], preferred_element_type=F32)

    @pl.when(j == 0)
    def _():
        o_ref[...] = partial_out()

    @pl.when((j > 0) & (j < last))
    def _():
        o_ref[...] += partial_out()

    @pl.when(j == last)
    def _():
        o_ref[...] = x_ref[...] + (o_ref[...] + partial_out())


def _gate_merge_out(x, h, ys, wg, wb, wo):
    t = x.shape[0]
    tm, tn = _pick(t, 512), 512
    nj = D_MODEL // tn
    row = lambda i, j: (i, 0)
    gate_specs = [pl.BlockSpec((D_MODEL, tn), functools.partial(lambda i, j, nbr: (0, nbr * nj + j), nbr=nbr))
                  for nbr in range(N_BRANCH)]
    return pl.pallas_call(
        _gate_merge_out_kernel,
        grid=(t // tm, nj),
        in_specs=[pl.BlockSpec((tm, D_MODEL), row), pl.BlockSpec((tm, D_MODEL), row)]
                 + [pl.BlockSpec((tm, BRANCH_W), row)] * N_BRANCH + gate_specs
                 + [pl.BlockSpec((N_BRANCH, BRANCH_W, tn), lambda i, j: (0, 0, j)),
                    pl.BlockSpec((tn, D_MODEL), lambda i, j: (j, 0))],
        out_specs=pl.BlockSpec((tm, D_MODEL), row),
        out_shape=jax.ShapeDtypeStruct((t, D_MODEL), F32),
        compiler_params=_cparams("parallel", "arbitrary"),
        name="gate_merge_out",
    )(x, h, *ys, wg, wg, wg, wg, wb, wo)


def _block_diag(w):
    eye = jnp.eye(LRU_BLOCKS, dtype=w.dtype)
    return jnp.einsum('nef,nm->nemf', w, eye).reshape(LRU_WIDTH, LRU_WIDTH)


def _prep_layer(l, ffn1_norm, ffn1_w1, ffn1_w3, ffn1_w2, mix_norm, w_in, mla_q_norm, mla_w_uq, mla_kv_norm,
                mla_w_ukv, lru_conv_w, lru_conv_b, lru_w_a, lru_b_a, lru_w_x, lru_b_x, lru_lambda, sink_logits,
                w_branch, w_out, ffn2_norm, ffn2_w1, ffn2_w3, ffn2_w2):
    row = lambda v: v.reshape(1, -1)
    wi = w_in[l]
    c0, c1, c2, c3 = A_COLS, A_COLS + B_COLS, A_COLS + B_COLS + C_COLS, A_COLS + B_COLS + C_COLS + D_COLS
    w_a = wi[:, :c0].reshape(D_MODEL, 3, N_DIL, A_HEADS * HEAD_DIM).transpose(0, 2, 1, 3).reshape(D_MODEL, A_COLS)
    wb = wi[:, c0:c1]
    zeros = lambda n: jnp.zeros((D_MODEL, n), wi.dtype)
    w_b = jnp.concatenate([wb[:, :Q_LORA + KV_LORA], zeros(QK_NOPE), wb[:, Q_LORA + KV_LORA:],
                           zeros(SLOT - QK_NOPE - QK_ROPE)], axis=1)
    w_cbd = jnp.concatenate([wi[:, c1:c2], w_b, wi[:, c2:c3]], axis=1)
    wq = mla_w_uq[l].reshape(Q_LORA, B_HEADS, QK_NOPE + QK_ROPE)
    wq = jnp.pad(wq, ((0, 0), (0, 0), (0, SLOT - QK_NOPE - QK_ROPE))).reshape(Q_LORA, B_HEADS * SLOT)
    wkv = mla_w_ukv[l].reshape(KV_LORA, B_HEADS, QK_NOPE + V_DIM)
    wk = jnp.pad(wkv[:, :, :QK_NOPE], ((0, 0), (0, 0), (0, SLOT - QK_NOPE))).reshape(KV_LORA, B_HEADS * SLOT)
    wv = wkv[:, :, QK_NOPE:].reshape(KV_LORA, B_HEADS * V_DIM)
    wg = jnp.stack([jnp.concatenate([_block_diag(lru_w_a[l, d]), _block_diag(lru_w_x[l, d])], axis=1)
                    for d in range(2)])
    bg = jnp.concatenate([lru_b_a[l], lru_b_x[l]], axis=1)
    return dict(
        ffn1=(row(ffn1_norm[l]), ffn1_w1[l].astype(BF16), ffn1_w3[l].astype(BF16), ffn1_w2[l].astype(BF16)),
        ffn2=(row(ffn2_norm[l]), ffn2_w1[l].astype(BF16), ffn2_w3[l].astype(BF16), ffn2_w2[l].astype(BF16)),
        mix_norm=row(mix_norm[l]),
        w_a=w_a.astype(BF16), w_cbd=w_cbd.astype(BF16), w_g=wi[:, c3:].astype(BF16),
        qn=row(mla_q_norm[l]), kvn=row(mla_kv_norm[l]),
        wq=wq.astype(BF16), wk=wk.astype(BF16), wv=wv.astype(BF16),
        conv_w=lru_conv_w[l], conv_b=row(lru_conv_b[l]), wg=wg.astype(BF16), bg=bg, lam=lru_lambda[l],
        sink=sink_logits[l], w_branch=w_branch[l].astype(BF16), w_out=w_out[l].astype(BF16),
    )


def _rope_slot_tables(seq):
    inv = ROPE_THETA ** (-jnp.arange(0, QK_ROPE, 2, dtype=F32) / QK_ROPE)
    ang = jnp.arange(seq, dtype=F32)[:, None] * inv[None, :]
    cos, sin = jnp.cos(ang), jnp.sin(ang)
    scale = (QK_NOPE + QK_ROPE) ** -0.5 * np.log2(np.e)
    z = lambda n: jnp.zeros((seq, n), F32)
    tail = SLOT - QK_NOPE - QK_ROPE
    cos_q = jnp.concatenate([jnp.full((seq, QK_NOPE), scale, F32), cos * scale, cos * scale, z(tail)], axis=1)
    sin_q = jnp.concatenate([z(QK_NOPE), -sin * scale, sin * scale, z(tail)], axis=1)
    cos_k = jnp.concatenate([z(QK_NOPE), cos, cos, z(tail)], axis=1)
    sin_k = jnp.concatenate([z(QK_NOPE), -sin, sin, z(tail)], axis=1)
    return cos_q, sin_q, cos_k, sin_k


def _layer(x, w, batch, seq, final_g):
    t = batch * seq
    x = _ffn(x, *w['ffn1'], final_g, False)
    *pas, h = _proj_a(x, w['mix_norm'], w['w_a'], batch, seq)
    pc, pb, pd = _proj_bcd(h, w['w_cbd'], ((C_COLS, BF16), (B_PAD_COLS, F32), (D_COLS, F32)))

    oa, la = [], []
    for pa, (window, dil) in zip(pas, DIL_CONFIGS):
        o, lse = _band_attention(
            pa, q_col=0, k_col=1, v_col=2, kv_width=A_HEADS * HEAD_DIM,
            n_heads=A_HEADS, n_kv=A_HEADS, radius=window // (2 * dil), step=dil, sink=None,
            out_dtype=F32, emit_lse=True)
        oa.append(o)
        la.append(lse)

    q, k, vt = _mla_prep(pb, w['qn'], w['kvn'], w['wq'], w['wk'], w['wv'], _rope_slot_tables(seq), batch, seq)
    yb = _mla_attn(q.reshape(batch, seq, -1), k.reshape(batch, seq, -1), vt, batch, seq)

    kvw = C_KV_HEADS * HEAD_DIM
    yc = _band_attention(
        pc.reshape(batch, 1, seq, C_COLS),
        q_col=0, k_col=C_HEADS * HEAD_DIM // kvw, v_col=C_HEADS * HEAD_DIM // kvw + 1, kv_width=kvw,
        n_heads=C_HEADS, n_kv=C_KV_HEADS, radius=C_RADIUS, step=1, sink=w['sink'],
        out_dtype=BF16, emit_lse=False)[0]

    hf, hr = _lru(pd.reshape(batch, seq, D_COLS), w['conv_w'], w['conv_b'], w['wg'], w['bg'], w['lam'])

    ya, yd = _branch_out(oa, la, hf.reshape(t, LRU_WIDTH), hr.reshape(t, LRU_WIDTH), pd, seq)
    x = _gate_merge_out(x, h, (ya, yb.reshape(t, BRANCH_W), yc.reshape(t, BRANCH_W), yd),
                        w['w_g'], w['w_branch'], w['w_out'])
    return _ffn(x, *w['ffn2'], final_g, final_g is not None and w.get('last', False))


def _trunk(x, layers, final_norm):
    batch, seq, _ = x.shape
    h = x.reshape(batch * seq, D_MODEL)
    fg = final_norm.reshape(1, -1)
    for l, w in enumerate(layers):
        h = _layer(h, dict(w, last=(l == len(layers) - 1)), batch, seq, fg)
    return h.reshape(batch, seq, D_MODEL)


def kernel(x_prompt, x_sample, ffn1_norm, ffn1_w1, ffn1_w3, ffn1_w2, mix_norm, w_in, mla_q_norm, mla_w_uq, mla_kv_norm, mla_w_ukv, lru_conv_w, lru_conv_b, lru_w_a, lru_b_a, lru_w_x, lru_b_x, lru_lambda, sink_logits, w_branch, w_out, ffn2_norm, ffn2_w1, ffn2_w3, ffn2_w2, final_norm):
    layers = [_prep_layer(l, ffn1_norm, ffn1_w1, ffn1_w3, ffn1_w2, mix_norm, w_in, mla_q_norm, mla_w_uq,
                          mla_kv_norm, mla_w_ukv, lru_conv_w, lru_conv_b, lru_w_a, lru_b_a, lru_w_x, lru_b_x,
                          lru_lambda, sink_logits, w_branch, w_out, ffn2_norm, ffn2_w1, ffn2_w3, ffn2_w2)
              for l in range(DEPTH)]
    return (_trunk(x_prompt, layers, final_norm), _trunk(x_sample, layers, final_norm))
```

```python
import functools

import numpy as np
import jax
import jax.numpy as jnp
from jax import lax
from jax.experimental import pallas as pl
from jax.experimental.pallas import tpu as pltpu

F32 = jnp.float32
BF16 = jnp.bfloat16

D_MODEL = 2048
DEPTH = 2
HEAD_DIM = 64
N_BRANCH = 4
BRANCH_W = 512
DIL_CONFIGS = ((128, 1), (512, 4), (2048, 16))
N_DIL = 3
A_HEADS = 8
B_HEADS = 8
Q_LORA = 384
KV_LORA = 128
QK_NOPE = 64
QK_ROPE = 32
V_DIM = 64
ROPE_THETA = 10000.0
C_HEADS = 8
C_KV_HEADS = 2
C_RADIUS = 128
LRU_WIDTH = 512
LRU_BLOCKS = 8
LRU_BLOCK = 64
CONV_W = 4
CONV_PAD_L = 2
LRU_C = 8.0
D_FF = 5632
EPS = 1e-6
NEG = -1e30

A_COLS = 3 * N_DIL * A_HEADS * HEAD_DIM
B_COLS = Q_LORA + KV_LORA + QK_ROPE
C_COLS = (C_HEADS + 2 * C_KV_HEADS) * HEAD_DIM
D_COLS = 2 * LRU_WIDTH
G_COLS = N_BRANCH * D_MODEL

LANES = 128
SUBLANES = 8
SLOT = LANES
B_PAD_COLS = Q_LORA + KV_LORA + SLOT
TILE_ROWS = 512
FFN_TILE_ROWS = 1024
FFN_TILE_HIDDEN = 512
MERGE_TILE_COLS = 512
ATTN_TILE_Q = 512
BAND_BLOCK_Q = 128
MLA_KEY_CHUNK = 256
LRU_TILE_STEPS = 128
VMEM_LIMIT = 52 * 1024 * 1024
VMEM_LIMIT_FFN = 56 * 1024 * 1024


def _cparams(*sem, vmem=VMEM_LIMIT):
    return pltpu.CompilerParams(dimension_semantics=sem, vmem_limit_bytes=vmem)


def _rms(x, g):
    return x * lax.rsqrt(jnp.mean(x * x, axis=-1, keepdims=True) + EPS) * g


def _pick(n, pref):
    t = min(n, pref)
    while n % t:
        t //= 2
    return t


def _ffn_kernel(x_ref, g_ref, w1_ref, w3_ref, w2_ref, fg_ref, o_ref, h_ref, *, final_norm):
    j = pl.program_id(1)
    last = pl.num_programs(1) - 1
    tm = x_ref.shape[0]
    halves = (slice(0, tm // 2), slice(tm // 2, tm))

    def down(h):
        a = jnp.dot(h, w1_ref[...], preferred_element_type=F32)
        b = jnp.dot(h, w3_ref[...], preferred_element_type=F32)
        act = (a * jax.nn.sigmoid(a) * b).astype(BF16)
        return jnp.dot(act, w2_ref[...], preferred_element_type=F32)

    @pl.when(j == 0)
    def _():
        for rows in halves:
            h = _rms(x_ref[rows, :], g_ref[...]).astype(BF16)
            h_ref[rows, :] = h
            o_ref[rows, :] = down(h)

    @pl.when((j > 0) & (j < last))
    def _():
        for rows in halves:
            o_ref[rows, :] += down(h_ref[rows, :])

    @pl.when(j == last)
    def _():
        for rows in halves:
            y = x_ref[rows, :] + 0.5 * (o_ref[rows, :] + down(h_ref[rows, :]))
            if final_norm:
                y = _rms(y, fg_ref[...])
            o_ref[rows, :] = y


def _ffn(x, g, w1, w3, w2, fg, final_norm):
    t = x.shape[0]
    tm, tf = _pick(t, FFN_TILE_ROWS), FFN_TILE_HIDDEN
    return pl.pallas_call(
        functools.partial(_ffn_kernel, final_norm=final_norm),
        grid=(t // tm, D_FF // tf),
        in_specs=[
            pl.BlockSpec((tm, D_MODEL), lambda i, j: (i, 0)),
            pl.BlockSpec((1, D_MODEL), lambda i, j: (0, 0)),
            pl.BlockSpec((D_MODEL, tf), lambda i, j: (0, j)),
            pl.BlockSpec((D_MODEL, tf), lambda i, j: (0, j)),
            pl.BlockSpec((tf, D_MODEL), lambda i, j: (j, 0)),
            pl.BlockSpec((1, D_MODEL), lambda i, j: (0, 0)),
        ],
        out_specs=pl.BlockSpec((tm, D_MODEL), lambda i, j: (i, 0)),
        out_shape=jax.ShapeDtypeStruct((t, D_MODEL), F32),
        scratch_shapes=[pltpu.VMEM((tm, D_MODEL), BF16)],
        compiler_params=_cparams("parallel", "arbitrary", vmem=VMEM_LIMIT_FFN),
        name="ffn",
    )(x, g, w1, w3, w2, fg)


def _rope_slot(x, cos_t, sin_t):
    lane = lax.broadcasted_iota(jnp.int32, x.shape, 1)
    first = (lane >= QK_NOPE) & (lane < QK_NOPE + QK_ROPE // 2)
    partner = jnp.where(first, pltpu.roll(x, SLOT - QK_ROPE // 2, 1), pltpu.roll(x, QK_ROPE // 2, 1))
    return x * cos_t + partner * sin_t


def _proj_bcd_kernel(h_ref, w_ref, qn_ref, kvn_ref, wq_ref, wk_ref, wv_ref, cq_ref, sq_ref, ck_ref, sk_ref,
                     pc_out, pd_out, q_out, k_out, vt_out):
    r = jnp.dot(h_ref[...], w_ref[...], preferred_element_type=F32)
    pc_out[...] = r[:, :C_COLS].astype(BF16)
    pd_out[...] = r[:, C_COLS + B_PAD_COLS:]
    pb = r[:, C_COLS:C_COLS + B_PAD_COLS]
    cq = _rms(pb[:, :Q_LORA], qn_ref[...]).astype(BF16)
    ckv = _rms(pb[:, Q_LORA:Q_LORA + KV_LORA], kvn_ref[...]).astype(BF16)
    kr = pb[:, Q_LORA + KV_LORA:]
    q = jnp.dot(cq, wq_ref[...], preferred_element_type=F32)
    k = jnp.dot(ckv, wk_ref[...], preferred_element_type=F32)
    vt_out[0] = jnp.dot(ckv, wv_ref[...], preferred_element_type=F32).T.astype(BF16)
    kr = _rope_slot(kr, ck_ref[...], sk_ref[...])
    for h in range(B_HEADS):
        sl = slice(h * SLOT, (h + 1) * SLOT)
        q_out[:, sl] = _rope_slot(q[:, sl], cq_ref[...], sq_ref[...]).astype(BF16)
        k_out[:, sl] = (k[:, sl] + kr).astype(BF16)


def _proj_bcd(h, w, qn, kvn, wq, wk, wv, tabs, batch, seq):
    t, n = h.shape[0], w.shape[1]
    tm = _pick(seq, TILE_ROWS)
    nseq = seq // tm
    row = lambda i: (i, 0)
    fixed = lambda i: (0, 0)
    tab = pl.BlockSpec((tm, SLOT), lambda i: (i % nseq, 0))
    return pl.pallas_call(
        _proj_bcd_kernel,
        grid=(t // tm,),
        in_specs=[pl.BlockSpec((tm, D_MODEL), row), pl.BlockSpec((D_MODEL, n), fixed),
                  pl.BlockSpec((1, Q_LORA), fixed), pl.BlockSpec((1, KV_LORA), fixed),
                  pl.BlockSpec((Q_LORA, B_HEADS * SLOT), fixed),
                  pl.BlockSpec((KV_LORA, B_HEADS * SLOT), fixed),
                  pl.BlockSpec((KV_LORA, B_HEADS * V_DIM), fixed),
                  tab, tab, tab, tab],
        out_specs=[pl.BlockSpec((tm, C_COLS), row), pl.BlockSpec((tm, D_COLS), row),
                   pl.BlockSpec((tm, B_HEADS * SLOT), row), pl.BlockSpec((tm, B_HEADS * SLOT), row),
                   pl.BlockSpec((1, B_HEADS * V_DIM, tm), lambda i: (i // nseq, 0, i % nseq))],
        out_shape=[jax.ShapeDtypeStruct((t, C_COLS), BF16), jax.ShapeDtypeStruct((t, D_COLS), F32),
                   jax.ShapeDtypeStruct((t, B_HEADS * SLOT), BF16), jax.ShapeDtypeStruct((t, B_HEADS * SLOT), BF16),
                   jax.ShapeDtypeStruct((batch, B_HEADS * V_DIM, seq), BF16)],
        compiler_params=_cparams("parallel"),
        name="proj_bcd",
    )(h, w, qn, kvn, wq, wk, wv, *tabs)


def _proj_a_kernel(x_ref, g_ref, w_ref, o1_ref, o2_ref, o3_ref, hout_ref, hn_ref, hm_ref, h_ref, *, tm):
    j = pl.program_id(1)
    chunks = D_MODEL // LANES

    @pl.when(j == 0)
    def _():
        hn = _rms(x_ref[...], g_ref[...])
        for c in range(chunks):
            hn_ref[c] = hn[:, c * LANES:(c + 1) * LANES]
        h_ref[...] = hn.astype(BF16)
        hout_ref[...] = hn.astype(BF16)

    for g, o_ref in enumerate((o1_ref, o2_ref, o3_ref)):
        dil = DIL_CONFIGS[g][1]
        n = tm // dil

        @pl.when(j == g)
        def _(g=g, o_ref=o_ref, dil=dil, n=n):
            if g == 1:
                for r in range(dil):
                    for c in range(chunks):
                        rows = hn_ref[c, pl.ds(r, n, stride=dil), :]
                        hm_ref[c, r * n:(r + 1) * n, :] = rows
                        h_ref[r * n:(r + 1) * n, c * LANES:(c + 1) * LANES] = rows.astype(BF16)
            elif g == 2:
                prev = DIL_CONFIGS[g - 1][1]
                ratio, n_prev = dil // prev, tm // prev
                for r in range(dil):
                    for c in range(chunks):
                        rows = hm_ref[c, pl.ds((r % prev) * n_prev + r // prev, n, stride=ratio), :]
                        h_ref[r * n:(r + 1) * n, c * LANES:(c + 1) * LANES] = rows.astype(BF16)
            res = jnp.dot(h_ref[...], w_ref[...], preferred_element_type=F32)
            for r in range(dil):
                o_ref[0, r] = res[r * n:(r + 1) * n].astype(BF16)


def _proj_a(x, g, w, batch, seq):
    t = x.shape[0]
    tm = _pick(seq, TILE_ROWS)
    nseq = seq // tm
    gw = A_COLS // N_DIL
    out_specs = [pl.BlockSpec((1, dil, tm // dil, gw), lambda i, j: (i // nseq, 0, i % nseq, 0))
                 for _, dil in DIL_CONFIGS]
    out_shape = [jax.ShapeDtypeStruct((batch, dil, seq // dil, gw), BF16) for _, dil in DIL_CONFIGS]
    out_specs.append(pl.BlockSpec((tm, D_MODEL), lambda i, j: (i, 0)))
    out_shape.append(jax.ShapeDtypeStruct((t, D_MODEL), BF16))
    return pl.pallas_call(
        functools.partial(_proj_a_kernel, tm=tm),
        grid=(t // tm, N_DIL),
        in_specs=[pl.BlockSpec((tm, D_MODEL), lambda i, j: (i, 0)),
                  pl.BlockSpec((1, D_MODEL), lambda i, j: (0, 0)),
                  pl.BlockSpec((D_MODEL, gw), lambda i, j: (0, j))],
        out_specs=out_specs,
        out_shape=out_shape,
        scratch_shapes=[pltpu.VMEM((D_MODEL // LANES, tm, LANES), F32), pltpu.VMEM((D_MODEL // LANES, tm, LANES), F32),
                        pltpu.VMEM((tm, D_MODEL), BF16)],
        compiler_params=_cparams("parallel", "arbitrary"),
        name="proj_a",
    )(x, g, w)


def _band_kernel(*refs, n_heads, n_kv, radius, sb, tq, has_sink, emit_lse):
    if has_sink:
        sink_ref, refs = refs[0], refs[1:]
    bias_ref, q_ref, kp_ref, kc_ref, kn_ref, vp_ref, vc_ref, vn_ref = refs[:8]
    o_ref = refs[8]
    kx_ref, vx_ref, vt_ref, qt_ref = refs[-4:]
    for x_ref, (p_ref, c_ref, n_ref) in ((kx_ref, (kp_ref, kc_ref, kn_ref)), (vx_ref, (vp_ref, vc_ref, vn_ref))):
        x_ref[0:radius, :] = p_ref[0, 0]
        x_ref[radius:radius + tq, :] = c_ref[0, 0]
        x_ref[radius + tq:, :] = n_ref[0, 0]
    vt_ref[...] = vx_ref[...].astype(F32).T.astype(BF16)
    qt_ref[...] = (q_ref[0, 0].astype(F32) * (HEAD_DIM ** -0.5)).T.astype(BF16)
    w = sb + 2 * radius
    rep = n_heads // n_kv
    heads = range(n_heads)
    zeros = jnp.zeros((HEAD_DIM, sb), BF16)
    nsub = tq // sb
    tile, last_tile = pl.program_id(2), pl.num_programs(2) - 1
    for u in range(nsub):
        rows = slice(u * sb, (u + 1) * sb)
        win = slice(u * sb, u * sb + w)
        kind = 0
        if u == 0:
            kind = kind + (tile == 0).astype(jnp.int32)
        if u == nsub - 1:
            kind = kind + 2 * (tile == last_tile).astype(jnp.int32)
        scores = []
        for h in heads:
            g = h // rep
            qh = qt_ref[h * HEAD_DIM:(h + 1) * HEAD_DIM, rows]
            rhs = jnp.concatenate([qh, zeros] if g % 2 == 0 else [zeros, qh], axis=0)
            pair = slice((g // 2) * 2 * HEAD_DIM, (g // 2 + 1) * 2 * HEAD_DIM)
            scores.append(jnp.dot(kx_ref[win, pair], rhs, preferred_element_type=F32) + bias_ref[kind, h])
        probs, stats = [], []
        for h in heads:
            m = jnp.max(scores[h], axis=0, keepdims=True)
            if has_sink:
                m = jnp.maximum(m, sink_ref[h])
            p = jnp.exp(scores[h] - m)
            l = jnp.sum(p, axis=0, keepdims=True)
            if has_sink:
                l = l + jnp.exp(sink_ref[h] - m)
            probs.append(p.astype(BF16))
            stats.append((m, l))
        outs, lses = [], []
        for h in heads:
            g = h // rep
            m, l = stats[h]
            outs.append(jnp.dot(vt_ref[g * HEAD_DIM:(g + 1) * HEAD_DIM, win], probs[h],
                                preferred_element_type=F32) / l)
            lses.append(jnp.broadcast_to(m + jnp.log(l), (HEAD_DIM, sb)))
        o_ref[0, 0, rows, :] = jnp.concatenate(outs, axis=0).T.astype(o_ref.dtype)
        if emit_lse:
            refs[9][0, 0, rows, :] = jnp.concatenate(lses, axis=0).T


def _alibi(n):
    return np.asarray(2.0 ** (-8.0 * np.arange(1, n + 1) / n), dtype=np.float32)


def _band_bias(n_heads, radius, step, sb):
    w = sb + 2 * radius
    shape = (4, n_heads, w, sb)
    kind, h, kj, qi = (lax.broadcasted_iota(jnp.int32, shape, d) for d in range(4))
    rel = jnp.abs(kj - radius - qi)
    kpos = kj - radius
    ok = (rel <= radius) & ((kpos >= 0) | (kind % 2 == 0)) & ((kpos < sb) | (kind < 2))
    slopes = jnp.asarray(_alibi(n_heads))[h]
    return jnp.where(ok, -slopes * (step * rel).astype(F32), NEG)


def _band_attention(arr, *, q_col, k_col, v_col, kv_width, n_heads, n_kv, radius, step, sink, out_dtype,
                    emit_lse):
    batch, n_classes, length, _ = arr.shape
    qw = n_heads * HEAD_DIM
    tq = _pick(length, ATTN_TILE_Q)
    sb = min(tq, BAND_BLOCK_Q)
    nblk = length // tq

    per = tq // radius
    last_halo = length // radius - 1

    def kv_spec(col, shift):
        if shift == 0:
            return pl.BlockSpec((1, 1, tq, kv_width), lambda b, r, i: (b, r, i, col))
        if shift < 0:
            return pl.BlockSpec((1, 1, radius, kv_width), lambda b, r, i: (b, r, jnp.maximum(i * per - 1, 0), col))
        return pl.BlockSpec((1, 1, radius, kv_width),
                            lambda b, r, i: (b, r, jnp.minimum((i + 1) * per, last_halo), col))

    bias = _band_bias(n_heads, radius, step, sb)
    in_specs = [pl.BlockSpec(bias.shape, lambda b, r, i: (0, 0, 0, 0)),
                pl.BlockSpec((1, 1, tq, qw), lambda b, r, i: (b, r, i, q_col)),
                kv_spec(k_col, -1), kv_spec(k_col, 0), kv_spec(k_col, 1),
                kv_spec(v_col, -1), kv_spec(v_col, 0), kv_spec(v_col, 1)]
    args = [bias] + [arr] * 7
    if sink is not None:
        in_specs = [pl.BlockSpec(memory_space=pltpu.SMEM)] + in_specs
        args = [sink] + args
    o_spec = pl.BlockSpec((1, 1, tq, qw), lambda b, r, i: (b, r, i, 0))
    o_shape = jax.ShapeDtypeStruct((batch, n_classes, length, qw), out_dtype)
    out_specs, out_shape = [o_spec], [o_shape]
    if emit_lse:
        out_specs, out_shape = [o_spec, o_spec], [o_shape, jax.ShapeDtypeStruct(o_shape.shape, F32)]
    return pl.pallas_call(
        functools.partial(_band_kernel, n_heads=n_heads, n_kv=n_kv, radius=radius, sb=sb, tq=tq,
                          has_sink=sink is not None, emit_lse=emit_lse),
        grid=(batch, n_classes, nblk),
        in_specs=in_specs,
        out_specs=out_specs,
        out_shape=out_shape,
        scratch_shapes=[pltpu.VMEM((tq + 2 * radius, kv_width), BF16), pltpu.VMEM((tq + 2 * radius, kv_width), BF16),
                        pltpu.VMEM((kv_width, tq + 2 * radius), BF16), pltpu.VMEM((qw, tq), BF16)],
        compiler_params=_cparams("parallel", "parallel", "arbitrary"),
        name="band_attn",
    )(*args)


def _mla_attn_kernel(q_ref, k_ref, vt_ref, o_ref, st_ref, *, seq, kc):
    chunks = [slice(c * kc, (c + 1) * kc) for c in range(seq // kc)]

    def qk(h, rows):
        sl = slice(h * SLOT, (h + 1) * SLOT)
        st = lax.dot_general(k_ref[0, rows, sl], q_ref[0, :, sl], (((1,), (1,)), ((), ())),
                             preferred_element_type=F32)
        st_ref[h % 2, rows, :] = st
        return jnp.max(st, axis=0, keepdims=True)

    def pv(h, rows, m):
        p = jnp.exp2(st_ref[h % 2, rows, :] - m)
        return (jnp.sum(p, axis=0, keepdims=True),
                jnp.dot(vt_ref[0, h * V_DIM:(h + 1) * V_DIM, rows], p.astype(BF16), preferred_element_type=F32))

    def add(acc, new):
        return new if acc is None else (acc[0] + new[0], acc[1] + new[1])

    n_heads = q_ref.shape[2] // SLOT
    m = functools.reduce(jnp.maximum, [qk(0, rows) for rows in chunks])
    outs = []
    for h in range(n_heads):
        acc = m_next = None
        for rows in chunks:
            acc = add(acc, pv(h, rows, m))
            if h + 1 < n_heads:
                mc = qk(h + 1, rows)
                m_next = mc if m_next is None else jnp.maximum(m_next, mc)
        outs.append(acc[1] / acc[0])
        m = m_next
    o_ref[0] = jnp.concatenate(outs, axis=0).T.astype(o_ref.dtype)


def _mla_attn(q, k, vt, batch, seq):
    tq = _pick(seq, ATTN_TILE_Q)
    kc = _pick(seq, MLA_KEY_CHUNK)
    hg = B_HEADS // 2
    return pl.pallas_call(
        functools.partial(_mla_attn_kernel, seq=seq, kc=kc),
        scratch_shapes=[pltpu.VMEM((2, seq, tq), F32)],
        grid=(batch, B_HEADS // hg, seq // tq),
        in_specs=[pl.BlockSpec((1, tq, hg * SLOT), lambda b, g, i: (b, i, g)),
                  pl.BlockSpec((1, seq, hg * SLOT), lambda b, g, i: (b, 0, g)),
                  pl.BlockSpec((1, hg * V_DIM, seq), lambda b, g, i: (b, g, 0))],
        out_specs=pl.BlockSpec((1, tq, hg * V_DIM), lambda b, g, i: (b, i, g)),
        out_shape=jax.ShapeDtypeStruct((batch, seq, B_HEADS * V_DIM), BF16),
        compiler_params=_cparams("parallel", "parallel", "arbitrary"),
        name="mla_attn",
    )(q, k, vt)


def _sigmoid_tanh(x):
    return 0.5 * (jnp.tanh(0.5 * x) + 1.0)


def _softplus(x):
    return jnp.maximum(x, 0.0) + jnp.log1p(jnp.exp(-jnp.abs(x)))


def _lru_kernel(xf_ref, xfp_ref, xfn_ref, xr_ref, xrp_ref, xrn_ref, cw_ref, cb_ref, wg_ref, bg_ref, lam_ref,
                hf_ref, hr_ref, ext_ref, af_ref, uf_ref, ar_ref, ur_ref, hfs_ref, hrs_ref, cf_ref, cr_ref,
                *, ts, pitch):
    i = pl.program_id(0)
    nt = pl.num_programs(0)
    halo = SUBLANES
    nb = xf_ref.shape[0]
    slabs = LRU_WIDTH // LANES

    @pl.when(i == 0)
    def _():
        cf_ref[...] = jnp.zeros_like(cf_ref)
        cr_ref[...] = jnp.zeros_like(cr_ref)

    def gates(d, tile, x_ref, xp_ref, xn_ref, a_ref, u_ref):
        def one_batch(b, carry):
            ext_ref[0:halo, :] = jnp.where(tile > 0, xp_ref[b], 0.0)
            ext_ref[halo:halo + ts, :] = x_ref[b]
            ext_ref[halo + ts:, :] = jnp.where(tile < nt - 1, xn_ref[b], 0.0)
            xc = cb_ref[...]
            for tap in range(CONV_W):
                off = halo - CONV_PAD_L + tap
                xc = xc + ext_ref[off:off + ts, :] * cw_ref[tap:tap + 1, :]
            g = jnp.dot(xc.astype(BF16), wg_ref[d], preferred_element_type=F32) + bg_ref[d:d + 1, :]
            r = _sigmoid_tanh(g[:, :LRU_WIDTH])
            ig = _sigmoid_tanh(g[:, LRU_WIDTH:])
            log_a = -LRU_C * r * _softplus(-lam_ref[d:d + 1, :])
            a = jnp.exp(log_a)
            u = jnp.sqrt(-jnp.tanh(log_a) * (1.0 + a * a)) * (ig * xc)
            rows = pl.ds(pl.multiple_of(b * pitch, SUBLANES), ts)
            for c in range(slabs):
                a_ref[c, rows, :] = a[:, c * LANES:(c + 1) * LANES]
                u_ref[c, rows, :] = u[:, c * LANES:(c + 1) * LANES]
            return carry
        lax.fori_loop(0, nb, one_batch, 0)

    gates(0, i, xf_ref, xfp_ref, xfn_ref, af_ref, uf_ref)
    gates(1, nt - 1 - i, xr_ref, xrp_ref, xrn_ref, ar_ref, ur_ref)

    def time_step(t, carry):
        hf, hr = carry
        fwd = pl.ds(t, nb, stride=pitch)
        rev = pl.ds(ts - 1 - t, nb, stride=pitch)
        new_f, new_r = [], []
        for c in range(slabs):
            h = af_ref[c, fwd, :] * hf[c] + uf_ref[c, fwd, :]
            hfs_ref[c, fwd, :] = h
            new_f.append(h)
            h = ar_ref[c, rev, :] * hr[c] + ur_ref[c, rev, :]
            hrs_ref[c, rev, :] = h
            new_r.append(h)
        return tuple(new_f), tuple(new_r)

    init = (tuple(cf_ref[c] for c in range(slabs)), tuple(cr_ref[c] for c in range(slabs)))
    hf, hr = lax.fori_loop(0, ts, time_step, init, unroll=8)
    for c in range(slabs):
        cf_ref[c] = hf[c]
        cr_ref[c] = hr[c]
        for b in range(nb):
            hf_ref[b, :, c * LANES:(c + 1) * LANES] = hfs_ref[c, b * pitch:b * pitch + ts, :]
            hr_ref[b, :, c * LANES:(c + 1) * LANES] = hrs_ref[c, b * pitch:b * pitch + ts, :]


def _lru(pd, cw, cb, wg, bg, lam):
    batch, seq, _ = pd.shape
    ts = _pick(seq, LRU_TILE_STEPS)
    nt = seq // ts
    per = ts // SUBLANES
    last8 = seq // SUBLANES - 1
    pitch = ts + SUBLANES

    def tile_specs(tile):
        return [pl.BlockSpec((batch, ts, LRU_WIDTH), lambda i: (0, tile(i), 0)),
                pl.BlockSpec((batch, SUBLANES, LRU_WIDTH), lambda i: (0, jnp.maximum(tile(i) * per - 1, 0), 0)),
                pl.BlockSpec((batch, SUBLANES, LRU_WIDTH), lambda i: (0, jnp.minimum((tile(i) + 1) * per, last8), 0))]

    fwd_tile = lambda i: i
    rev_tile = lambda i: nt - 1 - i
    fixed2 = lambda i: (0, 0)
    shp = jax.ShapeDtypeStruct((batch, seq, LRU_WIDTH), F32)
    slab = pltpu.VMEM((LRU_WIDTH // LANES, batch * pitch, LANES), F32)
    carry = pltpu.VMEM((LRU_WIDTH // LANES, batch, LANES), F32)
    return pl.pallas_call(
        functools.partial(_lru_kernel, ts=ts, pitch=pitch),
        grid=(nt,),
        in_specs=tile_specs(fwd_tile) + tile_specs(rev_tile)
                 + [pl.BlockSpec((CONV_W, LRU_WIDTH), fixed2), pl.BlockSpec((1, LRU_WIDTH), fixed2),
                    pl.BlockSpec((2, LRU_WIDTH, 2 * LRU_WIDTH), lambda i: (0, 0, 0)),
                    pl.BlockSpec((2, 2 * LRU_WIDTH), fixed2), pl.BlockSpec((2, LRU_WIDTH), fixed2)],
        out_specs=[pl.BlockSpec((batch, ts, LRU_WIDTH), lambda i: (0, i, 0)),
                   pl.BlockSpec((batch, ts, LRU_WIDTH), lambda i: (0, nt - 1 - i, 0))],
        out_shape=[shp, shp],
        scratch_shapes=[pltpu.VMEM((ts + 2 * SUBLANES, LRU_WIDTH), F32), slab, slab, slab, slab, slab, slab,
                        carry, carry],
        compiler_params=_cparams("arbitrary"),
        name="lru",
    )(pd, pd, pd, pd, pd, pd, cw, cb, wg, bg, lam)


def _gelu_tanh(x):
    return 0.5 * x * (1.0 + jnp.tanh(np.sqrt(2.0 / np.pi).astype(np.float32) * (x + 0.044715 * (x * x * x))))


def _branch_out_kernel(o0, o1, o2, l0, l1, l2, hf_ref, hr_ref, gd_ref, ya_ref, yd_ref, *nat_refs, tm):
    def natural(ref, g, buf):
        dil = DIL_CONFIGS[g][1]
        if dil == 1:
            return ref[0, 0]
        chunks = BRANCH_W // LANES
        for r in range(dil):
            for c in range(chunks):
                buf[c, pl.ds(r, tm // dil, stride=dil), :] = ref[0, r, :, c * LANES:(c + 1) * LANES]
        return jnp.concatenate([buf[c] for c in range(chunks)], axis=1)

    outs = (o0[0, 0], natural(o1, 1, nat_refs[0]), natural(o2, 2, nat_refs[1]))
    lse = (l0[0, 0], natural(l1, 1, nat_refs[2]), natural(l2, 2, nat_refs[3]))
    m = jnp.maximum(jnp.maximum(lse[0], lse[1]), lse[2])
    e = [jnp.exp(v - m) for v in lse]
    tot = e[0] + e[1] + e[2]
    ya = (e[0] / tot) * outs[0] + (e[1] / tot) * outs[1] + (e[2] / tot) * outs[2]
    ya_ref[...] = ya.astype(BF16)
    yd_ref[...] = ((hf_ref[...] + hr_ref[...]) * _gelu_tanh(gd_ref[...])).astype(BF16)


def _branch_out(oa, la, hf, hr, pd, seq):
    t = hf.shape[0]
    tm = _pick(seq, TILE_ROWS)
    nseq = seq // tm
    blk = pl.BlockSpec((tm, BRANCH_W), lambda i: (i, 0))
    cls = [pl.BlockSpec((1, dil, tm // dil, BRANCH_W), lambda i: (i // nseq, 0, i % nseq, 0))
           for _, dil in DIL_CONFIGS]
    shp = jax.ShapeDtypeStruct((t, BRANCH_W), BF16)
    return pl.pallas_call(
        functools.partial(_branch_out_kernel, tm=tm),
        grid=(t // tm,),
        in_specs=cls + cls + [blk, blk, pl.BlockSpec((tm, BRANCH_W), lambda i: (i, 1))],
        out_specs=[blk, blk],
        out_shape=[shp, shp],
        scratch_shapes=[pltpu.VMEM((BRANCH_W // LANES, tm, LANES), F32)] * 4,
        compiler_params=_cparams("parallel"),
        name="branch_out",
    )(*oa, *la, hf, hr, pd)


def _gate_merge_out_kernel(x_ref, h_ref, ya, yb, yc, yd, g0, g1, g2, g3, wb_ref, wo_ref, o_ref):
    j = pl.program_id(1)
    last = pl.num_programs(1) - 1

    def partial_out():
        h = h_ref[...]
        merged = None
        for nbr, (y_ref, wg_ref) in enumerate(zip((ya, yb, yc, yd), (g0, g1, g2, g3))):
            gate = jax.nn.sigmoid(jnp.dot(h, wg_ref[...], preferred_element_type=F32))
            term = gate * jnp.dot(y_ref[...], wb_ref[nbr], preferred_element_type=F32)
            merged = term if merged is None else merged + term
        return jnp.dot(merged.astype(BF16), wo_ref[...], preferred_element_type=F32)

    @pl.when(j == 0)
    def _():
        o_ref[...] = partial_out()

    @pl.when((j > 0) & (j < last))
    def _():
        o_ref[...] += partial_out()

    @pl.when(j == last)
    def _():
        o_ref[...] = x_ref[...] + (o_ref[...] + partial_out())


def _gate_merge_out(x, h, ys, wg, wb, wo):
    t = x.shape[0]
    tm, tn = _pick(t, TILE_ROWS), MERGE_TILE_COLS
    nj = D_MODEL // tn
    row = lambda i, j: (i, 0)
    gate_specs = [pl.BlockSpec((D_MODEL, tn), functools.partial(lambda i, j, nbr: (0, nbr * nj + j), nbr=nbr))
                  for nbr in range(N_BRANCH)]
    return pl.pallas_call(
        _gate_merge_out_kernel,
        grid=(t // tm, nj),
        in_specs=[pl.BlockSpec((tm, D_MODEL), row), pl.BlockSpec((tm, D_MODEL), row)]
                 + [pl.BlockSpec((tm, BRANCH_W), row)] * N_BRANCH + gate_specs
                 + [pl.BlockSpec((N_BRANCH, BRANCH_W, tn), lambda i, j: (0, 0, j)),
                    pl.BlockSpec((tn, D_MODEL), lambda i, j: (j, 0))],
        out_specs=pl.BlockSpec((tm, D_MODEL), row),
        out_shape=jax.ShapeDtypeStruct((t, D_MODEL), F32),
        compiler_params=_cparams("parallel", "arbitrary"),
        name="gate_merge_out",
    )(x, h, *ys, wg, wg, wg, wg, wb, wo)


def _block_diag(w):
    eye = jnp.eye(LRU_BLOCKS, dtype=w.dtype)
    return jnp.einsum('nef,nm->nemf', w, eye).reshape(LRU_WIDTH, LRU_WIDTH)


def _prep_layer(l, ffn1_norm, ffn1_w1, ffn1_w3, ffn1_w2, mix_norm, w_in, mla_q_norm, mla_w_uq, mla_kv_norm,
                mla_w_ukv, lru_conv_w, lru_conv_b, lru_w_a, lru_b_a, lru_w_x, lru_b_x, lru_lambda, sink_logits,
                w_branch, w_out, ffn2_norm, ffn2_w1, ffn2_w3, ffn2_w2):
    row = lambda v: v.reshape(1, -1)
    wi = w_in[l]
    c0, c1, c2, c3 = A_COLS, A_COLS + B_COLS, A_COLS + B_COLS + C_COLS, A_COLS + B_COLS + C_COLS + D_COLS
    w_a = wi[:, :c0].reshape(D_MODEL, 3, N_DIL, A_HEADS * HEAD_DIM).transpose(0, 2, 1, 3).reshape(D_MODEL, A_COLS)
    wb = wi[:, c0:c1]
    zeros = lambda n: jnp.zeros((D_MODEL, n), wi.dtype)
    w_b = jnp.concatenate([wb[:, :Q_LORA + KV_LORA], zeros(QK_NOPE), wb[:, Q_LORA + KV_LORA:],
                           zeros(SLOT - QK_NOPE - QK_ROPE)], axis=1)
    w_cbd = jnp.concatenate([wi[:, c1:c2], w_b, wi[:, c2:c3]], axis=1)
    wq = mla_w_uq[l].reshape(Q_LORA, B_HEADS, QK_NOPE + QK_ROPE)
    wq = jnp.pad(wq, ((0, 0), (0, 0), (0, SLOT - QK_NOPE - QK_ROPE))).reshape(Q_LORA, B_HEADS * SLOT)
    wkv = mla_w_ukv[l].reshape(KV_LORA, B_HEADS, QK_NOPE + V_DIM)
    wk = jnp.pad(wkv[:, :, :QK_NOPE], ((0, 0), (0, 0), (0, SLOT - QK_NOPE))).reshape(KV_LORA, B_HEADS * SLOT)
    wv = wkv[:, :, QK_NOPE:].reshape(KV_LORA, B_HEADS * V_DIM)
    wg = jnp.stack([jnp.concatenate([_block_diag(lru_w_a[l, d]), _block_diag(lru_w_x[l, d])], axis=1)
                    for d in range(2)])
    bg = jnp.concatenate([lru_b_a[l], lru_b_x[l]], axis=1)
    return dict(
        ffn1=(row(ffn1_norm[l]), ffn1_w1[l].astype(BF16), ffn1_w3[l].astype(BF16), ffn1_w2[l].astype(BF16)),
        ffn2=(row(ffn2_norm[l]), ffn2_w1[l].astype(BF16), ffn2_w3[l].astype(BF16), ffn2_w2[l].astype(BF16)),
        mix_norm=row(mix_norm[l]),
        w_a=w_a.astype(BF16), w_cbd=w_cbd.astype(BF16), w_g=wi[:, c3:].astype(BF16),
        qn=row(mla_q_norm[l]), kvn=row(mla_kv_norm[l]),
        wq=wq.astype(BF16), wk=wk.astype(BF16), wv=wv.astype(BF16),
        conv_w=lru_conv_w[l], conv_b=row(lru_conv_b[l]), wg=wg.astype(BF16), bg=bg, lam=lru_lambda[l],
        sink=sink_logits[l], w_branch=w_branch[l].astype(BF16), w_out=w_out[l].astype(BF16),
    )


def _rope_slot_tables(seq):
    inv = ROPE_THETA ** (-jnp.arange(0, QK_ROPE, 2, dtype=F32) / QK_ROPE)
    ang = jnp.arange(seq, dtype=F32)[:, None] * inv[None, :]
    cos, sin = jnp.cos(ang), jnp.sin(ang)
    scale = (QK_NOPE + QK_ROPE) ** -0.5 * np.log2(np.e)
    z = lambda n: jnp.zeros((seq, n), F32)
    tail = SLOT - QK_NOPE - QK_ROPE
    cos_q = jnp.concatenate([jnp.full((seq, QK_NOPE), scale, F32), cos * scale, cos * scale, z(tail)], axis=1)
    sin_q = jnp.concatenate([z(QK_NOPE), -sin * scale, sin * scale, z(tail)], axis=1)
    cos_k = jnp.concatenate([z(QK_NOPE), cos, cos, z(tail)], axis=1)
    sin_k = jnp.concatenate([z(QK_NOPE), -sin, sin, z(tail)], axis=1)
    return cos_q, sin_q, cos_k, sin_k


def _layer(x, w, batch, seq, final_g):
    t = batch * seq
    x = _ffn(x, *w['ffn1'], final_g, False)
    *pas, h = _proj_a(x, w['mix_norm'], w['w_a'], batch, seq)
    pc, pd, q, k, vt = _proj_bcd(h, w['w_cbd'], w['qn'], w['kvn'], w['wq'], w['wk'], w['wv'],
                                 _rope_slot_tables(seq), batch, seq)

    oa, la = [], []
    for pa, (window, dil) in zip(pas, DIL_CONFIGS):
        o, lse = _band_attention(
            pa, q_col=0, k_col=1, v_col=2, kv_width=A_HEADS * HEAD_DIM,
            n_heads=A_HEADS, n_kv=A_HEADS, radius=window // (2 * dil), step=dil, sink=None,
            out_dtype=F32, emit_lse=True)
        oa.append(o)
        la.append(lse)

    yb = _mla_attn(q.reshape(batch, seq, -1), k.reshape(batch, seq, -1), vt, batch, seq)

    kvw = C_KV_HEADS * HEAD_DIM
    yc = _band_attention(
        pc.reshape(batch, 1, seq, C_COLS),
        q_col=0, k_col=C_HEADS * HEAD_DIM // kvw, v_col=C_HEADS * HEAD_DIM // kvw + 1, kv_width=kvw,
        n_heads=C_HEADS, n_kv=C_KV_HEADS, radius=C_RADIUS, step=1, sink=w['sink'],
        out_dtype=BF16, emit_lse=False)[0]

    hf, hr = _lru(pd.reshape(batch, seq, D_COLS), w['conv_w'], w['conv_b'], w['wg'], w['bg'], w['lam'])

    ya, yd = _branch_out(oa, la, hf.reshape(t, LRU_WIDTH), hr.reshape(t, LRU_WIDTH), pd, seq)
    x = _gate_merge_out(x, h, (ya, yb.reshape(t, BRANCH_W), yc.reshape(t, BRANCH_W), yd),
                        w['w_g'], w['w_branch'], w['w_out'])
    return _ffn(x, *w['ffn2'], final_g, final_g is not None and w.get('last', False))


def _trunk(x, layers, final_norm):
    batch, seq, _ = x.shape
    h = x.reshape(batch * seq, D_MODEL)
    fg = final_norm.reshape(1, -1)
    for l, w in enumerate(layers):
        h = _layer(h, dict(w, last=(l == len(layers) - 1)), batch, seq, fg)
    return h.reshape(batch, seq, D_MODEL)


def kernel(x_prompt, x_sample, ffn1_norm, ffn1_w1, ffn1_w3, ffn1_w2, mix_norm, w_in, mla_q_norm, mla_w_uq, mla_kv_norm, mla_w_ukv, lru_conv_w, lru_conv_b, lru_w_a, lru_b_a, lru_w_x, lru_b_x, lru_lambda, sink_logits, w_branch, w_out, ffn2_norm, ffn2_w1, ffn2_w3, ffn2_w2, final_norm):
    layers = [_prep_layer(l, ffn1_norm, ffn1_w1, ffn1_w3, ffn1_w2, mix_norm, w_in, mla_q_norm, mla_w_uq,
                          mla_kv_norm, mla_w_ukv, lru_conv_w, lru_conv_b, lru_w_a, lru_b_a, lru_w_x, lru_b_x,
                          lru_lambda, sink_logits, w_branch, w_out, ffn2_norm, ffn2_w1, ffn2_w3, ffn2_w2)
              for l in range(DEPTH)]
    return (_trunk(x_prompt, layers, final_norm), _trunk(x_sample, layers, final_norm))
```

```python
import functools

import numpy as np
import jax
import jax.numpy as jnp
from jax import lax
from jax.experimental import pallas as pl
from jax.experimental.pallas import tpu as pltpu

F32 = jnp.float32
BF16 = jnp.bfloat16

D_MODEL = 2048
DEPTH = 2
HEAD_DIM = 64
N_BRANCH = 4
BRANCH_W = 512
DIL_CONFIGS = ((128, 1), (512, 4), (2048, 16))
N_DIL = 3
A_HEADS = 8
B_HEADS = 8
Q_LORA = 384
KV_LORA = 128
QK_NOPE = 64
QK_ROPE = 32
V_DIM = 64
ROPE_THETA = 10000.0
C_HEADS = 8
C_KV_HEADS = 2
C_RADIUS = 128
LRU_WIDTH = 512
LRU_BLOCKS = 8
CONV_W = 4
CONV_PAD_L = 2
LRU_C = 8.0
D_FF = 5632
EPS = 1e-6
NEG = -1e30

A_COLS = 3 * N_DIL * A_HEADS * HEAD_DIM
B_COLS = Q_LORA + KV_LORA + QK_ROPE
C_COLS = (C_HEADS + 2 * C_KV_HEADS) * HEAD_DIM
D_COLS = 2 * LRU_WIDTH

LANES = 128
SUBLANES = 8
SLOT = LANES
B_PAD_COLS = Q_LORA + KV_LORA + SLOT
TILE_ROWS = 512
FFN_TILE_ROWS = 1024
FFN_TILE_HIDDEN = 512
MERGE_TILE_COLS = 512
ATTN_TILE_Q = 512
BAND_BLOCK_Q = 128
MLA_KEY_CHUNK = 256
LRU_TILE_STEPS = 128
VMEM_LIMIT = 52 * 1024 * 1024
VMEM_LIMIT_FFN = 56 * 1024 * 1024


def _cparams(*sem, vmem=VMEM_LIMIT):
    return pltpu.CompilerParams(dimension_semantics=sem, vmem_limit_bytes=vmem)


def _rms(x, g):
    return x * lax.rsqrt(jnp.mean(x * x, axis=-1, keepdims=True) + EPS) * g


def _pick(n, pref):
    t = min(n, pref)
    while n % t:
        t //= 2
    return t


def _ffn_kernel(x_ref, g_ref, w1_ref, w3_ref, w2_ref, fg_ref, o_ref, h_ref, *, final_norm):
    j = pl.program_id(1)
    last = pl.num_programs(1) - 1
    tm = x_ref.shape[0]
    halves = (slice(0, tm // 2), slice(tm // 2, tm))

    def down(h):
        a = jnp.dot(h, w1_ref[...], preferred_element_type=F32)
        b = jnp.dot(h, w3_ref[...], preferred_element_type=F32)
        act = (a * jax.nn.sigmoid(a) * b).astype(BF16)
        return jnp.dot(act, w2_ref[...], preferred_element_type=F32)

    @pl.when(j == 0)
    def _():
        for rows in halves:
            h = _rms(x_ref[rows, :], g_ref[...]).astype(BF16)
            h_ref[rows, :] = h
            o_ref[rows, :] = down(h)

    @pl.when((j > 0) & (j < last))
    def _():
        for rows in halves:
            o_ref[rows, :] += down(h_ref[rows, :])

    @pl.when(j == last)
    def _():
        for rows in halves:
            y = x_ref[rows, :] + 0.5 * (o_ref[rows, :] + down(h_ref[rows, :]))
            if final_norm:
                y = _rms(y, fg_ref[...])
            o_ref[rows, :] = y


def _ffn(x, g, w1, w3, w2, fg, final_norm):
    t = x.shape[0]
    tm, tf = _pick(t, FFN_TILE_ROWS), FFN_TILE_HIDDEN
    return pl.pallas_call(
        functools.partial(_ffn_kernel, final_norm=final_norm),
        grid=(t // tm, D_FF // tf),
        in_specs=[
            pl.BlockSpec((tm, D_MODEL), lambda i, j: (i, 0)),
            pl.BlockSpec((1, D_MODEL), lambda i, j: (0, 0)),
            pl.BlockSpec((D_MODEL, tf), lambda i, j: (0, j)),
            pl.BlockSpec((D_MODEL, tf), lambda i, j: (0, j)),
            pl.BlockSpec((tf, D_MODEL), lambda i, j: (j, 0)),
            pl.BlockSpec((1, D_MODEL), lambda i, j: (0, 0)),
        ],
        out_specs=pl.BlockSpec((tm, D_MODEL), lambda i, j: (i, 0)),
        out_shape=jax.ShapeDtypeStruct((t, D_MODEL), F32),
        scratch_shapes=[pltpu.VMEM((tm, D_MODEL), BF16)],
        compiler_params=_cparams("parallel", "arbitrary", vmem=VMEM_LIMIT_FFN),
        name="ffn",
    )(x, g, w1, w3, w2, fg)


def _rope_slot(x, cos_t, sin_t):
    lane = lax.broadcasted_iota(jnp.int32, x.shape, 1)
    first = (lane >= QK_NOPE) & (lane < QK_NOPE + QK_ROPE // 2)
    partner = jnp.where(first, pltpu.roll(x, SLOT - QK_ROPE // 2, 1), pltpu.roll(x, QK_ROPE // 2, 1))
    return x * cos_t + partner * sin_t


def _proj_bcd_kernel(h_ref, w_ref, qn_ref, kvn_ref, wq_ref, wk_ref, wv_ref, cq_ref, sq_ref, ck_ref, sk_ref,
                     pc_out, pd_out, q_out, k_out, vt_out):
    r = jnp.dot(h_ref[...], w_ref[...], preferred_element_type=F32)
    pc_out[...] = r[:, :C_COLS].astype(BF16)
    pd_out[...] = r[:, C_COLS + B_PAD_COLS:]
    pb = r[:, C_COLS:C_COLS + B_PAD_COLS]
    cq = _rms(pb[:, :Q_LORA], qn_ref[...]).astype(BF16)
    ckv = _rms(pb[:, Q_LORA:Q_LORA + KV_LORA], kvn_ref[...]).astype(BF16)
    kr = pb[:, Q_LORA + KV_LORA:]
    q = jnp.dot(cq, wq_ref[...], preferred_element_type=F32)
    k = jnp.dot(ckv, wk_ref[...], preferred_element_type=F32)
    vt_out[0] = jnp.dot(ckv, wv_ref[...], preferred_element_type=F32).T.astype(BF16)
    kr = _rope_slot(kr, ck_ref[...], sk_ref[...])
    for h in range(B_HEADS):
        sl = slice(h * SLOT, (h + 1) * SLOT)
        q_out[0, sl, :] = _rope_slot(q[:, sl], cq_ref[...], sq_ref[...]).T.astype(BF16)
        k_out[:, sl] = (k[:, sl] + kr).astype(BF16)


def _proj_bcd(h, w, qn, kvn, wq, wk, wv, tabs, batch, seq):
    t, n = h.shape[0], w.shape[1]
    tm = _pick(seq, TILE_ROWS)
    nseq = seq // tm
    row = lambda i: (i, 0)
    fixed = lambda i: (0, 0)
    tab = pl.BlockSpec((tm, SLOT), lambda i: (i % nseq, 0))
    return pl.pallas_call(
        _proj_bcd_kernel,
        grid=(t // tm,),
        in_specs=[pl.BlockSpec((tm, D_MODEL), row), pl.BlockSpec((D_MODEL, n), fixed),
                  pl.BlockSpec((1, Q_LORA), fixed), pl.BlockSpec((1, KV_LORA), fixed),
                  pl.BlockSpec((Q_LORA, B_HEADS * SLOT), fixed),
                  pl.BlockSpec((KV_LORA, B_HEADS * SLOT), fixed),
                  pl.BlockSpec((KV_LORA, B_HEADS * V_DIM), fixed),
                  tab, tab, tab, tab],
        out_specs=[pl.BlockSpec((tm, C_COLS), row), pl.BlockSpec((tm, D_COLS), row),
                   pl.BlockSpec((1, B_HEADS * SLOT, tm), lambda i: (i // nseq, 0, i % nseq)),
                   pl.BlockSpec((tm, B_HEADS * SLOT), row),
                   pl.BlockSpec((1, B_HEADS * V_DIM, tm), lambda i: (i // nseq, 0, i % nseq))],
        out_shape=[jax.ShapeDtypeStruct((t, C_COLS), BF16), jax.ShapeDtypeStruct((t, D_COLS), F32),
                   jax.ShapeDtypeStruct((batch, B_HEADS * SLOT, seq), BF16),
                   jax.ShapeDtypeStruct((t, B_HEADS * SLOT), BF16),
                   jax.ShapeDtypeStruct((batch, B_HEADS * V_DIM, seq), BF16)],
        compiler_params=_cparams("parallel"),
        name="proj_bcd",
    )(h, w, qn, kvn, wq, wk, wv, *tabs)


def _proj_a_kernel(x_ref, g_ref, w_ref, o1_ref, o2_ref, o3_ref, hout_ref, hn_ref, hm_ref, h_ref, *, tm):
    j = pl.program_id(1)
    chunks = D_MODEL // LANES

    @pl.when(j == 0)
    def _():
        hn = _rms(x_ref[...], g_ref[...])
        for c in range(chunks):
            hn_ref[c] = hn[:, c * LANES:(c + 1) * LANES]
        h_ref[...] = hn.astype(BF16)
        hout_ref[...] = hn.astype(BF16)

    for g, o_ref in enumerate((o1_ref, o2_ref, o3_ref)):
        dil = DIL_CONFIGS[g][1]
        n = tm // dil

        @pl.when(j == g)
        def _(g=g, o_ref=o_ref, dil=dil, n=n):
            if g == 1:
                for r in range(dil):
                    for c in range(chunks):
                        rows = hn_ref[c, pl.ds(r, n, stride=dil), :]
                        hm_ref[c, r * n:(r + 1) * n, :] = rows
                        h_ref[r * n:(r + 1) * n, c * LANES:(c + 1) * LANES] = rows.astype(BF16)
            elif g == 2:
                prev = DIL_CONFIGS[g - 1][1]
                ratio, n_prev = dil // prev, tm // prev
                for r in range(dil):
                    for c in range(chunks):
                        rows = hm_ref[c, pl.ds((r % prev) * n_prev + r // prev, n, stride=ratio), :]
                        h_ref[r * n:(r + 1) * n, c * LANES:(c + 1) * LANES] = rows.astype(BF16)
            res = jnp.dot(h_ref[...], w_ref[...], preferred_element_type=F32)
            for r in range(dil):
                o_ref[0, r] = res[r * n:(r + 1) * n].astype(BF16)


def _proj_a(x, g, w, batch, seq):
    t = x.shape[0]
    tm = _pick(seq, TILE_ROWS)
    nseq = seq // tm
    gw = A_COLS // N_DIL
    out_specs = [pl.BlockSpec((1, dil, tm // dil, gw), lambda i, j: (i // nseq, 0, i % nseq, 0))
                 for _, dil in DIL_CONFIGS]
    out_shape = [jax.ShapeDtypeStruct((batch, dil, seq // dil, gw), BF16) for _, dil in DIL_CONFIGS]
    out_specs.append(pl.BlockSpec((tm, D_MODEL), lambda i, j: (i, 0)))
    out_shape.append(jax.ShapeDtypeStruct((t, D_MODEL), BF16))
    return pl.pallas_call(
        functools.partial(_proj_a_kernel, tm=tm),
        grid=(t // tm, N_DIL),
        in_specs=[pl.BlockSpec((tm, D_MODEL), lambda i, j: (i, 0)),
                  pl.BlockSpec((1, D_MODEL), lambda i, j: (0, 0)),
                  pl.BlockSpec((D_MODEL, gw), lambda i, j: (0, j))],
        out_specs=out_specs,
        out_shape=out_shape,
        scratch_shapes=[pltpu.VMEM((D_MODEL // LANES, tm, LANES), F32), pltpu.VMEM((D_MODEL // LANES, tm, LANES), F32),
                        pltpu.VMEM((tm, D_MODEL), BF16)],
        compiler_params=_cparams("parallel", "arbitrary"),
        name="proj_a",
    )(x, g, w)


def _band_kernel(*refs, n_heads, n_kv, radius, sb, tq, has_sink, emit_lse):
    if has_sink:
        sink_ref, refs = refs[0], refs[1:]
    bias_ref, q_ref, kp_ref, kc_ref, kn_ref, vp_ref, vc_ref, vn_ref = refs[:8]
    o_ref = refs[8]
    kx_ref, vx_ref, vt_ref, qt_ref = refs[-4:]
    for x_ref, (p_ref, c_ref, n_ref) in ((kx_ref, (kp_ref, kc_ref, kn_ref)), (vx_ref, (vp_ref, vc_ref, vn_ref))):
        x_ref[0:radius, :] = p_ref[0, 0]
        x_ref[radius:radius + tq, :] = c_ref[0, 0]
        x_ref[radius + tq:, :] = n_ref[0, 0]
    vt_ref[...] = vx_ref[...].astype(F32).T.astype(BF16)
    qt_ref[...] = (q_ref[0, 0].astype(F32) * (HEAD_DIM ** -0.5)).T.astype(BF16)
    w = sb + 2 * radius
    rep = n_heads // n_kv
    heads = range(n_heads)
    zeros = jnp.zeros((HEAD_DIM, sb), BF16)
    nsub = tq // sb
    tile, last_tile = pl.program_id(2), pl.num_programs(2) - 1
    for u in range(nsub):
        rows = slice(u * sb, (u + 1) * sb)
        win = slice(u * sb, u * sb + w)
        kind = 0
        if u == 0:
            kind = kind + (tile == 0).astype(jnp.int32)
        if u == nsub - 1:
            kind = kind + 2 * (tile == last_tile).astype(jnp.int32)
        scores = []
        for h in heads:
            g = h // rep
            qh = qt_ref[h * HEAD_DIM:(h + 1) * HEAD_DIM, rows]
            rhs = jnp.concatenate([qh, zeros] if g % 2 == 0 else [zeros, qh], axis=0)
            pair = slice((g // 2) * 2 * HEAD_DIM, (g // 2 + 1) * 2 * HEAD_DIM)
            scores.append(jnp.dot(kx_ref[win, pair], rhs, preferred_element_type=F32) + bias_ref[kind, h])
        probs, stats = [], []
        for h in heads:
            m = jnp.max(scores[h], axis=0, keepdims=True)
            if has_sink:
                m = jnp.maximum(m, sink_ref[h])
            p = jnp.exp(scores[h] - m)
            l = jnp.sum(p, axis=0, keepdims=True)
            if has_sink:
                l = l + jnp.exp(sink_ref[h] - m)
            probs.append(p.astype(BF16))
            stats.append((m, l))
        outs, lses = [], []
        for h in heads:
            g = h // rep
            m, l = stats[h]
            outs.append(jnp.dot(vt_ref[g * HEAD_DIM:(g + 1) * HEAD_DIM, win], probs[h],
                                preferred_element_type=F32) / l)
            lses.append(jnp.broadcast_to(m + jnp.log(l), (HEAD_DIM, sb)))
        o_ref[0, 0, rows, :] = jnp.concatenate(outs, axis=0).T.astype(o_ref.dtype)
        if emit_lse:
            refs[9][0, 0, rows, :] = jnp.concatenate(lses, axis=0).T


def _alibi(n):
    return np.asarray(2.0 ** (-8.0 * np.arange(1, n + 1) / n), dtype=np.float32)


def _band_bias(n_heads, radius, step, sb):
    w = sb + 2 * radius
    shape = (4, n_heads, w, sb)
    kind, h, kj, qi = (lax.broadcasted_iota(jnp.int32, shape, d) for d in range(4))
    rel = jnp.abs(kj - radius - qi)
    kpos = kj - radius
    ok = (rel <= radius) & ((kpos >= 0) | (kind % 2 == 0)) & ((kpos < sb) | (kind < 2))
    slopes = jnp.asarray(_alibi(n_heads))[h]
    return jnp.where(ok, -slopes * (step * rel).astype(F32), NEG)


def _band_attention(arr, *, q_col, k_col, v_col, kv_width, n_heads, n_kv, radius, step, sink, out_dtype,
                    emit_lse):
    batch, n_classes, length, _ = arr.shape
    qw = n_heads * HEAD_DIM
    tq = _pick(length, ATTN_TILE_Q)
    sb = min(tq, BAND_BLOCK_Q)
    nblk = length // tq

    per = tq // radius
    last_halo = length // radius - 1

    def kv_spec(col, shift):
        if shift == 0:
            return pl.BlockSpec((1, 1, tq, kv_width), lambda b, r, i: (b, r, i, col))
        if shift < 0:
            return pl.BlockSpec((1, 1, radius, kv_width), lambda b, r, i: (b, r, jnp.maximum(i * per - 1, 0), col))
        return pl.BlockSpec((1, 1, radius, kv_width),
                            lambda b, r, i: (b, r, jnp.minimum((i + 1) * per, last_halo), col))

    bias = _band_bias(n_heads, radius, step, sb)
    in_specs = [pl.BlockSpec(bias.shape, lambda b, r, i: (0, 0, 0, 0)),
                pl.BlockSpec((1, 1, tq, qw), lambda b, r, i: (b, r, i, q_col)),
                kv_spec(k_col, -1), kv_spec(k_col, 0), kv_spec(k_col, 1),
                kv_spec(v_col, -1), kv_spec(v_col, 0), kv_spec(v_col, 1)]
    args = [bias] + [arr] * 7
    if sink is not None:
        in_specs = [pl.BlockSpec(memory_space=pltpu.SMEM)] + in_specs
        args = [sink] + args
    o_spec = pl.BlockSpec((1, 1, tq, qw), lambda b, r, i: (b, r, i, 0))
    o_shape = jax.ShapeDtypeStruct((batch, n_classes, length, qw), out_dtype)
    out_specs, out_shape = [o_spec], [o_shape]
    if emit_lse:
        out_specs, out_shape = [o_spec, o_spec], [o_shape, jax.ShapeDtypeStruct(o_shape.shape, F32)]
    return pl.pallas_call(
        functools.partial(_band_kernel, n_heads=n_heads, n_kv=n_kv, radius=radius, sb=sb, tq=tq,
                          has_sink=sink is not None, emit_lse=emit_lse),
        grid=(batch, n_classes, nblk),
        in_specs=in_specs,
        out_specs=out_specs,
        out_shape=out_shape,
        scratch_shapes=[pltpu.VMEM((tq + 2 * radius, kv_width), BF16), pltpu.VMEM((tq + 2 * radius, kv_width), BF16),
                        pltpu.VMEM((kv_width, tq + 2 * radius), BF16), pltpu.VMEM((qw, tq), BF16)],
        compiler_params=_cparams("parallel", "parallel", "arbitrary"),
        name="band_attn",
    )(*args)


def _mla_attn_kernel(qt_ref, k_ref, vt_ref, o_ref, st_ref, *, seq, kc):
    chunks = [slice(c * kc, (c + 1) * kc) for c in range(seq // kc)]

    def qk(h, rows):
        sl = slice(h * SLOT, (h + 1) * SLOT)
        st = jnp.dot(k_ref[0, rows, sl], qt_ref[0, sl, :], preferred_element_type=F32)
        st_ref[h % 2, rows, :] = st
        return jnp.max(st, axis=0, keepdims=True)

    def pv(h, rows, m):
        p = jnp.exp2(st_ref[h % 2, rows, :] - m)
        return (jnp.sum(p, axis=0, keepdims=True),
                jnp.dot(vt_ref[0, h * V_DIM:(h + 1) * V_DIM, rows], p.astype(BF16), preferred_element_type=F32))

    def add(acc, new):
        return new if acc is None else (acc[0] + new[0], acc[1] + new[1])

    n_heads = qt_ref.shape[1] // SLOT
    m = functools.reduce(jnp.maximum, [qk(0, rows) for rows in chunks])
    outs = []
    for h in range(n_heads):
        acc = m_next = None
        for rows in chunks:
            acc = add(acc, pv(h, rows, m))
            if h + 1 < n_heads:
                mc = qk(h + 1, rows)
                m_next = mc if m_next is None else jnp.maximum(m_next, mc)
        outs.append(acc[1] / acc[0])
        m = m_next
    o_ref[0] = jnp.concatenate(outs, axis=0).T.astype(o_ref.dtype)


def _mla_attn(qt, k, vt, batch, seq):
    tq = _pick(seq, ATTN_TILE_Q)
    kc = _pick(seq, MLA_KEY_CHUNK)
    hg = B_HEADS // 2
    return pl.pallas_call(
        functools.partial(_mla_attn_kernel, seq=seq, kc=kc),
        scratch_shapes=[pltpu.VMEM((2, seq, tq), F32)],
        grid=(batch, B_HEADS // hg, seq // tq),
        in_specs=[pl.BlockSpec((1, hg * SLOT, tq), lambda b, g, i: (b, g, i)),
                  pl.BlockSpec((1, seq, hg * SLOT), lambda b, g, i: (b, 0, g)),
                  pl.BlockSpec((1, hg * V_DIM, seq), lambda b, g, i: (b, g, 0))],
        out_specs=pl.BlockSpec((1, tq, hg * V_DIM), lambda b, g, i: (b, i, g)),
        out_shape=jax.ShapeDtypeStruct((batch, seq, B_HEADS * V_DIM), BF16),
        compiler_params=_cparams("parallel", "parallel", "arbitrary"),
        name="mla_attn",
    )(qt, k, vt)


def _sigmoid_tanh(x):
    return 0.5 * (jnp.tanh(0.5 * x) + 1.0)


def _softplus(x):
    return jnp.maximum(x, 0.0) + jnp.log1p(jnp.exp(-jnp.abs(x)))


def _lru_kernel(xf_ref, xfp_ref, xfn_ref, xr_ref, xrp_ref, xrn_ref, cw_ref, cb_ref, wg_ref, bg_ref, lam_ref,
                hf_ref, hr_ref, ext_ref, af_ref, uf_ref, ar_ref, ur_ref, hfs_ref, hrs_ref, cf_ref, cr_ref,
                *, ts, pitch):
    i = pl.program_id(0)
    nt = pl.num_programs(0)
    halo = SUBLANES
    nb = xf_ref.shape[0]
    slabs = LRU_WIDTH // LANES

    @pl.when(i == 0)
    def _():
        cf_ref[...] = jnp.zeros_like(cf_ref)
        cr_ref[...] = jnp.zeros_like(cr_ref)

    def gates(d, tile, x_ref, xp_ref, xn_ref, a_ref, u_ref):
        def one_batch(b, carry):
            ext_ref[0:halo, :] = jnp.where(tile > 0, xp_ref[b], 0.0)
            ext_ref[halo:halo + ts, :] = x_ref[b]
            ext_ref[halo + ts:, :] = jnp.where(tile < nt - 1, xn_ref[b], 0.0)
            xc = cb_ref[...]
            for tap in range(CONV_W):
                off = halo - CONV_PAD_L + tap
                xc = xc + ext_ref[off:off + ts, :] * cw_ref[tap:tap + 1, :]
            g = jnp.dot(xc.astype(BF16), wg_ref[d], preferred_element_type=F32) + bg_ref[d:d + 1, :]
            r = _sigmoid_tanh(g[:, :LRU_WIDTH])
            ig = _sigmoid_tanh(g[:, LRU_WIDTH:])
            log_a = -LRU_C * r * _softplus(-lam_ref[d:d + 1, :])
            a = jnp.exp(log_a)
            u = jnp.sqrt(-jnp.tanh(log_a) * (1.0 + a * a)) * (ig * xc)
            rows = pl.ds(pl.multiple_of(b * pitch, SUBLANES), ts)
            for c in range(slabs):
                a_ref[c, rows, :] = a[:, c * LANES:(c + 1) * LANES]
                u_ref[c, rows, :] = u[:, c * LANES:(c + 1) * LANES]
            return carry
        lax.fori_loop(0, nb, one_batch, 0)

    gates(0, i, xf_ref, xfp_ref, xfn_ref, af_ref, uf_ref)
    gates(1, nt - 1 - i, xr_ref, xrp_ref, xrn_ref, ar_ref, ur_ref)

    def time_step(t, carry):
        hf, hr = carry
        fwd = pl.ds(t, nb, stride=pitch)
        rev = pl.ds(ts - 1 - t, nb, stride=pitch)
        new_f, new_r = [], []
        for c in range(slabs):
            h = af_ref[c, fwd, :] * hf[c] + uf_ref[c, fwd, :]
            hfs_ref[c, fwd, :] = h
            new_f.append(h)
            h = ar_ref[c, rev, :] * hr[c] + ur_ref[c, rev, :]
            hrs_ref[c, rev, :] = h
            new_r.append(h)
        return tuple(new_f), tuple(new_r)

    init = (tuple(cf_ref[c] for c in range(slabs)), tuple(cr_ref[c] for c in range(slabs)))
    hf, hr = lax.fori_loop(0, ts, time_step, init, unroll=8)
    for c in range(slabs):
        cf_ref[c] = hf[c]
        cr_ref[c] = hr[c]
        for b in range(nb):
            hf_ref[b, :, c * LANES:(c + 1) * LANES] = hfs_ref[c, b * pitch:b * pitch + ts, :]
            hr_ref[b, :, c * LANES:(c + 1) * LANES] = hrs_ref[c, b * pitch:b * pitch + ts, :]


def _lru(pd, cw, cb, wg, bg, lam):
    batch, seq, _ = pd.shape
    ts = _pick(seq, LRU_TILE_STEPS)
    nt = seq // ts
    per = ts // SUBLANES
    last8 = seq // SUBLANES - 1
    pitch = ts + SUBLANES

    def tile_specs(tile):
        return [pl.BlockSpec((batch, ts, LRU_WIDTH), lambda i: (0, tile(i), 0)),
                pl.BlockSpec((batch, SUBLANES, LRU_WIDTH), lambda i: (0, jnp.maximum(tile(i) * per - 1, 0), 0)),
                pl.BlockSpec((batch, SUBLANES, LRU_WIDTH), lambda i: (0, jnp.minimum((tile(i) + 1) * per, last8), 0))]

    fwd_tile = lambda i: i
    rev_tile = lambda i: nt - 1 - i
    fixed2 = lambda i: (0, 0)
    shp = jax.ShapeDtypeStruct((batch, seq, LRU_WIDTH), F32)
    slab = pltpu.VMEM((LRU_WIDTH // LANES, batch * pitch, LANES), F32)
    carry = pltpu.VMEM((LRU_WIDTH // LANES, batch, LANES), F32)
    return pl.pallas_call(
        functools.partial(_lru_kernel, ts=ts, pitch=pitch),
        grid=(nt,),
        in_specs=tile_specs(fwd_tile) + tile_specs(rev_tile)
                 + [pl.BlockSpec((CONV_W, LRU_WIDTH), fixed2), pl.BlockSpec((1, LRU_WIDTH), fixed2),
                    pl.BlockSpec((2, LRU_WIDTH, 2 * LRU_WIDTH), lambda i: (0, 0, 0)),
                    pl.BlockSpec((2, 2 * LRU_WIDTH), fixed2), pl.BlockSpec((2, LRU_WIDTH), fixed2)],
        out_specs=[pl.BlockSpec((batch, ts, LRU_WIDTH), lambda i: (0, i, 0)),
                   pl.BlockSpec((batch, ts, LRU_WIDTH), lambda i: (0, nt - 1 - i, 0))],
        out_shape=[shp, shp],
        scratch_shapes=[pltpu.VMEM((ts + 2 * SUBLANES, LRU_WIDTH), F32), slab, slab, slab, slab, slab, slab,
                        carry, carry],
        compiler_params=_cparams("arbitrary"),
        name="lru",
    )(pd, pd, pd, pd, pd, pd, cw, cb, wg, bg, lam)


def _gelu_tanh(x):
    return 0.5 * x * (1.0 + jnp.tanh(np.sqrt(2.0 / np.pi).astype(np.float32) * (x + 0.044715 * (x * x * x))))


def _branch_out_kernel(o0, o1, o2, l0, l1, l2, hf_ref, hr_ref, gd_ref, ya_ref, yd_ref, *nat_refs, tm):
    def natural(ref, g, buf):
        dil = DIL_CONFIGS[g][1]
        if dil == 1:
            return ref[0, 0]
        chunks = BRANCH_W // LANES
        for r in range(dil):
            for c in range(chunks):
                buf[c, pl.ds(r, tm // dil, stride=dil), :] = ref[0, r, :, c * LANES:(c + 1) * LANES]
        return jnp.concatenate([buf[c] for c in range(chunks)], axis=1)

    outs = (o0[0, 0], natural(o1, 1, nat_refs[0]), natural(o2, 2, nat_refs[1]))
    lse = (l0[0, 0], natural(l1, 1, nat_refs[2]), natural(l2, 2, nat_refs[3]))
    m = jnp.maximum(jnp.maximum(lse[0], lse[1]), lse[2])
    e = [jnp.exp(v - m) for v in lse]
    tot = e[0] + e[1] + e[2]
    ya = (e[0] / tot) * outs[0] + (e[1] / tot) * outs[1] + (e[2] / tot) * outs[2]
    ya_ref[...] = ya.astype(BF16)
    yd_ref[...] = ((hf_ref[...] + hr_ref[...]) * _gelu_tanh(gd_ref[...])).astype(BF16)


def _branch_out(oa, la, hf, hr, pd, seq):
    t = hf.shape[0]
    tm = _pick(seq, TILE_ROWS)
    nseq = seq // tm
    blk = pl.BlockSpec((tm, BRANCH_W), lambda i: (i, 0))
    cls = [pl.BlockSpec((1, dil, tm // dil, BRANCH_W), lambda i: (i // nseq, 0, i % nseq, 0))
           for _, dil in DIL_CONFIGS]
    shp = jax.ShapeDtypeStruct((t, BRANCH_W), BF16)
    return pl.pallas_call(
        functools.partial(_branch_out_kernel, tm=tm),
        grid=(t // tm,),
        in_specs=cls + cls + [blk, blk, pl.BlockSpec((tm, BRANCH_W), lambda i: (i, 1))],
        out_specs=[blk, blk],
        out_shape=[shp, shp],
        scratch_shapes=[pltpu.VMEM((BRANCH_W // LANES, tm, LANES), F32)] * 4,
        compiler_params=_cparams("parallel"),
        name="branch_out",
    )(*oa, *la, hf, hr, pd)


def _gate_merge_out_kernel(x_ref, h_ref, ya, yb, yc, yd, g0, g1, g2, g3, wb_ref, wo_ref, o_ref):
    j = pl.program_id(1)
    last = pl.num_programs(1) - 1

    def partial_out():
        h = h_ref[...]
        merged = None
        for nbr, (y_ref, wg_ref) in enumerate(zip((ya, yb, yc, yd), (g0, g1, g2, g3))):
            gate = jax.nn.sigmoid(jnp.dot(h, wg_ref[...], preferred_element_type=F32))
            term = gate * jnp.dot(y_ref[...], wb_ref[nbr], preferred_element_type=F32)
            merged = term if merged is None else merged + term
        return jnp.dot(merged.astype(BF16), wo_ref[...], preferred_element_type=F32)

    @pl.when(j == 0)
    def _():
        o_ref[...] = partial_out()

    @pl.when((j > 0) & (j < last))
    def _():
        o_ref[...] += partial_out()

    @pl.when(j == last)
    def _():
        o_ref[...] = x_ref[...] + (o_ref[...] + partial_out())


def _gate_merge_out(x, h, ys, wg, wb, wo):
    t = x.shape[0]
    tm, tn = _pick(t, TILE_ROWS), MERGE_TILE_COLS
    nj = D_MODEL // tn
    row = lambda i, j: (i, 0)
    gate_specs = [pl.BlockSpec((D_MODEL, tn), functools.partial(lambda i, j, nbr: (0, nbr * nj + j), nbr=nbr))
                  for nbr in range(N_BRANCH)]
    return pl.pallas_call(
        _gate_merge_out_kernel,
        grid=(t // tm, nj),
        in_specs=[pl.BlockSpec((tm, D_MODEL), row), pl.BlockSpec((tm, D_MODEL), row)]
                 + [pl.BlockSpec((tm, BRANCH_W), row)] * N_BRANCH + gate_specs
                 + [pl.BlockSpec((N_BRANCH, BRANCH_W, tn), lambda i, j: (0, 0, j)),
                    pl.BlockSpec((tn, D_MODEL), lambda i, j: (j, 0))],
        out_specs=pl.BlockSpec((tm, D_MODEL), row),
        out_shape=jax.ShapeDtypeStruct((t, D_MODEL), F32),
        compiler_params=_cparams("parallel", "arbitrary"),
        name="gate_merge_out",
    )(x, h, *ys, wg, wg, wg, wg, wb, wo)


def _block_diag(w):
    eye = jnp.eye(LRU_BLOCKS, dtype=w.dtype)
    return jnp.einsum('nef,nm->nemf', w, eye).reshape(LRU_WIDTH, LRU_WIDTH)


def _prep_layer(l, ffn1_norm, ffn1_w1, ffn1_w3, ffn1_w2, mix_norm, w_in, mla_q_norm, mla_w_uq, mla_kv_norm,
                mla_w_ukv, lru_conv_w, lru_conv_b, lru_w_a, lru_b_a, lru_w_x, lru_b_x, lru_lambda, sink_logits,
                w_branch, w_out, ffn2_norm, ffn2_w1, ffn2_w3, ffn2_w2):
    row = lambda v: v.reshape(1, -1)
    wi = w_in[l]
    c0, c1, c2, c3 = A_COLS, A_COLS + B_COLS, A_COLS + B_COLS + C_COLS, A_COLS + B_COLS + C_COLS + D_COLS
    w_a = wi[:, :c0].reshape(D_MODEL, 3, N_DIL, A_HEADS * HEAD_DIM).transpose(0, 2, 1, 3).reshape(D_MODEL, A_COLS)
    wb = wi[:, c0:c1]
    zeros = lambda n: jnp.zeros((D_MODEL, n), wi.dtype)
    w_b = jnp.concatenate([wb[:, :Q_LORA + KV_LORA], zeros(QK_NOPE), wb[:, Q_LORA + KV_LORA:],
                           zeros(SLOT - QK_NOPE - QK_ROPE)], axis=1)
    w_cbd = jnp.concatenate([wi[:, c1:c2], w_b, wi[:, c2:c3]], axis=1)
    wq = mla_w_uq[l].reshape(Q_LORA, B_HEADS, QK_NOPE + QK_ROPE)
    wq = jnp.pad(wq, ((0, 0), (0, 0), (0, SLOT - QK_NOPE - QK_ROPE))).reshape(Q_LORA, B_HEADS * SLOT)
    wkv = mla_w_ukv[l].reshape(KV_LORA, B_HEADS, QK_NOPE + V_DIM)
    wk = jnp.pad(wkv[:, :, :QK_NOPE], ((0, 0), (0, 0), (0, SLOT - QK_NOPE))).reshape(KV_LORA, B_HEADS * SLOT)
    wv = wkv[:, :, QK_NOPE:].reshape(KV_LORA, B_HEADS * V_DIM)
    wg = jnp.stack([jnp.concatenate([_block_diag(lru_w_a[l, d]), _block_diag(lru_w_x[l, d])], axis=1)
                    for d in range(2)])
    bg = jnp.concatenate([lru_b_a[l], lru_b_x[l]], axis=1)
    return dict(
        ffn1=(row(ffn1_norm[l]), ffn1_w1[l].astype(BF16), ffn1_w3[l].astype(BF16), ffn1_w2[l].astype(BF16)),
        ffn2=(row(ffn2_norm[l]), ffn2_w1[l].astype(BF16), ffn2_w3[l].astype(BF16), ffn2_w2[l].astype(BF16)),
        mix_norm=row(mix_norm[l]),
        w_a=w_a.astype(BF16), w_cbd=w_cbd.astype(BF16), w_g=wi[:, c3:].astype(BF16),
        qn=row(mla_q_norm[l]), kvn=row(mla_kv_norm[l]),
        wq=wq.astype(BF16), wk=wk.astype(BF16), wv=wv.astype(BF16),
        conv_w=lru_conv_w[l], conv_b=row(lru_conv_b[l]), wg=wg.astype(BF16), bg=bg, lam=lru_lambda[l],
        sink=sink_logits[l], w_branch=w_branch[l].astype(BF16), w_out=w_out[l].astype(BF16),
    )


def _rope_slot_tables(seq):
    inv = ROPE_THETA ** (-jnp.arange(0, QK_ROPE, 2, dtype=F32) / QK_ROPE)
    ang = jnp.arange(seq, dtype=F32)[:, None] * inv[None, :]
    cos, sin = jnp.cos(ang), jnp.sin(ang)
    scale = (QK_NOPE + QK_ROPE) ** -0.5 * np.log2(np.e)
    z = lambda n: jnp.zeros((seq, n), F32)
    tail = SLOT - QK_NOPE - QK_ROPE
    cos_q = jnp.concatenate([jnp.full((seq, QK_NOPE), scale, F32), cos * scale, cos * scale, z(tail)], axis=1)
    sin_q = jnp.concatenate([z(QK_NOPE), -sin * scale, sin * scale, z(tail)], axis=1)
    cos_k = jnp.concatenate([z(QK_NOPE), cos, cos, z(tail)], axis=1)
    sin_k = jnp.concatenate([z(QK_NOPE), -sin, sin, z(tail)], axis=1)
    return cos_q, sin_q, cos_k, sin_k


def _layer(x, w, batch, seq, final_g, last_layer):
    t = batch * seq
    x = _ffn(x, *w['ffn1'], final_g, False)
    *pas, h = _proj_a(x, w['mix_norm'], w['w_a'], batch, seq)
    pc, pd, qt, k, vt = _proj_bcd(h, w['w_cbd'], w['qn'], w['kvn'], w['wq'], w['wk'], w['wv'],
                                 _rope_slot_tables(seq), batch, seq)

    oa, la = [], []
    for pa, (window, dil) in zip(pas, DIL_CONFIGS):
        o, lse = _band_attention(
            pa, q_col=0, k_col=1, v_col=2, kv_width=A_HEADS * HEAD_DIM,
            n_heads=A_HEADS, n_kv=A_HEADS, radius=window // (2 * dil), step=dil, sink=None,
            out_dtype=F32, emit_lse=True)
        oa.append(o)
        la.append(lse)

    yb = _mla_attn(qt, k.reshape(batch, seq, -1), vt, batch, seq)

    kvw = C_KV_HEADS * HEAD_DIM
    yc = _band_attention(
        pc.reshape(batch, 1, seq, C_COLS),
        q_col=0, k_col=C_HEADS * HEAD_DIM // kvw, v_col=C_HEADS * HEAD_DIM // kvw + 1, kv_width=kvw,
        n_heads=C_HEADS, n_kv=C_KV_HEADS, radius=C_RADIUS, step=1, sink=w['sink'],
        out_dtype=BF16, emit_lse=False)[0]

    hf, hr = _lru(pd.reshape(batch, seq, D_COLS), w['conv_w'], w['conv_b'], w['wg'], w['bg'], w['lam'])

    ya, yd = _branch_out(oa, la, hf.reshape(t, LRU_WIDTH), hr.reshape(t, LRU_WIDTH), pd, seq)
    x = _gate_merge_out(x, h, (ya, yb.reshape(t, BRANCH_W), yc.reshape(t, BRANCH_W), yd),
                        w['w_g'], w['w_branch'], w['w_out'])
    return _ffn(x, *w['ffn2'], final_g, last_layer)


def _trunk(x, layers, final_norm):
    batch, seq, _ = x.shape
    h = x.reshape(batch * seq, D_MODEL)
    fg = final_norm.reshape(1, -1)
    for l, w in enumerate(layers):
        h = _layer(h, w, batch, seq, fg, l == len(layers) - 1)
    return h.reshape(batch, seq, D_MODEL)


def kernel(x_prompt, x_sample, ffn1_norm, ffn1_w1, ffn1_w3, ffn1_w2, mix_norm, w_in, mla_q_norm, mla_w_uq, mla_kv_norm, mla_w_ukv, lru_conv_w, lru_conv_b, lru_w_a, lru_b_a, lru_w_x, lru_b_x, lru_lambda, sink_logits, w_branch, w_out, ffn2_norm, ffn2_w1, ffn2_w3, ffn2_w2, final_norm):
    layers = [_prep_layer(l, ffn1_norm, ffn1_w1, ffn1_w3, ffn1_w2, mix_norm, w_in, mla_q_norm, mla_w_uq,
                          mla_kv_norm, mla_w_ukv, lru_conv_w, lru_conv_b, lru_w_a, lru_b_a, lru_w_x, lru_b_x,
                          lru_lambda, sink_logits, w_branch, w_out, ffn2_norm, ffn2_w1, ffn2_w3, ffn2_w2)
              for l in range(DEPTH)]
    return (_trunk(x_prompt, layers, final_norm), _trunk(x_sample, layers, final_norm))
```

```python
import functools

import numpy as np
import jax
import jax.numpy as jnp
from jax import lax
from jax.experimental import pallas as pl
from jax.experimental.pallas import tpu as pltpu

F32 = jnp.float32
BF16 = jnp.bfloat16

D_MODEL = 2048
DEPTH = 2
HEAD_DIM = 64
N_BRANCH = 4
BRANCH_W = 512
DIL_CONFIGS = ((128, 1), (512, 4), (2048, 16))
N_DIL = 3
A_HEADS = 8
B_HEADS = 8
Q_LORA = 384
KV_LORA = 128
QK_NOPE = 64
QK_ROPE = 32
V_DIM = 64
ROPE_THETA = 10000.0
C_HEADS = 8
C_KV_HEADS = 2
C_RADIUS = 128
LRU_WIDTH = 512
LRU_BLOCKS = 8
CONV_W = 4
CONV_PAD_L = 2
LRU_C = 8.0
D_FF = 5632
EPS = 1e-6
NEG = -1e30

A_COLS = 3 * N_DIL * A_HEADS * HEAD_DIM
B_COLS = Q_LORA + KV_LORA + QK_ROPE
C_COLS = (C_HEADS + 2 * C_KV_HEADS) * HEAD_DIM
D_COLS = 2 * LRU_WIDTH

LANES = 128
SUBLANES = 8
SLOT = LANES
B_PAD_COLS = Q_LORA + KV_LORA + SLOT
TILE_ROWS = 512
FFN_TILE_ROWS = 1024
FFN_TILE_HIDDEN = 512
MERGE_TILE_COLS = 512
ATTN_TILE_Q = 512
BAND_TILE_Q = 1024
BAND_BLOCK_Q = 128
MLA_KEY_CHUNK = 256
LRU_TILE_STEPS = 128
VMEM_LIMIT = 52 * 1024 * 1024
VMEM_LIMIT_FFN = 56 * 1024 * 1024


def _cparams(*sem, vmem=VMEM_LIMIT):
    return pltpu.CompilerParams(dimension_semantics=sem, vmem_limit_bytes=vmem)


def _rms(x, g):
    return x * lax.rsqrt(jnp.mean(x * x, axis=-1, keepdims=True) + EPS) * g


def _pick(n, pref):
    t = min(n, pref)
    while n % t:
        t //= 2
    return t


def _ffn_kernel(x_ref, g_ref, w1_ref, w3_ref, w2_ref, fg_ref, o_ref, h_ref, *, final_norm):
    j = pl.program_id(1)
    last = pl.num_programs(1) - 1
    tm = x_ref.shape[0]
    halves = (slice(0, tm // 2), slice(tm // 2, tm))

    def down(h):
        a = jnp.dot(h, w1_ref[...], preferred_element_type=F32)
        b = jnp.dot(h, w3_ref[...], preferred_element_type=F32)
        act = (a * jax.nn.sigmoid(a) * b).astype(BF16)
        return jnp.dot(act, w2_ref[...], preferred_element_type=F32)

    @pl.when(j == 0)
    def _():
        for rows in halves:
            h = _rms(x_ref[rows, :], g_ref[...]).astype(BF16)
            h_ref[rows, :] = h
            o_ref[rows, :] = down(h)

    @pl.when((j > 0) & (j < last))
    def _():
        for rows in halves:
            o_ref[rows, :] += down(h_ref[rows, :])

    @pl.when(j == last)
    def _():
        for rows in halves:
            y = x_ref[rows, :] + 0.5 * (o_ref[rows, :] + down(h_ref[rows, :]))
            if final_norm:
                y = _rms(y, fg_ref[...])
            o_ref[rows, :] = y


def _ffn(x, g, w1, w3, w2, fg, final_norm):
    t = x.shape[0]
    tm, tf = _pick(t, FFN_TILE_ROWS), FFN_TILE_HIDDEN
    return pl.pallas_call(
        functools.partial(_ffn_kernel, final_norm=final_norm),
        grid=(t // tm, D_FF // tf),
        in_specs=[
            pl.BlockSpec((tm, D_MODEL), lambda i, j: (i, 0)),
            pl.BlockSpec((1, D_MODEL), lambda i, j: (0, 0)),
            pl.BlockSpec((D_MODEL, tf), lambda i, j: (0, j)),
            pl.BlockSpec((D_MODEL, tf), lambda i, j: (0, j)),
            pl.BlockSpec((tf, D_MODEL), lambda i, j: (j, 0)),
            pl.BlockSpec((1, D_MODEL), lambda i, j: (0, 0)),
        ],
        out_specs=pl.BlockSpec((tm, D_MODEL), lambda i, j: (i, 0)),
        out_shape=jax.ShapeDtypeStruct((t, D_MODEL), F32),
        scratch_shapes=[pltpu.VMEM((tm, D_MODEL), BF16)],
        compiler_params=_cparams("parallel", "arbitrary", vmem=VMEM_LIMIT_FFN),
        name="ffn",
    )(x, g, w1, w3, w2, fg)


def _rope_slot(x, cos_t, sin_t):
    lane = lax.broadcasted_iota(jnp.int32, x.shape, 1)
    first = (lane >= QK_NOPE) & (lane < QK_NOPE + QK_ROPE // 2)
    partner = jnp.where(first, pltpu.roll(x, SLOT - QK_ROPE // 2, 1), pltpu.roll(x, QK_ROPE // 2, 1))
    return x * cos_t + partner * sin_t


def _proj_bcd_kernel(h_ref, w_ref, qn_ref, kvn_ref, wq_ref, wk_ref, wv_ref, cq_ref, sq_ref, ck_ref, sk_ref,
                     pc_out, pd_out, q_out, k_out, vt_out):
    r = jnp.dot(h_ref[...], w_ref[...], preferred_element_type=F32)
    pc_out[...] = r[:, :C_COLS].astype(BF16)
    pd_out[...] = r[:, C_COLS + B_PAD_COLS:]
    pb = r[:, C_COLS:C_COLS + B_PAD_COLS]
    cq = _rms(pb[:, :Q_LORA], qn_ref[...]).astype(BF16)
    ckv = _rms(pb[:, Q_LORA:Q_LORA + KV_LORA], kvn_ref[...]).astype(BF16)
    kr = pb[:, Q_LORA + KV_LORA:]
    q = jnp.dot(cq, wq_ref[...], preferred_element_type=F32)
    k = jnp.dot(ckv, wk_ref[...], preferred_element_type=F32)
    vt_out[0] = jnp.dot(ckv, wv_ref[...], preferred_element_type=F32).T.astype(BF16)
    kr = _rope_slot(kr, ck_ref[...], sk_ref[...])
    for h in range(B_HEADS):
        sl = slice(h * SLOT, (h + 1) * SLOT)
        q_out[0, sl, :] = _rope_slot(q[:, sl], cq_ref[...], sq_ref[...]).T.astype(BF16)
        k_out[:, sl] = (k[:, sl] + kr).astype(BF16)


def _proj_bcd(h, w, qn, kvn, wq, wk, wv, tabs, batch, seq):
    t, n = h.shape[0], w.shape[1]
    tm = _pick(seq, TILE_ROWS)
    nseq = seq // tm
    row = lambda i: (i, 0)
    fixed = lambda i: (0, 0)
    tab = pl.BlockSpec((tm, SLOT), lambda i: (i % nseq, 0))
    return pl.pallas_call(
        _proj_bcd_kernel,
        grid=(t // tm,),
        in_specs=[pl.BlockSpec((tm, D_MODEL), row), pl.BlockSpec((D_MODEL, n), fixed),
                  pl.BlockSpec((1, Q_LORA), fixed), pl.BlockSpec((1, KV_LORA), fixed),
                  pl.BlockSpec((Q_LORA, B_HEADS * SLOT), fixed),
                  pl.BlockSpec((KV_LORA, B_HEADS * SLOT), fixed),
                  pl.BlockSpec((KV_LORA, B_HEADS * V_DIM), fixed),
                  tab, tab, tab, tab],
        out_specs=[pl.BlockSpec((tm, C_COLS), row), pl.BlockSpec((tm, D_COLS), row),
                   pl.BlockSpec((1, B_HEADS * SLOT, tm), lambda i: (i // nseq, 0, i % nseq)),
                   pl.BlockSpec((tm, B_HEADS * SLOT), row),
                   pl.BlockSpec((1, B_HEADS * V_DIM, tm), lambda i: (i // nseq, 0, i % nseq))],
        out_shape=[jax.ShapeDtypeStruct((t, C_COLS), BF16), jax.ShapeDtypeStruct((t, D_COLS), F32),
                   jax.ShapeDtypeStruct((batch, B_HEADS * SLOT, seq), BF16),
                   jax.ShapeDtypeStruct((t, B_HEADS * SLOT), BF16),
                   jax.ShapeDtypeStruct((batch, B_HEADS * V_DIM, seq), BF16)],
        compiler_params=_cparams("parallel"),
        name="proj_bcd",
    )(h, w, qn, kvn, wq, wk, wv, *tabs)


def _proj_a_kernel(x_ref, g_ref, w_ref, o1_ref, o2_ref, o3_ref, hout_ref, hn_ref, hm_ref, h_ref, *, tm):
    j = pl.program_id(1)
    chunks = D_MODEL // LANES

    @pl.when(j == 0)
    def _():
        hn = _rms(x_ref[...], g_ref[...])
        for c in range(chunks):
            hn_ref[c] = hn[:, c * LANES:(c + 1) * LANES]
        h_ref[...] = hn.astype(BF16)
        hout_ref[...] = hn.astype(BF16)

    for g, o_ref in enumerate((o1_ref, o2_ref, o3_ref)):
        dil = DIL_CONFIGS[g][1]
        n = tm // dil

        @pl.when(j == g)
        def _(g=g, o_ref=o_ref, dil=dil, n=n):
            if g == 1:
                for r in range(dil):
                    for c in range(chunks):
                        rows = hn_ref[c, pl.ds(r, n, stride=dil), :]
                        hm_ref[c, r * n:(r + 1) * n, :] = rows
                        h_ref[r * n:(r + 1) * n, c * LANES:(c + 1) * LANES] = rows.astype(BF16)
            elif g == 2:
                prev = DIL_CONFIGS[g - 1][1]
                ratio, n_prev = dil // prev, tm // prev
                for r in range(dil):
                    for c in range(chunks):
                        rows = hm_ref[c, pl.ds((r % prev) * n_prev + r // prev, n, stride=ratio), :]
                        h_ref[r * n:(r + 1) * n, c * LANES:(c + 1) * LANES] = rows.astype(BF16)
            res = jnp.dot(h_ref[...], w_ref[...], preferred_element_type=F32)
            for r in range(dil):
                o_ref[0, r] = res[r * n:(r + 1) * n].astype(BF16)


def _proj_a(x, g, w, batch, seq):
    t = x.shape[0]
    tm = _pick(seq, TILE_ROWS)
    nseq = seq // tm
    gw = A_COLS // N_DIL
    out_specs = [pl.BlockSpec((1, dil, tm // dil, gw), lambda i, j: (i // nseq, 0, i % nseq, 0))
                 for _, dil in DIL_CONFIGS]
    out_shape = [jax.ShapeDtypeStruct((batch, dil, seq // dil, gw), BF16) for _, dil in DIL_CONFIGS]
    out_specs.append(pl.BlockSpec((tm, D_MODEL), lambda i, j: (i, 0)))
    out_shape.append(jax.ShapeDtypeStruct((t, D_MODEL), BF16))
    return pl.pallas_call(
        functools.partial(_proj_a_kernel, tm=tm),
        grid=(t // tm, N_DIL),
        in_specs=[pl.BlockSpec((tm, D_MODEL), lambda i, j: (i, 0)),
                  pl.BlockSpec((1, D_MODEL), lambda i, j: (0, 0)),
                  pl.BlockSpec((D_MODEL, gw), lambda i, j: (0, j))],
        out_specs=out_specs,
        out_shape=out_shape,
        scratch_shapes=[pltpu.VMEM((D_MODEL // LANES, tm, LANES), F32), pltpu.VMEM((D_MODEL // LANES, tm, LANES), F32),
                        pltpu.VMEM((tm, D_MODEL), BF16)],
        compiler_params=_cparams("parallel", "arbitrary"),
        name="proj_a",
    )(x, g, w)


def _band_kernel(*refs, n_heads, n_kv, radius, sb, tq, has_sink, emit_lse):
    if has_sink:
        sink_ref, refs = refs[0], refs[1:]
    bias_ref, q_ref, kp_ref, kc_ref, kn_ref, vp_ref, vc_ref, vn_ref = refs[:8]
    o_ref = refs[8]
    kx_ref, vx_ref, vt_ref, qt_ref = refs[-4:]
    for x_ref, (p_ref, c_ref, n_ref) in ((kx_ref, (kp_ref, kc_ref, kn_ref)), (vx_ref, (vp_ref, vc_ref, vn_ref))):
        x_ref[0:radius, :] = p_ref[0, 0]
        x_ref[radius:radius + tq, :] = c_ref[0, 0]
        x_ref[radius + tq:, :] = n_ref[0, 0]
    vt_ref[...] = vx_ref[...].astype(F32).T.astype(BF16)
    qt_ref[...] = (q_ref[0, 0].astype(F32) * (HEAD_DIM ** -0.5)).T.astype(BF16)
    w = sb + 2 * radius
    rep = n_heads // n_kv
    heads = range(n_heads)
    zeros = jnp.zeros((HEAD_DIM, sb), BF16)
    nsub = tq // sb
    tile, last_tile = pl.program_id(2), pl.num_programs(2) - 1
    for u in range(nsub):
        rows = slice(u * sb, (u + 1) * sb)
        win = slice(u * sb, u * sb + w)
        kind = 0
        if u == 0:
            kind = kind + (tile == 0).astype(jnp.int32)
        if u == nsub - 1:
            kind = kind + 2 * (tile == last_tile).astype(jnp.int32)
        scores = []
        for h in heads:
            g = h // rep
            qh = qt_ref[h * HEAD_DIM:(h + 1) * HEAD_DIM, rows]
            rhs = jnp.concatenate([qh, zeros] if g % 2 == 0 else [zeros, qh], axis=0)
            pair = slice((g // 2) * 2 * HEAD_DIM, (g // 2 + 1) * 2 * HEAD_DIM)
            scores.append(jnp.dot(kx_ref[win, pair], rhs, preferred_element_type=F32) + bias_ref[kind, h])
        probs, stats = [], []
        for h in heads:
            m = jnp.max(scores[h], axis=0, keepdims=True)
            if has_sink:
                m = jnp.maximum(m, sink_ref[h])
            p = jnp.exp(scores[h] - m)
            l = jnp.sum(p, axis=0, keepdims=True)
            if has_sink:
                l = l + jnp.exp(sink_ref[h] - m)
            probs.append(p.astype(BF16))
            stats.append((m, l))
        outs, lses = [], []
        for h in heads:
            g = h // rep
            m, l = stats[h]
            outs.append(jnp.dot(vt_ref[g * HEAD_DIM:(g + 1) * HEAD_DIM, win], probs[h],
                                preferred_element_type=F32) / l)
            lses.append(jnp.broadcast_to(m + jnp.log(l), (HEAD_DIM, sb)))
        o_ref[0, 0, rows, :] = jnp.concatenate(outs, axis=0).T.astype(o_ref.dtype)
        if emit_lse:
            refs[9][0, 0, rows, :] = jnp.concatenate(lses, axis=0).T


def _alibi(n):
    return np.asarray(2.0 ** (-8.0 * np.arange(1, n + 1) / n), dtype=np.float32)


def _band_bias(n_heads, radius, step, sb):
    w = sb + 2 * radius
    shape = (4, n_heads, w, sb)
    kind, h, kj, qi = (lax.broadcasted_iota(jnp.int32, shape, d) for d in range(4))
    rel = jnp.abs(kj - radius - qi)
    kpos = kj - radius
    ok = (rel <= radius) & ((kpos >= 0) | (kind % 2 == 0)) & ((kpos < sb) | (kind < 2))
    slopes = jnp.asarray(_alibi(n_heads))[h]
    return jnp.where(ok, -slopes * (step * rel).astype(F32), NEG)


def _band_attention(arr, *, q_col, k_col, v_col, kv_width, n_heads, n_kv, radius, step, sink, out_dtype,
                    emit_lse):
    batch, n_classes, length, _ = arr.shape
    qw = n_heads * HEAD_DIM
    tq = _pick(length, BAND_TILE_Q)
    sb = min(tq, BAND_BLOCK_Q)
    nblk = length // tq

    per = tq // radius
    last_halo = length // radius - 1

    def kv_spec(col, shift):
        if shift == 0:
            return pl.BlockSpec((1, 1, tq, kv_width), lambda b, r, i: (b, r, i, col))
        if shift < 0:
            return pl.BlockSpec((1, 1, radius, kv_width), lambda b, r, i: (b, r, jnp.maximum(i * per - 1, 0), col))
        return pl.BlockSpec((1, 1, radius, kv_width),
                            lambda b, r, i: (b, r, jnp.minimum((i + 1) * per, last_halo), col))

    bias = _band_bias(n_heads, radius, step, sb)
    in_specs = [pl.BlockSpec(bias.shape, lambda b, r, i: (0, 0, 0, 0)),
                pl.BlockSpec((1, 1, tq, qw), lambda b, r, i: (b, r, i, q_col)),
                kv_spec(k_col, -1), kv_spec(k_col, 0), kv_spec(k_col, 1),
                kv_spec(v_col, -1), kv_spec(v_col, 0), kv_spec(v_col, 1)]
    args = [bias] + [arr] * 7
    if sink is not None:
        in_specs = [pl.BlockSpec(memory_space=pltpu.SMEM)] + in_specs
        args = [sink] + args
    o_spec = pl.BlockSpec((1, 1, tq, qw), lambda b, r, i: (b, r, i, 0))
    o_shape = jax.ShapeDtypeStruct((batch, n_classes, length, qw), out_dtype)
    out_specs, out_shape = [o_spec], [o_shape]
    if emit_lse:
        out_specs, out_shape = [o_spec, o_spec], [o_shape, jax.ShapeDtypeStruct(o_shape.shape, F32)]
    return pl.pallas_call(
        functools.partial(_band_kernel, n_heads=n_heads, n_kv=n_kv, radius=radius, sb=sb, tq=tq,
                          has_sink=sink is not None, emit_lse=emit_lse),
        grid=(batch, n_classes, nblk),
        in_specs=in_specs,
        out_specs=out_specs,
        out_shape=out_shape,
        scratch_shapes=[pltpu.VMEM((tq + 2 * radius, kv_width), BF16), pltpu.VMEM((tq + 2 * radius, kv_width), BF16),
                        pltpu.VMEM((kv_width, tq + 2 * radius), BF16), pltpu.VMEM((qw, tq), BF16)],
        compiler_params=_cparams("parallel", "parallel", "arbitrary"),
        name="band_attn",
    )(*args)


def _mla_attn_kernel(qt_ref, k_ref, vt_ref, o_ref, st_ref, *, seq, kc):
    chunks = [slice(c * kc, (c + 1) * kc) for c in range(seq // kc)]

    def qk(h, rows):
        sl = slice(h * SLOT, (h + 1) * SLOT)
        st = jnp.dot(k_ref[0, rows, sl], qt_ref[0, sl, :], preferred_element_type=F32)
        st_ref[h % 2, rows, :] = st
        return jnp.max(st, axis=0, keepdims=True)

    def pv(h, rows, m):
        p = jnp.exp2(st_ref[h % 2, rows, :] - m)
        return (jnp.sum(p, axis=0, keepdims=True),
                jnp.dot(vt_ref[0, h * V_DIM:(h + 1) * V_DIM, rows], p.astype(BF16), preferred_element_type=F32))

    def add(acc, new):
        return new if acc is None else (acc[0] + new[0], acc[1] + new[1])

    n_heads = qt_ref.shape[1] // SLOT
    m = functools.reduce(jnp.maximum, [qk(0, rows) for rows in chunks])
    outs = []
    for h in range(n_heads):
        acc = m_next = None
        for rows in chunks:
            acc = add(acc, pv(h, rows, m))
            if h + 1 < n_heads:
                mc = qk(h + 1, rows)
                m_next = mc if m_next is None else jnp.maximum(m_next, mc)
        outs.append(acc[1] / acc[0])
        m = m_next
    o_ref[0] = jnp.concatenate(outs, axis=0).T.astype(o_ref.dtype)


def _mla_attn(qt, k, vt, batch, seq):
    tq = _pick(seq, ATTN_TILE_Q)
    kc = _pick(seq, MLA_KEY_CHUNK)
    hg = B_HEADS // 2
    return pl.pallas_call(
        functools.partial(_mla_attn_kernel, seq=seq, kc=kc),
        scratch_shapes=[pltpu.VMEM((2, seq, tq), F32)],
        grid=(batch, B_HEADS // hg, seq // tq),
        in_specs=[pl.BlockSpec((1, hg * SLOT, tq), lambda b, g, i: (b, g, i)),
                  pl.BlockSpec((1, seq, hg * SLOT), lambda b, g, i: (b, 0, g)),
                  pl.BlockSpec((1, hg * V_DIM, seq), lambda b, g, i: (b, g, 0))],
        out_specs=pl.BlockSpec((1, tq, hg * V_DIM), lambda b, g, i: (b, i, g)),
        out_shape=jax.ShapeDtypeStruct((batch, seq, B_HEADS * V_DIM), BF16),
        compiler_params=_cparams("parallel", "parallel", "arbitrary"),
        name="mla_attn",
    )(qt, k, vt)


def _sigmoid_tanh(x):
    return 0.5 * (jnp.tanh(0.5 * x) + 1.0)


def _softplus(x):
    return jnp.maximum(x, 0.0) + jnp.log1p(jnp.exp(-jnp.abs(x)))


def _lru_kernel(xf_ref, xfp_ref, xfn_ref, xr_ref, xrp_ref, xrn_ref, cw_ref, cb_ref, wg_ref, bg_ref, lam_ref,
                hf_ref, hr_ref, ext_ref, af_ref, uf_ref, ar_ref, ur_ref, hfs_ref, hrs_ref, cf_ref, cr_ref,
                *, ts, pitch):
    i = pl.program_id(0)
    nt = pl.num_programs(0)
    halo = SUBLANES
    nb = xf_ref.shape[0]
    slabs = LRU_WIDTH // LANES

    @pl.when(i == 0)
    def _():
        cf_ref[...] = jnp.zeros_like(cf_ref)
        cr_ref[...] = jnp.zeros_like(cr_ref)

    def gates(d, tile, x_ref, xp_ref, xn_ref, a_ref, u_ref):
        def one_batch(b, carry):
            ext_ref[0:halo, :] = jnp.where(tile > 0, xp_ref[b], 0.0)
            ext_ref[halo:halo + ts, :] = x_ref[b]
            ext_ref[halo + ts:, :] = jnp.where(tile < nt - 1, xn_ref[b], 0.0)
            xc = cb_ref[...]
            for tap in range(CONV_W):
                off = halo - CONV_PAD_L + tap
                xc = xc + ext_ref[off:off + ts, :] * cw_ref[tap:tap + 1, :]
            g = jnp.dot(xc.astype(BF16), wg_ref[d], preferred_element_type=F32) + bg_ref[d:d + 1, :]
            r = _sigmoid_tanh(g[:, :LRU_WIDTH])
            ig = _sigmoid_tanh(g[:, LRU_WIDTH:])
            log_a = -LRU_C * r * _softplus(-lam_ref[d:d + 1, :])
            a = jnp.exp(log_a)
            u = jnp.sqrt(-jnp.tanh(log_a) * (1.0 + a * a)) * (ig * xc)
            rows = pl.ds(pl.multiple_of(b * pitch, SUBLANES), ts)
            for c in range(slabs):
                a_ref[c, rows, :] = a[:, c * LANES:(c + 1) * LANES]
                u_ref[c, rows, :] = u[:, c * LANES:(c + 1) * LANES]
            return carry
        lax.fori_loop(0, nb, one_batch, 0)

    gates(0, i, xf_ref, xfp_ref, xfn_ref, af_ref, uf_ref)
    gates(1, nt - 1 - i, xr_ref, xrp_ref, xrn_ref, ar_ref, ur_ref)

    def time_step(t, carry):
        hf, hr = carry
        fwd = pl.ds(t, nb, stride=pitch)
        rev = pl.ds(ts - 1 - t, nb, stride=pitch)
        new_f, new_r = [], []
        for c in range(slabs):
            h = af_ref[c, fwd, :] * hf[c] + uf_ref[c, fwd, :]
            hfs_ref[c, fwd, :] = h
            new_f.append(h)
            h = ar_ref[c, rev, :] * hr[c] + ur_ref[c, rev, :]
            hrs_ref[c, rev, :] = h
            new_r.append(h)
        return tuple(new_f), tuple(new_r)

    init = (tuple(cf_ref[c] for c in range(slabs)), tuple(cr_ref[c] for c in range(slabs)))
    hf, hr = lax.fori_loop(0, ts, time_step, init, unroll=8)
    for c in range(slabs):
        cf_ref[c] = hf[c]
        cr_ref[c] = hr[c]
        for b in range(nb):
            hf_ref[b, :, c * LANES:(c + 1) * LANES] = hfs_ref[c, b * pitch:b * pitch + ts, :]
            hr_ref[b, :, c * LANES:(c + 1) * LANES] = hrs_ref[c, b * pitch:b * pitch + ts, :]


def _lru(pd, cw, cb, wg, bg, lam):
    batch, seq, _ = pd.shape
    ts = _pick(seq, LRU_TILE_STEPS)
    nt = seq // ts
    per = ts // SUBLANES
    last8 = seq // SUBLANES - 1
    pitch = ts + SUBLANES

    def tile_specs(tile):
        return [pl.BlockSpec((batch, ts, LRU_WIDTH), lambda i: (0, tile(i), 0)),
                pl.BlockSpec((batch, SUBLANES, LRU_WIDTH), lambda i: (0, jnp.maximum(tile(i) * per - 1, 0), 0)),
                pl.BlockSpec((batch, SUBLANES, LRU_WIDTH), lambda i: (0, jnp.minimum((tile(i) + 1) * per, last8), 0))]

    fwd_tile = lambda i: i
    rev_tile = lambda i: nt - 1 - i
    fixed2 = lambda i: (0, 0)
    shp = jax.ShapeDtypeStruct((batch, seq, LRU_WIDTH), F32)
    slab = pltpu.VMEM((LRU_WIDTH // LANES, batch * pitch, LANES), F32)
    carry = pltpu.VMEM((LRU_WIDTH // LANES, batch, LANES), F32)
    return pl.pallas_call(
        functools.partial(_lru_kernel, ts=ts, pitch=pitch),
        grid=(nt,),
        in_specs=tile_specs(fwd_tile) + tile_specs(rev_tile)
                 + [pl.BlockSpec((CONV_W, LRU_WIDTH), fixed2), pl.BlockSpec((1, LRU_WIDTH), fixed2),
                    pl.BlockSpec((2, LRU_WIDTH, 2 * LRU_WIDTH), lambda i: (0, 0, 0)),
                    pl.BlockSpec((2, 2 * LRU_WIDTH), fixed2), pl.BlockSpec((2, LRU_WIDTH), fixed2)],
        out_specs=[pl.BlockSpec((batch, ts, LRU_WIDTH), lambda i: (0, i, 0)),
                   pl.BlockSpec((batch, ts, LRU_WIDTH), lambda i: (0, nt - 1 - i, 0))],
        out_shape=[shp, shp],
        scratch_shapes=[pltpu.VMEM((ts + 2 * SUBLANES, LRU_WIDTH), F32), slab, slab, slab, slab, slab, slab,
                        carry, carry],
        compiler_params=_cparams("arbitrary"),
        name="lru",
    )(pd, pd, pd, pd, pd, pd, cw, cb, wg, bg, lam)


def _gelu_tanh(x):
    return 0.5 * x * (1.0 + jnp.tanh(np.sqrt(2.0 / np.pi).astype(np.float32) * (x + 0.044715 * (x * x * x))))


def _branch_out_kernel(o0, o1, o2, l0, l1, l2, hf_ref, hr_ref, gd_ref, ya_ref, yd_ref, *nat_refs, tm):
    def natural(ref, g, buf):
        dil = DIL_CONFIGS[g][1]
        if dil == 1:
            return ref[0, 0]
        chunks = BRANCH_W // LANES
        for r in range(dil):
            for c in range(chunks):
                buf[c, pl.ds(r, tm // dil, stride=dil), :] = ref[0, r, :, c * LANES:(c + 1) * LANES]
        return jnp.concatenate([buf[c] for c in range(chunks)], axis=1)

    outs = (o0[0, 0], natural(o1, 1, nat_refs[0]), natural(o2, 2, nat_refs[1]))
    lse = (l0[0, 0], natural(l1, 1, nat_refs[2]), natural(l2, 2, nat_refs[3]))
    m = jnp.maximum(jnp.maximum(lse[0], lse[1]), lse[2])
    e = [jnp.exp(v - m) for v in lse]
    tot = e[0] + e[1] + e[2]
    ya = (e[0] / tot) * outs[0] + (e[1] / tot) * outs[1] + (e[2] / tot) * outs[2]
    ya_ref[...] = ya.astype(BF16)
    yd_ref[...] = ((hf_ref[...] + hr_ref[...]) * _gelu_tanh(gd_ref[...])).astype(BF16)


def _branch_out(oa, la, hf, hr, pd, seq):
    t = hf.shape[0]
    tm = _pick(seq, TILE_ROWS)
    nseq = seq // tm
    blk = pl.BlockSpec((tm, BRANCH_W), lambda i: (i, 0))
    cls = [pl.BlockSpec((1, dil, tm // dil, BRANCH_W), lambda i: (i // nseq, 0, i % nseq, 0))
           for _, dil in DIL_CONFIGS]
    shp = jax.ShapeDtypeStruct((t, BRANCH_W), BF16)
    return pl.pallas_call(
        functools.partial(_branch_out_kernel, tm=tm),
        grid=(t // tm,),
        in_specs=cls + cls + [blk, blk, pl.BlockSpec((tm, BRANCH_W), lambda i: (i, 1))],
        out_specs=[blk, blk],
        out_shape=[shp, shp],
        scratch_shapes=[pltpu.VMEM((BRANCH_W // LANES, tm, LANES), F32)] * 4,
        compiler_params=_cparams("parallel"),
        name="branch_out",
    )(*oa, *la, hf, hr, pd)


def _gate_merge_out_kernel(x_ref, h_ref, ya, yb, yc, yd, g0, g1, g2, g3, wb_ref, wo_ref, o_ref):
    j = pl.program_id(1)
    last = pl.num_programs(1) - 1

    def partial_out():
        h = h_ref[...]
        merged = None
        for nbr, (y_ref, wg_ref) in enumerate(zip((ya, yb, yc, yd), (g0, g1, g2, g3))):
            gate = jax.nn.sigmoid(jnp.dot(h, wg_ref[...], preferred_element_type=F32))
            term = gate * jnp.dot(y_ref[...], wb_ref[nbr], preferred_element_type=F32)
            merged = term if merged is None else merged + term
        return jnp.dot(merged.astype(BF16), wo_ref[...], preferred_element_type=F32)

    @pl.when(j == 0)
    def _():
        o_ref[...] = partial_out()

    @pl.when((j > 0) & (j < last))
    def _():
        o_ref[...] += partial_out()

    @pl.when(j == last)
    def _():
        o_ref[...] = x_ref[...] + (o_ref[...] + partial_out())


def _gate_merge_out(x, h, ys, wg, wb, wo):
    t = x.shape[0]
    tm, tn = _pick(t, TILE_ROWS), MERGE_TILE_COLS
    nj = D_MODEL // tn
    row = lambda i, j: (i, 0)
    gate_specs = [pl.BlockSpec((D_MODEL, tn), functools.partial(lambda i, j, nbr: (0, nbr * nj + j), nbr=nbr))
                  for nbr in range(N_BRANCH)]
    return pl.pallas_call(
        _gate_merge_out_kernel,
        grid=(t // tm, nj),
        in_specs=[pl.BlockSpec((tm, D_MODEL), row), pl.BlockSpec((tm, D_MODEL), row)]
                 + [pl.BlockSpec((tm, BRANCH_W), row)] * N_BRANCH + gate_specs
                 + [pl.BlockSpec((N_BRANCH, BRANCH_W, tn), lambda i, j: (0, 0, j)),
                    pl.BlockSpec((tn, D_MODEL), lambda i, j: (j, 0))],
        out_specs=pl.BlockSpec((tm, D_MODEL), row),
        out_shape=jax.ShapeDtypeStruct((t, D_MODEL), F32),
        compiler_params=_cparams("parallel", "arbitrary"),
        name="gate_merge_out",
    )(x, h, *ys, wg, wg, wg, wg, wb, wo)


def _block_diag(w):
    eye = jnp.eye(LRU_BLOCKS, dtype=w.dtype)
    return jnp.einsum('nef,nm->nemf', w, eye).reshape(LRU_WIDTH, LRU_WIDTH)


def _prep_layer(l, ffn1_norm, ffn1_w1, ffn1_w3, ffn1_w2, mix_norm, w_in, mla_q_norm, mla_w_uq, mla_kv_norm,
                mla_w_ukv, lru_conv_w, lru_conv_b, lru_w_a, lru_b_a, lru_w_x, lru_b_x, lru_lambda, sink_logits,
                w_branch, w_out, ffn2_norm, ffn2_w1, ffn2_w3, ffn2_w2):
    row = lambda v: v.reshape(1, -1)
    wi = w_in[l]
    c0, c1, c2, c3 = A_COLS, A_COLS + B_COLS, A_COLS + B_COLS + C_COLS, A_COLS + B_COLS + C_COLS + D_COLS
    w_a = wi[:, :c0].reshape(D_MODEL, 3, N_DIL, A_HEADS * HEAD_DIM).transpose(0, 2, 1, 3).reshape(D_MODEL, A_COLS)
    wb = wi[:, c0:c1]
    zeros = lambda n: jnp.zeros((D_MODEL, n), wi.dtype)
    w_b = jnp.concatenate([wb[:, :Q_LORA + KV_LORA], zeros(QK_NOPE), wb[:, Q_LORA + KV_LORA:],
                           zeros(SLOT - QK_NOPE - QK_ROPE)], axis=1)
    w_cbd = jnp.concatenate([wi[:, c1:c2], w_b, wi[:, c2:c3]], axis=1)
    wq = mla_w_uq[l].reshape(Q_LORA, B_HEADS, QK_NOPE + QK_ROPE)
    wq = jnp.pad(wq, ((0, 0), (0, 0), (0, SLOT - QK_NOPE - QK_ROPE))).reshape(Q_LORA, B_HEADS * SLOT)
    wkv = mla_w_ukv[l].reshape(KV_LORA, B_HEADS, QK_NOPE + V_DIM)
    wk = jnp.pad(wkv[:, :, :QK_NOPE], ((0, 0), (0, 0), (0, SLOT - QK_NOPE))).reshape(KV_LORA, B_HEADS * SLOT)
    wv = wkv[:, :, QK_NOPE:].reshape(KV_LORA, B_HEADS * V_DIM)
    wg = jnp.stack([jnp.concatenate([_block_diag(lru_w_a[l, d]), _block_diag(lru_w_x[l, d])], axis=1)
                    for d in range(2)])
    bg = jnp.concatenate([lru_b_a[l], lru_b_x[l]], axis=1)
    return dict(
        ffn1=(row(ffn1_norm[l]), ffn1_w1[l].astype(BF16), ffn1_w3[l].astype(BF16), ffn1_w2[l].astype(BF16)),
        ffn2=(row(ffn2_norm[l]), ffn2_w1[l].astype(BF16), ffn2_w3[l].astype(BF16), ffn2_w2[l].astype(BF16)),
        mix_norm=row(mix_norm[l]),
        w_a=w_a.astype(BF16), w_cbd=w_cbd.astype(BF16), w_g=wi[:, c3:].astype(BF16),
        qn=row(mla_q_norm[l]), kvn=row(mla_kv_norm[l]),
        wq=wq.astype(BF16), wk=wk.astype(BF16), wv=wv.astype(BF16),
        conv_w=lru_conv_w[l], conv_b=row(lru_conv_b[l]), wg=wg.astype(BF16), bg=bg, lam=lru_lambda[l],
        sink=sink_logits[l], w_branch=w_branch[l].astype(BF16), w_out=w_out[l].astype(BF16),
    )


def _rope_slot_tables(seq):
    inv = ROPE_THETA ** (-jnp.arange(0, QK_ROPE, 2, dtype=F32) / QK_ROPE)
    ang = jnp.arange(seq, dtype=F32)[:, None] * inv[None, :]
    cos, sin = jnp.cos(ang), jnp.sin(ang)
    scale = (QK_NOPE + QK_ROPE) ** -0.5 * np.log2(np.e)
    z = lambda n: jnp.zeros((seq, n), F32)
    tail = SLOT - QK_NOPE - QK_ROPE
    cos_q = jnp.concatenate([jnp.full((seq, QK_NOPE), scale, F32), cos * scale, cos * scale, z(tail)], axis=1)
    sin_q = jnp.concatenate([z(QK_NOPE), -sin * scale, sin * scale, z(tail)], axis=1)
    cos_k = jnp.concatenate([z(QK_NOPE), cos, cos, z(tail)], axis=1)
    sin_k = jnp.concatenate([z(QK_NOPE), -sin, sin, z(tail)], axis=1)
    return cos_q, sin_q, cos_k, sin_k


def _layer(x, w, batch, seq, final_g, last_layer):
    t = batch * seq
    x = _ffn(x, *w['ffn1'], final_g, False)
    *pas, h = _proj_a(x, w['mix_norm'], w['w_a'], batch, seq)
    pc, pd, qt, k, vt = _proj_bcd(h, w['w_cbd'], w['qn'], w['kvn'], w['wq'], w['wk'], w['wv'],
                                 _rope_slot_tables(seq), batch, seq)

    oa, la = [], []
    for pa, (window, dil) in zip(pas, DIL_CONFIGS):
        o, lse = _band_attention(
            pa, q_col=0, k_col=1, v_col=2, kv_width=A_HEADS * HEAD_DIM,
            n_heads=A_HEADS, n_kv=A_HEADS, radius=window // (2 * dil), step=dil, sink=None,
            out_dtype=F32, emit_lse=True)
        oa.append(o)
        la.append(lse)

    yb = _mla_attn(qt, k.reshape(batch, seq, -1), vt, batch, seq)

    kvw = C_KV_HEADS * HEAD_DIM
    yc = _band_attention(
        pc.reshape(batch, 1, seq, C_COLS),
        q_col=0, k_col=C_HEADS * HEAD_DIM // kvw, v_col=C_HEADS * HEAD_DIM // kvw + 1, kv_width=kvw,
        n_heads=C_HEADS, n_kv=C_KV_HEADS, radius=C_RADIUS, step=1, sink=w['sink'],
        out_dtype=BF16, emit_lse=False)[0]

    hf, hr = _lru(pd.reshape(batch, seq, D_COLS), w['conv_w'], w['conv_b'], w['wg'], w['bg'], w['lam'])

    ya, yd = _branch_out(oa, la, hf.reshape(t, LRU_WIDTH), hr.reshape(t, LRU_WIDTH), pd, seq)
    x = _gate_merge_out(x, h, (ya, yb.reshape(t, BRANCH_W), yc.reshape(t, BRANCH_W), yd),
                        w['w_g'], w['w_branch'], w['w_out'])
    return _ffn(x, *w['ffn2'], final_g, last_layer)


def _trunk(x, layers, final_norm):
    batch, seq, _ = x.shape
    h = x.reshape(batch * seq, D_MODEL)
    fg = final_norm.reshape(1, -1)
    for l, w in enumerate(layers):
        h = _layer(h, w, batch, seq, fg, l == len(layers) - 1)
    return h.reshape(batch, seq, D_MODEL)


def kernel(x_prompt, x_sample, ffn1_norm, ffn1_w1, ffn1_w3, ffn1_w2, mix_norm, w_in, mla_q_norm, mla_w_uq, mla_kv_norm, mla_w_ukv, lru_conv_w, lru_conv_b, lru_w_a, lru_b_a, lru_w_x, lru_b_x, lru_lambda, sink_logits, w_branch, w_out, ffn2_norm, ffn2_w1, ffn2_w3, ffn2_w2, final_norm):
    layers = [_prep_layer(l, ffn1_norm, ffn1_w1, ffn1_w3, ffn1_w2, mix_norm, w_in, mla_q_norm, mla_w_uq,
                          mla_kv_norm, mla_w_ukv, lru_conv_w, lru_conv_b, lru_w_a, lru_b_a, lru_w_x, lru_b_x,
                          lru_lambda, sink_logits, w_branch, w_out, ffn2_norm, ffn2_w1, ffn2_w3, ffn2_w2)
              for l in range(DEPTH)]
    return (_trunk(x_prompt, layers, final_norm), _trunk(x_sample, layers, final_norm))
```
